```python
import jax, jax.numpy as jnp
from jax import lax
import numpy as np

D_MODEL = 2048
BATCH = 4
SEQ = 2048
DEPTH = 2
DEC_BATCH = 128
DEC_SEQ = 8
PAST_LEN = 16384
PAGE_SIZE = 128

N_META = 16
N_MIXERS = 2
N_MLSTM_LAYERS = (DEPTH + 1) // 2
N_POOL_LAYERS = DEPTH // 2
MLSTM_HEADS = 4
MLSTM_DQK = D_MODEL // (2 * MLSTM_HEADS)
MLSTM_DV = D_MODEL // MLSTM_HEADS
MLSTM_CHUNK = 64
MLSTM_PROJ = 2 * MLSTM_HEADS * MLSTM_DQK + 2 * MLSTM_HEADS * MLSTM_DV + 2 * MLSTM_HEADS
MLSTM_SPLITS = (MLSTM_HEADS * MLSTM_DQK,
                2 * MLSTM_HEADS * MLSTM_DQK,
                2 * MLSTM_HEADS * MLSTM_DQK + MLSTM_HEADS * MLSTM_DV,
                2 * MLSTM_HEADS * MLSTM_DQK + 2 * MLSTM_HEADS * MLSTM_DV,
                2 * MLSTM_HEADS * MLSTM_DQK + 2 * MLSTM_HEADS * MLSTM_DV + MLSTM_HEADS)
POOL_WINDOWS = (2, 4, 8, 16)
POOL_GROUPS = len(POOL_WINDOWS)
POOL_GROUP_DIM = D_MODEL // POOL_GROUPS
POOL_BUF = max(POOL_WINDOWS) - 1
D_FF = 4 * D_MODEL
EPS = 1e-6
GATE_PAD = -1e30

kernel_name = 'hybrid_mlstm_pool_decoder_step'


def rmsnorm(x, g):
    xf = x.astype(jnp.float32)
    y = xf * lax.rsqrt(jnp.mean(xf * xf, axis=-1, keepdims=True) + EPS) * g.astype(jnp.float32)
    return y.astype(x.dtype)


def mlstm_chunkwise(q, k, v, log_i, log_f, state, chunk):
    B, T, H, _ = q.shape
    L = min(chunk, T)
    pad = (-T) % L
    if pad:
        pw = ((0, 0), (0, pad), (0, 0), (0, 0))
        q, k, v = jnp.pad(q, pw), jnp.pad(k, pw), jnp.pad(v, pw)
        log_i = jnp.pad(log_i, ((0, 0), (0, pad), (0, 0)), constant_values=GATE_PAD)
        log_f = jnp.pad(log_f, ((0, 0), (0, pad), (0, 0)))
    nc = (T + pad) // L

    def to_chunks(a):
        return jnp.moveaxis(a.reshape(B, nc, L, *a.shape[2:]), 1, 0)

    causal = jnp.tril(jnp.ones((L, L), dtype=bool))

    def step(carry, xs):
        C, n, m = carry
        qc, kc, vc, lic, lfc = xs
        b = jnp.cumsum(lfc, axis=1).transpose(0, 2, 1)
        li = lic.transpose(0, 2, 1)
        dmat = jnp.where(causal, b[..., :, None] - b[..., None, :] + li[..., None, :], -jnp.inf)
        m_inter = b + m[..., None]
        m_t = jnp.maximum(m_inter, jnp.max(dmat, axis=-1))
        w_inter = jnp.exp(m_inter - m_t)
        scores = jnp.einsum('blhd,bshd->bhls', qc, kc) * jnp.exp(dmat - m_t[..., None])
        num = (jnp.einsum('bhls,bshe->blhe', scores, vc)
               + jnp.einsum('blhd,bhde->blhe', qc, C) * w_inter.transpose(0, 2, 1)[..., None])
        den = jnp.sum(scores, axis=-1) + w_inter * jnp.einsum('blhd,bhd->bhl', qc, n)
        denom = jnp.maximum(jnp.abs(den), jnp.exp(-m_t)).transpose(0, 2, 1)[..., None]
        h = num / denom
        m_new = m_t[..., -1]
        decay = jnp.exp(b[..., -1:] - b + li - m_new[..., None])
        carry_w = jnp.exp(b[..., -1] + m - m_new)
        C_new = carry_w[..., None, None] * C + jnp.einsum('bhs,bshd,bshe->bhde', decay, kc, vc)
        n_new = carry_w[..., None] * n + jnp.einsum('bhs,bshd->bhd', decay, kc)
        return (C_new, n_new, m_new), h

    state, hs = lax.scan(step, state, (to_chunks(q), to_chunks(k), to_chunks(v),
                                       to_chunks(log_i), to_chunks(log_f)))
    h = jnp.moveaxis(hs, 0, 1).reshape(B, nc * L, H, MLSTM_DV)[:, :T]
    return h, state


def mlstm_mixer(xn, c0, n0, m0, segments, w_in, b_i, b_f, head_g, w_out):
    B, T, _ = xn.shape
    H = MLSTM_HEADS
    proj = xn @ w_in
    q, k, v, o, gi, gf = jnp.split(proj, MLSTM_SPLITS, axis=-1)
    q = q.reshape(B, T, H, MLSTM_DQK).astype(jnp.float32)
    k = k.reshape(B, T, H, MLSTM_DQK).astype(jnp.float32) * (MLSTM_DQK ** -0.5)
    v = v.reshape(B, T, H, MLSTM_DV).astype(jnp.float32)
    log_i = gi.astype(jnp.float32) + b_i.astype(jnp.float32)
    log_f = jax.nn.log_sigmoid(gf.astype(jnp.float32) + b_f.astype(jnp.float32))
    state = (c0.astype(jnp.float32), n0.astype(jnp.float32), m0.astype(jnp.float32))
    hs = []
    start = 0
    for length, chunk in segments:
        sl = slice(start, start + length)
        h, state = mlstm_chunkwise(q[:, sl], k[:, sl], v[:, sl], log_i[:, sl], log_f[:, sl], state, chunk)
        hs.append(h)
        start += length
    h = jnp.concatenate(hs, axis=1)
    hf = h * lax.rsqrt(jnp.mean(h * h, axis=-1, keepdims=True) + EPS) \
        * head_g.astype(jnp.float32).reshape(H, MLSTM_DV)
    gate = jax.nn.sigmoid(o.astype(jnp.float32)).reshape(B, T, H, MLSTM_DV)
    out = (hf * gate).reshape(B, T, H * MLSTM_DV).astype(xn.dtype) @ w_out
    return out, state


def pool_mixer(xn, prefix, start_pos, w_in, w_group, scale, w_out):
    B, T, _ = xn.shape
    u = xn @ w_in
    ext = jnp.concatenate([prefix.astype(u.dtype), u], axis=1)
    cs = jnp.pad(jnp.cumsum(ext.astype(jnp.float32), axis=1), ((0, 0), (1, 0), (0, 0)))
    pos = (start_pos + jnp.arange(T)).astype(jnp.float32)
    hi = cs[:, POOL_BUF + 1:POOL_BUF + 1 + T]
    outs = []
    for g, w in enumerate(POOL_WINDOWS):
        sl = slice(g * POOL_GROUP_DIM, (g + 1) * POOL_GROUP_DIM)
        total = hi[..., sl] - cs[:, POOL_BUF + 1 - w:POOL_BUF + 1 - w + T, sl]
        cnt = jnp.minimum(jnp.float32(w), pos + 1.0)[None, :, None]
        outs.append(total / cnt)
    pooled = jnp.concatenate(outs, axis=-1) - u.astype(jnp.float32)
    mixed = jnp.einsum('btgc,gcd->btgd', pooled.reshape(B, T, POOL_GROUPS, POOL_GROUP_DIM),
                       w_group.astype(jnp.float32)).reshape(B, T, D_MODEL)
    out = (mixed * scale.astype(jnp.float32)).astype(xn.dtype) @ w_out
    return out, ext[:, -POOL_BUF:]


def sq_relu_mlp(xn, w_up, w_down):
    return jnp.square(jax.nn.relu(xn @ w_up)) @ w_down


def setup_inputs(seed: int = 0) -> dict:
    key = jax.random.key(seed)
    ks = jax.random.split(key, 24)
    f32 = jnp.float32
    NA, NB, H = N_MLSTM_LAYERS, N_POOL_LAYERS, MLSTM_HEADS

    def nrm(k, shape, s):
        return jax.random.normal(k, shape, f32) * s

    return {
        'x_prompt': nrm(ks[0], (BATCH, SEQ, D_MODEL), 1.0),
        'x_sample': nrm(ks[1], (DEC_BATCH, DEC_SEQ, D_MODEL), 1.0),
        'state_mlstm_c': nrm(ks[2], (NA, DEC_BATCH, H, MLSTM_DQK, MLSTM_DV), 0.1),
        'state_mlstm_n': nrm(ks[3], (NA, DEC_BATCH, H, MLSTM_DQK), 0.1),
        'state_mlstm_m': nrm(ks[4], (NA, DEC_BATCH, H), 0.5),
        'state_pool': nrm(ks[5], (NB, DEC_BATCH, POOL_BUF, D_MODEL), 1.0),
        'meta_tokens': nrm(ks[6], (N_META, D_MODEL), 1.0),
        'norm_mix_pre': 1.0 + nrm(ks[7], (DEPTH, D_MODEL), 0.02),
        'norm_mix_post': 1.0 + nrm(ks[8], (DEPTH, D_MODEL), 0.02),
        'norm_ffn_pre': 1.0 + nrm(ks[9], (DEPTH, D_MODEL), 0.02),
        'norm_ffn_post': 1.0 + nrm(ks[10], (DEPTH, D_MODEL), 0.02),
        'mlstm_w_in': nrm(ks[11], (NA, D_MODEL, MLSTM_PROJ), D_MODEL ** -0.5),
        'mlstm_b_i': nrm(ks[12], (NA, H), 0.1),
        'mlstm_b_f': jnp.linspace(3.0, 6.0, H, dtype=f32)[None, :] + nrm(ks[13], (NA, H), 0.1),
        'mlstm_head_norm': 1.0 + nrm(ks[14], (NA, H * MLSTM_DV), 0.02),
        'mlstm_w_out': nrm(ks[15], (NA, H * MLSTM_DV, D_MODEL), (H * MLSTM_DV) ** -0.5),
        'pool_w_in': nrm(ks[16], (NB, D_MODEL, D_MODEL), D_MODEL ** -0.5),
        'pool_w_group': nrm(ks[17], (NB, POOL_GROUPS, POOL_GROUP_DIM, POOL_GROUP_DIM), POOL_GROUP_DIM ** -0.5),
        'pool_scale': 1.0 + nrm(ks[18], (NB, D_MODEL), 0.1),
        'pool_w_out': nrm(ks[19], (NB, D_MODEL, D_MODEL), D_MODEL ** -0.5),
        'ffn_w_up': nrm(ks[20], (DEPTH, D_MODEL, D_FF), D_MODEL ** -0.5),
        'ffn_w_down': nrm(ks[21], (DEPTH, D_FF, D_MODEL), D_FF ** -0.5),
    }


def reference(x_prompt, x_sample, state_mlstm_c, state_mlstm_n, state_mlstm_m, state_pool,
              meta_tokens, norm_mix_pre, norm_mix_post, norm_ffn_pre, norm_ffn_post,
              mlstm_w_in, mlstm_b_i, mlstm_b_f, mlstm_head_norm, mlstm_w_out,
              pool_w_in, pool_w_group, pool_scale, pool_w_out, ffn_w_up, ffn_w_down):
    B, S, _ = x_prompt.shape
    DB, DS, _ = x_sample.shape
    H = MLSTM_HEADS
    meta = jnp.broadcast_to(meta_tokens.astype(x_prompt.dtype)[None], (B, N_META, D_MODEL))
    hp = jnp.concatenate([meta, x_prompt], axis=1)
    hs = x_sample
    c_p, n_p, m_p, pool_p = [], [], [], []
    c_s, n_s, m_s, pool_s = [], [], [], []
    for layer in range(DEPTH):
        j = layer // N_MIXERS
        xn_p = rmsnorm(hp, norm_mix_pre[layer])
        xn_s = rmsnorm(hs, norm_mix_pre[layer])
        if layer % N_MIXERS == 0:
            wts = (mlstm_w_in[j], mlstm_b_i[j], mlstm_b_f[j], mlstm_head_norm[j], mlstm_w_out[j])
            zc = jnp.zeros((B, H, MLSTM_DQK, MLSTM_DV), jnp.float32)
            zn = jnp.zeros((B, H, MLSTM_DQK), jnp.float32)
            zm = jnp.zeros((B, H), jnp.float32)
            mix_p, (cp, np_, mp) = mlstm_mixer(xn_p, zc, zn, zm,
                                               [(N_META, N_META), (S, MLSTM_CHUNK)], *wts)
            mix_s, (cs_, ns_, ms_) = mlstm_mixer(xn_s, state_mlstm_c[j], state_mlstm_n[j], state_mlstm_m[j],
                                                 [(DS, MLSTM_CHUNK)], *wts)
            c_p.append(cp.astype(state_mlstm_c.dtype))
            n_p.append(np_.astype(state_mlstm_n.dtype))
            m_p.append(mp.astype(state_mlstm_m.dtype))
            c_s.append(cs_.astype(state_mlstm_c.dtype))
            n_s.append(ns_.astype(state_mlstm_n.dtype))
            m_s.append(ms_.astype(state_mlstm_m.dtype))
        else:
            wts = (pool_w_in[j], pool_w_group[j], pool_scale[j], pool_w_out[j])
            mix_p, bp = pool_mixer(xn_p, jnp.zeros((B, POOL_BUF, D_MODEL), xn_p.dtype), 0, *wts)
            mix_s, bs = pool_mixer(xn_s, state_pool[j], PAST_LEN, *wts)
            pool_p.append(bp.astype(state_pool.dtype))
            pool_s.append(bs.astype(state_pool.dtype))
        hp = hp + rmsnorm(mix_p, norm_mix_post[layer])
        hs = hs + rmsnorm(mix_s, norm_mix_post[layer])
        hp = hp + rmsnorm(sq_relu_mlp(rmsnorm(hp, norm_ffn_pre[layer]), ffn_w_up[layer], ffn_w_down[layer]),
                          norm_ffn_post[layer])
        hs = hs + rmsnorm(sq_relu_mlp(rmsnorm(hs, norm_ffn_pre[layer]), ffn_w_up[layer], ffn_w_down[layer]),
                          norm_ffn_post[layer])
    y_prompt = hp[:, N_META:]
    y_sample = hs
    new_c_prompt = jnp.stack(c_p, 0)
    new_n_prompt = jnp.stack(n_p, 0)
    new_m_prompt = jnp.stack(m_p, 0)
    new_pool_prompt = jnp.stack(pool_p, 0)
    new_c_sample = jnp.stack(c_s, 0)
    new_n_sample = jnp.stack(n_s, 0)
    new_m_sample = jnp.stack(m_s, 0)
    new_pool_sample = jnp.stack(pool_s, 0)
    return (y_prompt, y_sample, new_c_prompt, new_n_prompt, new_m_prompt, new_pool_prompt,
            new_c_sample, new_n_sample, new_m_sample, new_pool_sample)
```

```python
import functools

import jax
import jax.numpy as jnp
from jax import lax
from jax.experimental import pallas as pl
from jax.experimental.pallas import tpu as pltpu

D = 2048
N_META = 16
HEADS = 4
DQK = 256
DV = 512
D_FF = 4 * D
POOL_WINDOWS = (2, 4, 8, 16)
POOL_GROUP_DIM = D // len(POOL_WINDOWS)
POOL_BUF = 15
PAST_LEN = 16384
EPS = 1e-6
K_SCALE = DQK ** -0.5
PROJ_MAIN = 2 * HEADS * DQK + 2 * HEADS * DV
GATE_LANES = 128

VMEM_LIMIT = 56 * 1024 * 1024

F32 = jnp.float32
BF16 = jnp.bfloat16


def _params(semantics):
    return pltpu.CompilerParams(dimension_semantics=semantics, vmem_limit_bytes=VMEM_LIMIT)


def _rmsnorm(x, g):
    return x * lax.rsqrt(jnp.mean(x * x, axis=-1, keepdims=True) + EPS) * g


def _dot(a, b):
    return jnp.dot(a, b, preferred_element_type=F32)


def _norm_matmul_kernel(x_ref, g_ref, w_ref, o_ref, xn_ref):
    @pl.when(pl.program_id(1) == 0)
    def _():
        xn_ref[...] = _rmsnorm(x_ref[...], g_ref[...]).astype(BF16)

    o_ref[...] = _dot(xn_ref[...], w_ref[...]).astype(o_ref.dtype)


def _norm_matmul_gates_kernel(x_ref, g_ref, w_ref, wg_ref, bg_ref, o_ref, gate_ref, xn_ref):
    @pl.when(pl.program_id(1) == 0)
    def _():
        xn = _rmsnorm(x_ref[...], g_ref[...]).astype(BF16)
        xn_ref[...] = xn
        z = _dot(xn, wg_ref[...]) + bg_ref[...]
        lane = lax.broadcasted_iota(jnp.int32, z.shape, 1)
        log_sig = jnp.minimum(z, 0.0) - jnp.log1p(jnp.exp(-jnp.abs(z)))
        gate_ref[...] = jnp.where(lane >= HEADS, log_sig, z)

    o_ref[...] = _dot(xn_ref[...], w_ref[...]).astype(o_ref.dtype)


def _norm_matmul(x, g, w, *, tm, tn, out_dtype, gates=None):
    rows, n = x.shape[0], w.shape[1]
    grid = (rows // tm, n // tn)
    x_spec = pl.BlockSpec((tm, D), lambda i, j: (i, 0))
    g_spec = pl.BlockSpec((1, D), lambda i, j: (0, 0))
    w_spec = pl.BlockSpec((D, tn), lambda i, j: (0, j))
    o_spec = pl.BlockSpec((tm, tn), lambda i, j: (i, j))
    scratch = [pltpu.VMEM((tm, D), BF16)]
    if gates is None:
        return pl.pallas_call(
            _norm_matmul_kernel,
            out_shape=jax.ShapeDtypeStruct((rows, n), out_dtype),
            grid=grid, in_specs=[x_spec, g_spec, w_spec], out_specs=o_spec,
            scratch_shapes=scratch, compiler_params=_params(("parallel", "arbitrary")),
            name="norm_matmul",
        )(x, g, w)
    wg, bg = gates
    return pl.pallas_call(
        _norm_matmul_gates_kernel,
        out_shape=(jax.ShapeDtypeStruct((rows, n), out_dtype),
                   jax.ShapeDtypeStruct((rows, GATE_LANES), F32)),
        grid=grid,
        in_specs=[x_spec, g_spec, w_spec,
                  pl.BlockSpec((D, GATE_LANES), lambda i, j: (0, 0)),
                  pl.BlockSpec((1, GATE_LANES), lambda i, j: (0, 0))],
        out_specs=(o_spec, pl.BlockSpec((tm, GATE_LANES), lambda i, j: (i, 0))),
        scratch_shapes=scratch, compiler_params=_params(("parallel", "arbitrary")),
        name="norm_matmul_gates",
    )(x, g, w, wg, bg)


def _matmul_norm_res_kernel(a_ref, w_ref, g_ref, h_ref, o_ref):
    y = _dot(a_ref[...], w_ref[...])
    o_ref[...] = h_ref[...] + _rmsnorm(y, g_ref[...])


def _matmul_norm_res(a, w, g, h, *, tm):
    rows = a.shape[0]
    return pl.pallas_call(
        _matmul_norm_res_kernel,
        out_shape=jax.ShapeDtypeStruct((rows, D), F32),
        grid=(rows // tm,),
        in_specs=[pl.BlockSpec((tm, D), lambda i: (i, 0)),
                  pl.BlockSpec((D, D), lambda i: (0, 0)),
                  pl.BlockSpec((1, D), lambda i: (0, 0)),
                  pl.BlockSpec((tm, D), lambda i: (i, 0))],
        out_specs=pl.BlockSpec((tm, D), lambda i: (i, 0)),
        compiler_params=_params(("parallel",)),
        name="matmul_norm_res",
    )(a, w, g, h)


def _ffn_kernel(h_ref, gpre_ref, wup_ref, wdown_ref, gpost_ref, o_ref, xn_ref, acc_ref):
    f = pl.program_id(1)

    @pl.when(f == 0)
    def _():
        xn_ref[...] = _rmsnorm(h_ref[...], gpre_ref[...]).astype(BF16)
        acc_ref[...] = jnp.zeros_like(acc_ref)

    a = jnp.maximum(_dot(xn_ref[...], wup_ref[...]), 0.0)
    acc_ref[...] += _dot((a * a).astype(BF16), wdown_ref[...])

    @pl.when(f == pl.num_programs(1) - 1)
    def _():
        o_ref[...] = h_ref[...] + _rmsnorm(acc_ref[...], gpost_ref[...])


def _ffn(h, g_pre, w_up, w_down, g_post, *, tm, tf):
    rows = h.shape[0]
    return pl.pallas_call(
        _ffn_kernel,
        out_shape=jax.ShapeDtypeStruct((rows, D), F32),
        grid=(rows // tm, D_FF // tf),
        in_specs=[pl.BlockSpec((tm, D), lambda i, f: (i, 0)),
                  pl.BlockSpec((1, D), lambda i, f: (0, 0)),
                  pl.BlockSpec((D, tf), lambda i, f: (0, f)),
                  pl.BlockSpec((tf, D), lambda i, f: (f, 0)),
                  pl.BlockSpec((1, D), lambda i, f: (0, 0))],
        out_specs=pl.BlockSpec((tm, D), lambda i, f: (i, 0)),
        scratch_shapes=[pltpu.VMEM((tm, D), BF16), pltpu.VMEM((tm, D), F32)],
        compiler_params=_params(("parallel", "arbitrary")),
        name="ffn",
    )(h, g_pre, w_up, w_down, g_post)


def _mlstm_chunk(q, k, v, li_row, lf_row, li_col, lf_col, c_state, n_row, m):
    L = q.shape[0]
    t_idx = lax.broadcasted_iota(jnp.int32, (L, L), 0)
    s_idx = lax.broadcasted_iota(jnp.int32, (L, L), 1)
    causal = s_idx <= t_idx
    b_col = jnp.sum(jnp.where(causal, lf_row, 0.0), axis=1, keepdims=True)
    b_row = jnp.sum(jnp.where(t_idx <= s_idx, lf_col, 0.0), axis=0, keepdims=True)
    b_tot = b_col[L - 1:L, :]
    dmat = jnp.where(causal, b_col - b_row + li_row, -jnp.inf)
    m_inter = b_col + m
    m_t = jnp.maximum(m_inter, jnp.max(dmat, axis=1, keepdims=True))
    p = jnp.exp(dmat - m_t)
    w_inter = jnp.exp(m_inter - m_t)
    qk = lax.dot_general(q, k, (((1,), (1,)), ((), ())), preferred_element_type=F32)
    scores = qk * (p * K_SCALE)
    num = _dot(scores.astype(BF16), v) + _dot(q, c_state.astype(BF16)) * w_inter
    qn = jnp.sum(q.astype(F32) * n_row, axis=1, keepdims=True)
    den = jnp.sum(scores, axis=1, keepdims=True) + w_inter * qn
    h = num / jnp.maximum(jnp.abs(den), jnp.exp(-m_t))
    m_new = m_t[L - 1:L, :]
    decay = jnp.exp(b_tot - b_col + li_col - m_new)
    carry = jnp.exp(b_tot + m - m_new)
    kd = k.astype(F32) * (decay * K_SCALE)
    c_new = carry * c_state + lax.dot_general(
        kd.astype(BF16), v, (((0,), (0,)), ((), ())), preferred_element_type=F32)
    n_new = carry * n_row + jnp.sum(kd, axis=0, keepdims=True)
    return h, c_new, n_new, m_new


def _mlstm_kernel(qk_ref, v_ref, o_ref, gcol_ref, grow_ref, hg_ref, c0_ref, n0_ref, m0_ref,
                  out_ref, c_out_ref, n_out_ref, m_out_ref, c_scr, n_scr, m_scr, *, seqs, length):
    c_idx = pl.program_id(1)

    @pl.when(c_idx == 0)
    def _():
        c_scr[...] = c0_ref[...]
        n_scr[...] = n0_ref[...]
        m_scr[...] = m0_ref[...]

    for s in range(seqs):
        rows = slice(s * length, (s + 1) * length)
        gcol = gcol_ref[rows, :]
        grow = grow_ref[0, s]
        for hd in range(HEADS):
            q = qk_ref[rows, hd * DQK:(hd + 1) * DQK]
            k = qk_ref[rows, (HEADS + hd) * DQK:(HEADS + hd + 1) * DQK]
            v = v_ref[rows, hd * DV:(hd + 1) * DV]
            h, c_new, n_new, m_new = _mlstm_chunk(
                q, k, v,
                grow[hd:hd + 1, :], grow[HEADS + hd:HEADS + hd + 1, :],
                gcol[:, hd:hd + 1], gcol[:, HEADS + hd:HEADS + hd + 1],
                c_scr[s, hd], n_scr[s, hd:hd + 1, :], m_scr[s, hd:hd + 1, 0:1])
            c_scr[s, hd] = c_new
            n_scr[s, hd:hd + 1, :] = n_new
            m_scr[s, hd:hd + 1, :] = jnp.broadcast_to(m_new, (1, GATE_LANES))
            hn = _rmsnorm(h, hg_ref[:, hd * DV:(hd + 1) * DV])
            gate = jax.nn.sigmoid(o_ref[rows, hd * DV:(hd + 1) * DV].astype(F32))
            out_ref[rows, hd * DV:(hd + 1) * DV] = (hn * gate).astype(out_ref.dtype)

    @pl.when(c_idx == pl.num_programs(1) - 1)
    def _():
        c_out_ref[...] = c_scr[...]
        n_out_ref[...] = n_scr[...]
        m_out_ref[...] = m_scr[...]


def _mlstm(proj, gcol, grow, head_g, c0, n0, m0, *, row0, n_blocks, seqs, length, n_chunks):
    assert seqs == 1 or n_chunks == 1
    blk = seqs * length
    assert row0 % blk == 0
    b0 = row0 // blk
    n_seq = n_blocks * seqs

    def rows_map(col):
        return lambda b, c: (b0 + b * n_chunks + c, col)

    kernel = functools.partial(_mlstm_kernel, seqs=seqs, length=length)
    return pl.pallas_call(
        kernel,
        out_shape=(jax.ShapeDtypeStruct((n_blocks * n_chunks * blk, HEADS * DV), BF16),
                   jax.ShapeDtypeStruct((n_seq, HEADS, DQK, DV), F32),
                   jax.ShapeDtypeStruct((n_seq, HEADS, DQK), F32),
                   jax.ShapeDtypeStruct((n_seq, HEADS, GATE_LANES), F32)),
        grid=(n_blocks, n_chunks),
        in_specs=[pl.BlockSpec((blk, 2 * HEADS * DQK), rows_map(0)),
                  pl.BlockSpec((blk, HEADS * DV), rows_map(1)),
                  pl.BlockSpec((blk, HEADS * DV), rows_map(2)),
                  pl.BlockSpec((blk, GATE_LANES), rows_map(0)),
                  pl.BlockSpec((1, seqs, 2 * HEADS, length), lambda b, c: (b * n_chunks + c, 0, 0, 0)),
                  pl.BlockSpec((1, HEADS * DV), lambda b, c: (0, 0)),
                  pl.BlockSpec((seqs, HEADS, DQK, DV), lambda b, c: (b, 0, 0, 0)),
                  pl.BlockSpec((seqs, HEADS, DQK), lambda b, c: (b, 0, 0)),
                  pl.BlockSpec((seqs, HEADS, GATE_LANES), lambda b, c: (b, 0, 0))],
        out_specs=(pl.BlockSpec((blk, HEADS * DV), lambda b, c: (b * n_chunks + c, 0)),
                   pl.BlockSpec((seqs, HEADS, DQK, DV), lambda b, c: (b, 0, 0, 0)),
                   pl.BlockSpec((seqs, HEADS, DQK), lambda b, c: (b, 0, 0)),
                   pl.BlockSpec((seqs, HEADS, GATE_LANES), lambda b, c: (b, 0, 0))),
        scratch_shapes=[pltpu.VMEM((seqs, HEADS, DQK, DV), F32),
                        pltpu.VMEM((seqs, HEADS, DQK), F32),
                        pltpu.VMEM((seqs, HEADS, GATE_LANES), F32)],
        compiler_params=_params(("parallel", "arbitrary")),
        name=f"mlstm_l{length}",
    )(proj, proj, proj, gcol, grow, head_g, c0, n0, m0)


def _pool_mix(ext, first, n_rows, pos, wg_ref, scale_ref):
    outs = []
    for g, w in enumerate(POOL_WINDOWS):
        cols = slice(g * POOL_GROUP_DIM, (g + 1) * POOL_GROUP_DIM)
        e = ext[:, cols]
        total, span = e, 1
        while span < w:
            total = total + pltpu.roll(total, span, 0)
            span *= 2
        cnt = jnp.minimum(float(w), pos + 1.0)
        pooled = total[first:first + n_rows] / cnt - e[first:first + n_rows]
        outs.append(_dot(pooled.astype(BF16), wg_ref[g]))
    mixed = jnp.concatenate(outs, axis=-1) * scale_ref[...]
    return mixed.astype(BF16)


def _pool_rows_kernel(u_ref, halo_ref, wg_ref, scale_ref, o_ref, *, tiles_per_seq, pos0):
    tr = u_ref.shape[0]
    ext = jnp.concatenate([halo_ref[...], u_ref[...]], axis=0)
    tile = pl.program_id(0) % tiles_per_seq
    pos = (lax.broadcasted_iota(jnp.int32, (tr, 1), 0) + (tile * tr + pos0)).astype(F32)
    o_ref[...] = _pool_mix(ext, N_META, tr, pos, wg_ref, scale_ref)


def _pool_rows(u, halo_src, w_group, scale, *, row0, n_tiles, tr, tiles_per_seq, pos0, first_halo_block):
    hb = tr // N_META
    t0 = row0 // tr

    def halo_map(i):
        return (jnp.where(i % tiles_per_seq == 0, first_halo_block, (t0 + i) * hb - 1), 0)

    kernel = functools.partial(_pool_rows_kernel, tiles_per_seq=tiles_per_seq, pos0=pos0)
    return pl.pallas_call(
        kernel,
        out_shape=jax.ShapeDtypeStruct((n_tiles * tr, D), BF16),
        grid=(n_tiles,),
        in_specs=[pl.BlockSpec((tr, D), lambda i: (t0 + i, 0)),
                  pl.BlockSpec((N_META, D), halo_map),
                  pl.BlockSpec((len(POOL_WINDOWS), POOL_GROUP_DIM, POOL_GROUP_DIM), lambda i: (0, 0, 0)),
                  pl.BlockSpec((1, D), lambda i: (0, 0))],
        out_specs=pl.BlockSpec((tr, D), lambda i: (i, 0)),
        compiler_params=_params(("parallel",)),
        name=f"pool_rows_{tr}",
    )(u, halo_src, w_group, scale)


def _pool_seqs_kernel(ext_ref, wg_ref, scale_ref, o_ref, *, pos0):
    g, r, _ = ext_ref.shape
    n_new = r - 1 - POOL_BUF
    ext = ext_ref[...].reshape(g * r, D)
    outs = []
    pos = (lax.broadcasted_iota(jnp.int32, (g, n_new, 1), 1) + pos0).astype(F32)
    for gi, w in enumerate(POOL_WINDOWS):
        cols = slice(gi * POOL_GROUP_DIM, (gi + 1) * POOL_GROUP_DIM)
        e = ext[:, cols]
        total, span = e, 1
        while span < w:
            total = total + pltpu.roll(total, span, 0)
            span *= 2
        cnt = jnp.minimum(float(w), pos + 1.0)
        tot3 = total.reshape(g, r, POOL_GROUP_DIM)[:, 1 + POOL_BUF:, :]
        e3 = e.reshape(g, r, POOL_GROUP_DIM)[:, 1 + POOL_BUF:, :]
        pooled = (tot3 / cnt - e3).reshape(g * n_new, POOL_GROUP_DIM)
        outs.append(_dot(pooled.astype(BF16), wg_ref[gi]))
    mixed = jnp.concatenate(outs, axis=-1) * scale_ref[...]
    o_ref[...] = mixed.astype(BF16)


def _pool_seqs(ext3, w_group, scale, *, seqs_per_block, pos0):
    n_seq, r, _ = ext3.shape
    n_new = r - 1 - POOL_BUF
    kernel = functools.partial(_pool_seqs_kernel, pos0=pos0)
    return pl.pallas_call(
        kernel,
        out_shape=jax.ShapeDtypeStruct((n_seq * n_new, D), BF16),
        grid=(n_seq // seqs_per_block,),
        in_specs=[pl.BlockSpec((seqs_per_block, r, D), lambda i: (i, 0, 0)),
                  pl.BlockSpec((len(POOL_WINDOWS), POOL_GROUP_DIM, POOL_GROUP_DIM), lambda i: (0, 0, 0)),
                  pl.BlockSpec((1, D), lambda i: (0, 0))],
        out_specs=pl.BlockSpec((seqs_per_block * n_new, D), lambda i: (i, 0)),
        compiler_params=_params(("parallel",)),
        name="pool_seqs",
    )(ext3, w_group, scale)


TM_PROJ = 928
TN_PROJ = 1024
TM = 464
TF = 1024
CHUNK = 256
SAMPLE_SEQS = 2
POOL_TR = 512
POOL_SEQS = 16


def kernel(x_prompt, x_sample, state_mlstm_c, state_mlstm_n, state_mlstm_m, state_pool, meta_tokens,
           norm_mix_pre, norm_mix_post, norm_ffn_pre, norm_ffn_post, mlstm_w_in, mlstm_b_i, mlstm_b_f,
           mlstm_head_norm, mlstm_w_out, pool_w_in, pool_w_group, pool_scale, pool_w_out,
           ffn_w_up, ffn_w_down):
    B, S, _ = x_prompt.shape
    DB, DS, _ = x_sample.shape
    n_p, n_s = B * S, DB * DS
    row_s, row_m = n_p, n_p + n_s
    rows = -(-(row_m + N_META) // TM_PROJ) * TM_PROJ
    assert rows % TM == 0 and S % CHUNK == 0 and S % POOL_TR == 0 and row_m % N_META == 0

    h = jnp.concatenate([x_prompt.reshape(n_p, D), x_sample.reshape(n_s, D), meta_tokens.astype(F32),
                         jnp.zeros((rows - row_m - N_META, D), F32)], axis=0)

    def gain(a):
        return a.reshape(1, D).astype(F32)

    w_in = mlstm_w_in[0]
    w_gate = jnp.pad(w_in[:, PROJ_MAIN:], ((0, 0), (0, GATE_LANES - 2 * HEADS))).astype(BF16)
    b_gate = jnp.pad(jnp.concatenate([mlstm_b_i[0], mlstm_b_f[0]]).astype(F32),
                     (0, GATE_LANES - 2 * HEADS)).reshape(1, GATE_LANES)
    proj, gcol = _norm_matmul(h, gain(norm_mix_pre[0]), w_in[:, :PROJ_MAIN].astype(BF16),
                              tm=TM_PROJ, tn=TN_PROJ, out_dtype=BF16, gates=(w_gate, b_gate))
    g8 = gcol[:, :2 * HEADS]

    def gate_rows(r0, n_blk, seqs, length):
        return g8[r0:r0 + n_blk * seqs * length].reshape(n_blk, seqs, length, 2 * HEADS).transpose(0, 1, 3, 2)

    head_g = mlstm_head_norm[0].reshape(1, HEADS * DV).astype(F32)
    zc = jnp.zeros((1, HEADS, DQK, DV), F32)
    zn = jnp.zeros((1, HEADS, DQK), F32)
    zm = jnp.zeros((1, HEADS, GATE_LANES), F32)
    mix_m, c_m, n_m, m_m = _mlstm(proj, gcol, gate_rows(row_m, 1, 1, N_META), head_g, zc, zn, zm,
                                  row0=row_m, n_blocks=1, seqs=1, length=N_META, n_chunks=1)
    mix_p, c_p, n_p_, m_p = _mlstm(proj, gcol, gate_rows(0, B * (S // CHUNK), 1, CHUNK), head_g,
                                   jnp.broadcast_to(c_m, (B,) + c_m.shape[1:]),
                                   jnp.broadcast_to(n_m, (B,) + n_m.shape[1:]),
                                   jnp.broadcast_to(m_m, (B,) + m_m.shape[1:]),
                                   row0=0, n_blocks=B, seqs=1, length=CHUNK, n_chunks=S // CHUNK)
    m0_s = jnp.broadcast_to(state_mlstm_m[0].astype(F32)[:, :, None], (DB, HEADS, GATE_LANES))
    mix_s, c_s, n_s_, m_s = _mlstm(proj, gcol, gate_rows(row_s, DB // SAMPLE_SEQS, SAMPLE_SEQS, DS), head_g,
                                   state_mlstm_c[0].astype(F32), state_mlstm_n[0].astype(F32), m0_s,
                                   row0=row_s, n_blocks=DB // SAMPLE_SEQS, seqs=SAMPLE_SEQS, length=DS,
                                   n_chunks=1)
    pad_rows = jnp.zeros((rows - row_m - N_META, D), BF16)
    mix = jnp.concatenate([mix_p, mix_s, mix_m, pad_rows], axis=0)
    h = _matmul_norm_res(mix, mlstm_w_out[0].astype(BF16), gain(norm_mix_post[0]), h, tm=TM)
    h = _ffn(h, gain(norm_ffn_pre[0]), ffn_w_up[0].astype(BF16), ffn_w_down[0].astype(BF16),
             gain(norm_ffn_post[0]), tm=TM, tf=TF)

    u = _norm_matmul(h, gain(norm_mix_pre[1]), pool_w_in[0].astype(BF16), tm=TM_PROJ, tn=TN_PROJ,
                     out_dtype=F32)
    w_group = pool_w_group[0].astype(BF16)
    p_scale = pool_scale[0].reshape(1, D).astype(F32)
    u_meta = u[row_m:row_m + N_META]
    pmix_m = _pool_rows(u_meta, jnp.zeros((N_META, D), F32), w_group, p_scale, row0=0, n_tiles=1,
                        tr=N_META, tiles_per_seq=1, pos0=0, first_halo_block=0)
    pmix_p = _pool_rows(u, u, w_group, p_scale, row0=0, n_tiles=n_p // POOL_TR, tr=POOL_TR,
                        tiles_per_seq=S // POOL_TR, pos0=N_META, first_halo_block=row_m // N_META)
    ext_s = jnp.concatenate([jnp.zeros((DB, 1, D), F32), state_pool[0].astype(F32),
                             u[row_s:row_m].reshape(DB, DS, D)], axis=1)
    pmix_s = _pool_seqs(ext_s, w_group, p_scale, seqs_per_block=POOL_SEQS, pos0=PAST_LEN)
    pmix = jnp.concatenate([pmix_p, pmix_s, pmix_m, pad_rows], axis=0)
    h = _matmul_norm_res(pmix, pool_w_out[0].astype(BF16), gain(norm_mix_post[1]), h, tm=TM)
    h = _ffn(h, gain(norm_ffn_pre[1]), ffn_w_up[1].astype(BF16), ffn_w_down[1].astype(BF16),
             gain(norm_ffn_post[1]), tm=TM, tf=TF)

    y_prompt = h[:n_p].reshape(B, S, D)
    y_sample = h[row_s:row_m].reshape(DB, DS, D)
    dt_c, dt_n, dt_m, dt_pool = state_mlstm_c.dtype, state_mlstm_n.dtype, state_mlstm_m.dtype, state_pool.dtype
    u_prompt = u[:n_p].reshape(B, S, D)
    return (y_prompt, y_sample,
            c_p[None].astype(dt_c), n_p_[None].astype(dt_n), m_p[None, :, :, 0].astype(dt_m),
            u_prompt[None, :, S - POOL_BUF:].astype(dt_pool),
            c_s[None].astype(dt_c), n_s_[None].astype(dt_n), m_s[None, :, :, 0].astype(dt_m),
            ext_s[None, :, -POOL_BUF:].astype(dt_pool))
```

```python
import functools

import jax
import jax.numpy as jnp
from jax import lax
from jax.experimental import pallas as pl
from jax.experimental.pallas import tpu as pltpu

D = 2048
N_META = 16
HEADS = 4
DQK = 256
DV = 512
D_FF = 4 * D
POOL_WINDOWS = (2, 4, 8, 16)
POOL_GROUP_DIM = D // len(POOL_WINDOWS)
POOL_BUF = 15
PAST_LEN = 16384
EPS = 1e-6
K_SCALE = DQK ** -0.5
PROJ_MAIN = 2 * HEADS * DQK + 2 * HEADS * DV
GATE_LANES = 128

VMEM_LIMIT = 56 * 1024 * 1024

F32 = jnp.float32
BF16 = jnp.bfloat16


def _params(semantics):
    return pltpu.CompilerParams(dimension_semantics=semantics, vmem_limit_bytes=VMEM_LIMIT)


def _rmsnorm(x, g):
    return x * lax.rsqrt(jnp.mean(x * x, axis=-1, keepdims=True) + EPS) * g


def _dot(a, b):
    return jnp.dot(a, b, preferred_element_type=F32)


def _norm_matmul_kernel(x_ref, g_ref, w_ref, o_ref, xn_ref):
    @pl.when(pl.program_id(1) == 0)
    def _():
        xn_ref[...] = _rmsnorm(x_ref[...], g_ref[...]).astype(BF16)

    o_ref[...] = _dot(xn_ref[...], w_ref[...]).astype(o_ref.dtype)


def _norm_matmul_gates_kernel(x_ref, g_ref, w_ref, wg_ref, bg_ref, o_ref, gate_ref, xn_ref):
    @pl.when(pl.program_id(1) == 0)
    def _():
        xn = _rmsnorm(x_ref[...], g_ref[...]).astype(BF16)
        xn_ref[...] = xn
        z = _dot(xn, wg_ref[...]) + bg_ref[...]
        lane = lax.broadcasted_iota(jnp.int32, z.shape, 1)
        log_sig = jnp.minimum(z, 0.0) - jnp.log1p(jnp.exp(-jnp.abs(z)))
        gate_ref[...] = jnp.where(lane >= HEADS, log_sig, z)

    o_ref[...] = _dot(xn_ref[...], w_ref[...]).astype(o_ref.dtype)


def _norm_matmul(x, g, w, *, tm, tn, out_dtype, gates=None):
    rows, n = x.shape[0], w.shape[1]
    grid = (rows // tm, n // tn)
    x_spec = pl.BlockSpec((tm, D), lambda i, j: (i, 0))
    g_spec = pl.BlockSpec((1, D), lambda i, j: (0, 0))
    w_spec = pl.BlockSpec((D, tn), lambda i, j: (0, j))
    o_spec = pl.BlockSpec((tm, tn), lambda i, j: (i, j))
    scratch = [pltpu.VMEM((tm, D), BF16)]
    if gates is None:
        return pl.pallas_call(
            _norm_matmul_kernel,
            out_shape=jax.ShapeDtypeStruct((rows, n), out_dtype),
            grid=grid, in_specs=[x_spec, g_spec, w_spec], out_specs=o_spec,
            scratch_shapes=scratch, compiler_params=_params(("parallel", "arbitrary")),
            name="norm_matmul",
        )(x, g, w)
    wg, bg = gates
    return pl.pallas_call(
        _norm_matmul_gates_kernel,
        out_shape=(jax.ShapeDtypeStruct((rows, n), out_dtype),
                   jax.ShapeDtypeStruct((rows, GATE_LANES), F32)),
        grid=grid,
        in_specs=[x_spec, g_spec, w_spec,
                  pl.BlockSpec((D, GATE_LANES), lambda i, j: (0, 0)),
                  pl.BlockSpec((1, GATE_LANES), lambda i, j: (0, 0))],
        out_specs=(o_spec, pl.BlockSpec((tm, GATE_LANES), lambda i, j: (i, 0))),
        scratch_shapes=scratch, compiler_params=_params(("parallel", "arbitrary")),
        name="norm_matmul_gates",
    )(x, g, w, wg, bg)


def _matmul_norm_res_kernel(a_ref, w_ref, g_ref, h_ref, o_ref):
    y = _dot(a_ref[...], w_ref[...])
    o_ref[...] = h_ref[...] + _rmsnorm(y, g_ref[...])


def _matmul_norm_res(a, w, g, h, *, tm):
    rows = a.shape[0]
    return pl.pallas_call(
        _matmul_norm_res_kernel,
        out_shape=jax.ShapeDtypeStruct((rows, D), F32),
        grid=(rows // tm,),
        in_specs=[pl.BlockSpec((tm, D), lambda i: (i, 0)),
                  pl.BlockSpec((D, D), lambda i: (0, 0)),
                  pl.BlockSpec((1, D), lambda i: (0, 0)),
                  pl.BlockSpec((tm, D), lambda i: (i, 0))],
        out_specs=pl.BlockSpec((tm, D), lambda i: (i, 0)),
        compiler_params=_params(("parallel",)),
        name="matmul_norm_res",
    )(a, w, g, h)


def _ffn_kernel(h_ref, gpre_ref, wup_ref, wdown_ref, gpost_ref, o_ref, xn_ref, acc_ref):
    f = pl.program_id(1)

    @pl.when(f == 0)
    def _():
        xn_ref[...] = _rmsnorm(h_ref[...], gpre_ref[...]).astype(BF16)
        acc_ref[...] = jnp.zeros_like(acc_ref)

    a = jnp.maximum(_dot(xn_ref[...], wup_ref[...]), 0.0)
    acc_ref[...] += _dot((a * a).astype(BF16), wdown_ref[...])

    @pl.when(f == pl.num_programs(1) - 1)
    def _():
        o_ref[...] = h_ref[...] + _rmsnorm(acc_ref[...], gpost_ref[...])


def _ffn(h, g_pre, w_up, w_down, g_post, *, tm, tf):
    rows = h.shape[0]
    return pl.pallas_call(
        _ffn_kernel,
        out_shape=jax.ShapeDtypeStruct((rows, D), F32),
        grid=(rows // tm, D_FF // tf),
        in_specs=[pl.BlockSpec((tm, D), lambda i, f: (i, 0)),
                  pl.BlockSpec((1, D), lambda i, f: (0, 0)),
                  pl.BlockSpec((D, tf), lambda i, f: (0, f)),
                  pl.BlockSpec((tf, D), lambda i, f: (f, 0)),
                  pl.BlockSpec((1, D), lambda i, f: (0, 0))],
        out_specs=pl.BlockSpec((tm, D), lambda i, f: (i, 0)),
        scratch_shapes=[pltpu.VMEM((tm, D), BF16), pltpu.VMEM((tm, D), F32)],
        compiler_params=_params(("parallel", "arbitrary")),
        name="ffn",
    )(h, g_pre, w_up, w_down, g_post)


def _mlstm_chunk(q, k, v, li_row, lf_row, li_col, lf_col, c_state, n_row, m):
    L = q.shape[0]
    t_idx = lax.broadcasted_iota(jnp.int32, (L, L), 0)
    s_idx = lax.broadcasted_iota(jnp.int32, (L, L), 1)
    causal = s_idx <= t_idx
    b_col = jnp.sum(jnp.where(causal, lf_row, 0.0), axis=1, keepdims=True)
    b_row = jnp.sum(jnp.where(t_idx <= s_idx, lf_col, 0.0), axis=0, keepdims=True)
    b_tot = b_col[L - 1:L, :]
    dmat = jnp.where(causal, b_col - b_row + li_row, -jnp.inf)
    m_inter = b_col + m
    m_t = jnp.maximum(m_inter, jnp.max(dmat, axis=1, keepdims=True))
    p = jnp.exp(dmat - m_t)
    w_inter = jnp.exp(m_inter - m_t)
    qk = lax.dot_general(q, k, (((1,), (1,)), ((), ())), preferred_element_type=F32)
    scores = qk * (p * K_SCALE)
    num = _dot(scores.astype(BF16), v) + _dot(q, c_state.astype(BF16)) * w_inter
    qn = jnp.sum(q.astype(F32) * n_row, axis=1, keepdims=True)
    den = jnp.sum(scores, axis=1, keepdims=True) + w_inter * qn
    h = num / jnp.maximum(jnp.abs(den), jnp.exp(-m_t))
    m_new = m_t[L - 1:L, :]
    decay = jnp.exp(b_tot - b_col + li_col - m_new)
    carry = jnp.exp(b_tot + m - m_new)
    kd = k.astype(F32) * (decay * K_SCALE)
    c_new = carry * c_state + lax.dot_general(
        kd.astype(BF16), v, (((0,), (0,)), ((), ())), preferred_element_type=F32)
    n_new = carry * n_row + jnp.sum(kd, axis=0, keepdims=True)
    return h, c_new, n_new, m_new


def _mlstm_kernel(qk_ref, v_ref, o_ref, gcol_ref, grow_ref, hg_ref, c0_ref, n0_ref, m0_ref,
                  out_ref, c_out_ref, n_out_ref, m_out_ref, c_scr, n_scr, m_scr, *, seqs, length):
    c_idx = pl.program_id(1)

    @pl.when(c_idx == 0)
    def _():
        c_scr[...] = c0_ref[...]
        n_scr[...] = n0_ref[...]
        m_scr[...] = m0_ref[...]

    for s in range(seqs):
        rows = slice(s * length, (s + 1) * length)
        gcol = gcol_ref[rows, :]
        grow = grow_ref[0, s]
        for hd in range(HEADS):
            q = qk_ref[rows, hd * DQK:(hd + 1) * DQK]
            k = qk_ref[rows, (HEADS + hd) * DQK:(HEADS + hd + 1) * DQK]
            v = v_ref[rows, hd * DV:(hd + 1) * DV]
            h, c_new, n_new, m_new = _mlstm_chunk(
                q, k, v,
                grow[hd:hd + 1, :], grow[HEADS + hd:HEADS + hd + 1, :],
                gcol[:, hd:hd + 1], gcol[:, HEADS + hd:HEADS + hd + 1],
                c_scr[s, hd], n_scr[s, hd:hd + 1, :], m_scr[s, hd:hd + 1, 0:1])
            c_scr[s, hd] = c_new
            n_scr[s, hd:hd + 1, :] = n_new
            m_scr[s, hd:hd + 1, :] = jnp.broadcast_to(m_new, (1, GATE_LANES))
            hn = _rmsnorm(h, hg_ref[:, hd * DV:(hd + 1) * DV])
            gate = jax.nn.sigmoid(o_ref[rows, hd * DV:(hd + 1) * DV].astype(F32))
            out_ref[rows, hd * DV:(hd + 1) * DV] = (hn * gate).astype(out_ref.dtype)

    @pl.when(c_idx == pl.num_programs(1) - 1)
    def _():
        c_out_ref[...] = c_scr[...]
        n_out_ref[...] = n_scr[...]
        m_out_ref[...] = m_scr[...]


def _mlstm(proj, gcol, grow, head_g, c0, n0, m0, *, row0, n_blocks, seqs, length, n_chunks):
    assert seqs == 1 or n_chunks == 1
    blk = seqs * length
    assert row0 % blk == 0
    b0 = row0 // blk
    n_seq = n_blocks * seqs

    def rows_map(col):
        return lambda b, c: (b0 + b * n_chunks + c, col)

    kernel = functools.partial(_mlstm_kernel, seqs=seqs, length=length)
    return pl.pallas_call(
        kernel,
        out_shape=(jax.ShapeDtypeStruct((n_blocks * n_chunks * blk, HEADS * DV), BF16),
                   jax.ShapeDtypeStruct((n_seq, HEADS, DQK, DV), F32),
                   jax.ShapeDtypeStruct((n_seq, HEADS, DQK), F32),
                   jax.ShapeDtypeStruct((n_seq, HEADS, GATE_LANES), F32)),
        grid=(n_blocks, n_chunks),
        in_specs=[pl.BlockSpec((blk, 2 * HEADS * DQK), rows_map(0)),
                  pl.BlockSpec((blk, HEADS * DV), rows_map(1)),
                  pl.BlockSpec((blk, HEADS * DV), rows_map(2)),
                  pl.BlockSpec((blk, GATE_LANES), rows_map(0)),
                  pl.BlockSpec((1, seqs, 2 * HEADS, length), lambda b, c: (b * n_chunks + c, 0, 0, 0)),
                  pl.BlockSpec((1, HEADS * DV), lambda b, c: (0, 0)),
                  pl.BlockSpec((seqs, HEADS, DQK, DV), lambda b, c: (b, 0, 0, 0)),
                  pl.BlockSpec((seqs, HEADS, DQK), lambda b, c: (b, 0, 0)),
                  pl.BlockSpec((seqs, HEADS, GATE_LANES), lambda b, c: (b, 0, 0))],
        out_specs=(pl.BlockSpec((blk, HEADS * DV), lambda b, c: (b * n_chunks + c, 0)),
                   pl.BlockSpec((seqs, HEADS, DQK, DV), lambda b, c: (b, 0, 0, 0)),
                   pl.BlockSpec((seqs, HEADS, DQK), lambda b, c: (b, 0, 0)),
                   pl.BlockSpec((seqs, HEADS, GATE_LANES), lambda b, c: (b, 0, 0))),
        scratch_shapes=[pltpu.VMEM((seqs, HEADS, DQK, DV), F32),
                        pltpu.VMEM((seqs, HEADS, DQK), F32),
                        pltpu.VMEM((seqs, HEADS, GATE_LANES), F32)],
        compiler_params=_params(("parallel", "arbitrary")),
        name=f"mlstm_l{length}",
    )(proj, proj, proj, gcol, grow, head_g, c0, n0, m0)


def _pool_mix(ext, first, n_rows, pos, wg_ref, scale_ref):
    outs = []
    for g, w in enumerate(POOL_WINDOWS):
        cols = slice(g * POOL_GROUP_DIM, (g + 1) * POOL_GROUP_DIM)
        e = ext[:, cols]
        total, span = e, 1
        while span < w:
            total = total + pltpu.roll(total, span, 0)
            span *= 2
        cnt = jnp.minimum(float(w), pos + 1.0)
        pooled = total[first:first + n_rows] / cnt - e[first:first + n_rows]
        outs.append(_dot(pooled.astype(BF16), wg_ref[g]))
    mixed = jnp.concatenate(outs, axis=-1) * scale_ref[...]
    return mixed.astype(BF16)


def _pool_rows_kernel(u_ref, prev_ref, first_ref, wg_ref, scale_ref, o_ref, *, tiles_per_seq, pos0):
    tr = u_ref.shape[0]
    tile = pl.program_id(0) % tiles_per_seq
    halo = jnp.where(tile == 0, first_ref[...], prev_ref[...])
    ext = jnp.concatenate([halo, u_ref[...]], axis=0)
    pos = (lax.broadcasted_iota(jnp.int32, (tr, 1), 0) + (tile * tr + pos0)).astype(F32)
    o_ref[...] = _pool_mix(ext, N_META, tr, pos, wg_ref, scale_ref)


def _pool_rows(u, first_src, w_group, scale, *, row0, n_tiles, tr, tiles_per_seq, pos0, first_block):
    hb = tr // N_META
    t0 = row0 // tr
    kernel = functools.partial(_pool_rows_kernel, tiles_per_seq=tiles_per_seq, pos0=pos0)
    return pl.pallas_call(
        kernel,
        out_shape=jax.ShapeDtypeStruct((n_tiles * tr, D), BF16),
        grid=(n_tiles,),
        in_specs=[pl.BlockSpec((tr, D), lambda i: (t0 + i, 0)),
                  pl.BlockSpec((N_META, D), lambda i: (jnp.maximum((t0 + i) * hb - 1, 0), 0)),
                  pl.BlockSpec((N_META, D), lambda i: (first_block, 0)),
                  pl.BlockSpec((len(POOL_WINDOWS), POOL_GROUP_DIM, POOL_GROUP_DIM), lambda i: (0, 0, 0)),
                  pl.BlockSpec((1, D), lambda i: (0, 0))],
        out_specs=pl.BlockSpec((tr, D), lambda i: (i, 0)),
        compiler_params=_params(("parallel",)),
        name=f"pool_rows_{tr}",
    )(u, u, first_src, w_group, scale)


def _pool_seqs_kernel(ext_ref, wg_ref, scale_ref, o_ref, *, pos0):
    g, r, _ = ext_ref.shape
    n_new = r - 1 - POOL_BUF
    ext = ext_ref[...].reshape(g * r, D)
    outs = []
    pos = (lax.broadcasted_iota(jnp.int32, (g, n_new, 1), 1) + pos0).astype(F32)
    for gi, w in enumerate(POOL_WINDOWS):
        cols = slice(gi * POOL_GROUP_DIM, (gi + 1) * POOL_GROUP_DIM)
        e = ext[:, cols]
        total, span = e, 1
        while span < w:
            total = total + pltpu.roll(total, span, 0)
            span *= 2
        cnt = jnp.minimum(float(w), pos + 1.0)
        tot3 = total.reshape(g, r, POOL_GROUP_DIM)[:, 1 + POOL_BUF:, :]
        e3 = e.reshape(g, r, POOL_GROUP_DIM)[:, 1 + POOL_BUF:, :]
        pooled = (tot3 / cnt - e3).reshape(g * n_new, POOL_GROUP_DIM)
        outs.append(_dot(pooled.astype(BF16), wg_ref[gi]))
    mixed = jnp.concatenate(outs, axis=-1) * scale_ref[...]
    o_ref[...] = mixed.astype(BF16)


def _pool_seqs(ext3, w_group, scale, *, seqs_per_block, pos0):
    n_seq, r, _ = ext3.shape
    n_new = r - 1 - POOL_BUF
    kernel = functools.partial(_pool_seqs_kernel, pos0=pos0)
    return pl.pallas_call(
        kernel,
        out_shape=jax.ShapeDtypeStruct((n_seq * n_new, D), BF16),
        grid=(n_seq // seqs_per_block,),
        in_specs=[pl.BlockSpec((seqs_per_block, r, D), lambda i: (i, 0, 0)),
                  pl.BlockSpec((len(POOL_WINDOWS), POOL_GROUP_DIM, POOL_GROUP_DIM), lambda i: (0, 0, 0)),
                  pl.BlockSpec((1, D), lambda i: (0, 0))],
        out_specs=pl.BlockSpec((seqs_per_block * n_new, D), lambda i: (i, 0)),
        compiler_params=_params(("parallel",)),
        name="pool_seqs",
    )(ext3, w_group, scale)


TM_PROJ = 1024
TN_PROJ = 1024
TM = 512
TM_AUX = 544
TF = 1024
CHUNK = 256
SAMPLE_SEQS = 2
POOL_TR = 512
POOL_SEQS = 16


def kernel(x_prompt, x_sample, state_mlstm_c, state_mlstm_n, state_mlstm_m, state_pool, meta_tokens,
           norm_mix_pre, norm_mix_post, norm_ffn_pre, norm_ffn_post, mlstm_w_in, mlstm_b_i, mlstm_b_f,
           mlstm_head_norm, mlstm_w_out, pool_w_in, pool_w_group, pool_scale, pool_w_out,
           ffn_w_up, ffn_w_down):
    B, S, _ = x_prompt.shape
    DB, DS, _ = x_sample.shape
    n_p, n_s = B * S, DB * DS
    rows_a = 2 * TM_AUX
    row_m = n_s
    assert n_s + N_META <= rows_a and n_p % TM_PROJ == 0 and n_p % TM == 0
    assert S % CHUNK == 0 and S % POOL_TR == 0 and row_m % N_META == 0

    h_p = x_prompt.reshape(n_p, D)
    h_a = jnp.concatenate([x_sample.reshape(n_s, D), meta_tokens.astype(F32),
                           jnp.zeros((rows_a - n_s - N_META, D), F32)], axis=0)
    pad_a = jnp.zeros((rows_a - n_s - N_META, D), BF16)

    def gain(a):
        return a.reshape(1, D).astype(F32)

    def norm_matmul(w, g, out_dtype, gates=None):
        return (_norm_matmul(h_p, g, w, tm=TM_PROJ, tn=TN_PROJ, out_dtype=out_dtype, gates=gates),
                _norm_matmul(h_a, g, w, tm=TM_AUX, tn=TN_PROJ, out_dtype=out_dtype, gates=gates))

    def matmul_norm_res(a_p, a_a, w, g):
        return _matmul_norm_res(a_p, w, g, h_p, tm=TM), _matmul_norm_res(a_a, w, g, h_a, tm=TM_AUX)

    def ffn(layer):
        args = (gain(norm_ffn_pre[layer]), ffn_w_up[layer].astype(BF16), ffn_w_down[layer].astype(BF16),
                gain(norm_ffn_post[layer]))
        return _ffn(h_p, *args, tm=TM, tf=TF), _ffn(h_a, *args, tm=TM_AUX, tf=TF)

    w_in = mlstm_w_in[0]
    w_gate = jnp.pad(w_in[:, PROJ_MAIN:], ((0, 0), (0, GATE_LANES - 2 * HEADS))).astype(BF16)
    b_gate = jnp.pad(jnp.concatenate([mlstm_b_i[0], mlstm_b_f[0]]).astype(F32),
                     (0, GATE_LANES - 2 * HEADS)).reshape(1, GATE_LANES)
    (proj_p, gcol_p), (proj_a, gcol_a) = norm_matmul(w_in[:, :PROJ_MAIN].astype(BF16), gain(norm_mix_pre[0]),
                                                     BF16, gates=(w_gate, b_gate))

    def gate_rows(gcol, r0, n_blk, seqs, length):
        g8 = gcol[r0:r0 + n_blk * seqs * length, :2 * HEADS]
        return g8.reshape(n_blk, seqs, length, 2 * HEADS).transpose(0, 1, 3, 2)

    head_g = mlstm_head_norm[0].reshape(1, HEADS * DV).astype(F32)
    zc = jnp.zeros((1, HEADS, DQK, DV), F32)
    zn = jnp.zeros((1, HEADS, DQK), F32)
    zm = jnp.zeros((1, HEADS, GATE_LANES), F32)
    mix_m, c_m, n_m, m_m = _mlstm(proj_a, gcol_a, gate_rows(gcol_a, row_m, 1, 1, N_META), head_g, zc, zn, zm,
                                  row0=row_m, n_blocks=1, seqs=1, length=N_META, n_chunks=1)
    mix_p, c_p, n_p_, m_p = _mlstm(proj_p, gcol_p, gate_rows(gcol_p, 0, B * (S // CHUNK), 1, CHUNK), head_g,
                                   jnp.broadcast_to(c_m, (B,) + c_m.shape[1:]),
                                   jnp.broadcast_to(n_m, (B,) + n_m.shape[1:]),
                                   jnp.broadcast_to(m_m, (B,) + m_m.shape[1:]),
                                   row0=0, n_blocks=B, seqs=1, length=CHUNK, n_chunks=S // CHUNK)
    m0_s = jnp.broadcast_to(state_mlstm_m[0].astype(F32)[:, :, None], (DB, HEADS, GATE_LANES))
    mix_s, c_s, n_s_, m_s = _mlstm(proj_a, gcol_a, gate_rows(gcol_a, 0, DB // SAMPLE_SEQS, SAMPLE_SEQS, DS),
                                   head_g, state_mlstm_c[0].astype(F32), state_mlstm_n[0].astype(F32), m0_s,
                                   row0=0, n_blocks=DB // SAMPLE_SEQS, seqs=SAMPLE_SEQS, length=DS, n_chunks=1)
    mix_a = jnp.concatenate([mix_s, mix_m, pad_a], axis=0)
    h_p, h_a = matmul_norm_res(mix_p, mix_a, mlstm_w_out[0].astype(BF16), gain(norm_mix_post[0]))
    h_p, h_a = ffn(0)

    u_p, u_a = norm_matmul(pool_w_in[0].astype(BF16), gain(norm_mix_pre[1]), F32)
    w_group = pool_w_group[0].astype(BF16)
    p_scale = pool_scale[0].reshape(1, D).astype(F32)
    pmix_m = _pool_rows(u_a, jnp.zeros((N_META, D), F32), w_group, p_scale, row0=row_m, n_tiles=1,
                        tr=N_META, tiles_per_seq=1, pos0=0, first_block=0)
    pmix_p = _pool_rows(u_p, u_a, w_group, p_scale, row0=0, n_tiles=n_p // POOL_TR, tr=POOL_TR,
                        tiles_per_seq=S // POOL_TR, pos0=N_META, first_block=row_m // N_META)
    ext_s = jnp.concatenate([jnp.zeros((DB, 1, D), F32), state_pool[0].astype(F32),
                             u_a[:n_s].reshape(DB, DS, D)], axis=1)
    pmix_s = _pool_seqs(ext_s, w_group, p_scale, seqs_per_block=POOL_SEQS, pos0=PAST_LEN)
    pmix_a = jnp.concatenate([pmix_s, pmix_m, pad_a], axis=0)
    h_p, h_a = matmul_norm_res(pmix_p, pmix_a, pool_w_out[0].astype(BF16), gain(norm_mix_post[1]))
    h_p, h_a = ffn(1)

    y_prompt = h_p.reshape(B, S, D)
    y_sample = h_a[:n_s].reshape(DB, DS, D)
    dt_c, dt_n, dt_m, dt_pool = state_mlstm_c.dtype, state_mlstm_n.dtype, state_mlstm_m.dtype, state_pool.dtype
    u_prompt = u_p.reshape(B, S, D)
    return (y_prompt, y_sample,
            c_p[None].astype(dt_c), n_p_[None].astype(dt_n), m_p[None, :, :, 0].astype(dt_m),
            u_prompt[None, :, S - POOL_BUF:].astype(dt_pool),
            c_s[None].astype(dt_c), n_s_[None].astype(dt_n), m_s[None, :, :, 0].astype(dt_m),
            ext_s[None, :, -POOL_BUF:].astype(dt_pool))
```

```python
import functools

import jax
import jax.numpy as jnp
from jax import lax
from jax.experimental import pallas as pl
from jax.experimental.pallas import tpu as pltpu

D = 2048
N_META = 16
HEADS = 4
DQK = 256
DV = 512
D_FF = 4 * D
POOL_WINDOWS = (2, 4, 8, 16)
POOL_GROUP_DIM = D // len(POOL_WINDOWS)
POOL_BUF = 15
PAST_LEN = 16384
EPS = 1e-6
K_SCALE = DQK ** -0.5
PROJ_MAIN = 2 * HEADS * DQK + 2 * HEADS * DV
GATE_LANES = 128

VMEM_LIMIT = 56 * 1024 * 1024

F32 = jnp.float32
BF16 = jnp.bfloat16


def _params(semantics):
    return pltpu.CompilerParams(dimension_semantics=semantics, vmem_limit_bytes=VMEM_LIMIT)


def _rmsnorm(x, g):
    return x * lax.rsqrt(jnp.mean(x * x, axis=-1, keepdims=True) + EPS) * g


def _dot(a, b):
    return jnp.dot(a, b, preferred_element_type=F32)


def _norm_matmul_kernel(x_ref, g_ref, w_ref, o_ref, xn_ref):
    @pl.when(pl.program_id(1) == 0)
    def _():
        xn_ref[...] = _rmsnorm(x_ref[...], g_ref[...]).astype(BF16)

    o_ref[...] = _dot(xn_ref[...], w_ref[...].astype(BF16)).astype(o_ref.dtype)


def _norm_matmul_gates_kernel(x_ref, g_ref, w_ref, wg_ref, bg_ref, o_ref, gate_ref, xn_ref):
    @pl.when(pl.program_id(1) == 0)
    def _():
        xn = _rmsnorm(x_ref[...], g_ref[...]).astype(BF16)
        xn_ref[...] = xn
        z = _dot(xn, wg_ref[...]) + bg_ref[...]
        lane = lax.broadcasted_iota(jnp.int32, z.shape, 1)
        log_sig = jnp.minimum(z, 0.0) - jnp.log1p(jnp.exp(-jnp.abs(z)))
        gate_ref[...] = jnp.where(lane >= HEADS, log_sig, z)

    o_ref[...] = _dot(xn_ref[...], w_ref[...].astype(BF16)).astype(o_ref.dtype)


def _norm_matmul(x, g, w, *, n, tm, tn, out_dtype, gates=None):
    rows = x.shape[0]
    grid = (rows // tm, n // tn)
    x_spec = pl.BlockSpec((tm, D), lambda i, j: (i, 0))
    g_spec = pl.BlockSpec((1, D), lambda i, j: (0, 0))
    w_spec = pl.BlockSpec((D, tn), lambda i, j: (0, j))
    o_spec = pl.BlockSpec((tm, tn), lambda i, j: (i, j))
    scratch = [pltpu.VMEM((tm, D), BF16)]
    if gates is None:
        return pl.pallas_call(
            _norm_matmul_kernel,
            out_shape=jax.ShapeDtypeStruct((rows, n), out_dtype),
            grid=grid, in_specs=[x_spec, g_spec, w_spec], out_specs=o_spec,
            scratch_shapes=scratch, compiler_params=_params(("parallel", "arbitrary")),
            name="norm_matmul",
        )(x, g, w)
    wg, bg = gates
    return pl.pallas_call(
        _norm_matmul_gates_kernel,
        out_shape=(jax.ShapeDtypeStruct((rows, n), out_dtype),
                   jax.ShapeDtypeStruct((rows, GATE_LANES), F32)),
        grid=grid,
        in_specs=[x_spec, g_spec, w_spec,
                  pl.BlockSpec((D, GATE_LANES), lambda i, j: (0, 0)),
                  pl.BlockSpec((1, GATE_LANES), lambda i, j: (0, 0))],
        out_specs=(o_spec, pl.BlockSpec((tm, GATE_LANES), lambda i, j: (i, 0))),
        scratch_shapes=scratch, compiler_params=_params(("parallel", "arbitrary")),
        name="norm_matmul_gates",
    )(x, g, w, wg, bg)


def _matmul_norm_res_kernel(a_ref, w_ref, g_ref, h_ref, o_ref, wb_ref):
    @pl.when(pl.program_id(0) == 0)
    def _():
        wb_ref[...] = w_ref[...].astype(BF16)

    y = _dot(a_ref[...], wb_ref[...])
    o_ref[...] = h_ref[...] + _rmsnorm(y, g_ref[...])


def _matmul_norm_res(a, w, g, h, *, tm):
    rows = a.shape[0]
    return pl.pallas_call(
        _matmul_norm_res_kernel,
        out_shape=jax.ShapeDtypeStruct((rows, D), F32),
        grid=(rows // tm,),
        in_specs=[pl.BlockSpec((tm, D), lambda i: (i, 0)),
                  pl.BlockSpec((D, D), lambda i: (0, 0), pipeline_mode=pl.Buffered(1)),
                  pl.BlockSpec((1, D), lambda i: (0, 0)),
                  pl.BlockSpec((tm, D), lambda i: (i, 0))],
        out_specs=pl.BlockSpec((tm, D), lambda i: (i, 0)),
        scratch_shapes=[pltpu.VMEM((D, D), BF16)],
        compiler_params=_params(("arbitrary",)),
        name="matmul_norm_res",
    )(a, w, g, h)


def _ffn_kernel(h_ref, gpre_ref, wup_ref, wdown_ref, gpost_ref, o_ref, xn_ref):
    f = pl.program_id(1)

    @pl.when(f == 0)
    def _():
        xn_ref[...] = _rmsnorm(h_ref[...], gpre_ref[...]).astype(BF16)
        o_ref[...] = jnp.zeros_like(o_ref)

    a = jnp.maximum(_dot(xn_ref[...], wup_ref[...].astype(BF16)), 0.0)
    o_ref[...] += _dot((a * a).astype(BF16), wdown_ref[...].astype(BF16))

    @pl.when(f == pl.num_programs(1) - 1)
    def _():
        o_ref[...] = h_ref[...] + _rmsnorm(o_ref[...], gpost_ref[...])


def _ffn(h, g_pre, w_up, w_down, g_post, *, tm, tf):
    rows = h.shape[0]
    single = pl.Buffered(1)
    return pl.pallas_call(
        _ffn_kernel,
        out_shape=jax.ShapeDtypeStruct((rows, D), F32),
        grid=(rows // tm, D_FF // tf),
        in_specs=[pl.BlockSpec((tm, D), lambda i, f: (i, 0), pipeline_mode=single),
                  pl.BlockSpec((1, D), lambda i, f: (0, 0)),
                  pl.BlockSpec((D, tf), lambda i, f: (0, f)),
                  pl.BlockSpec((tf, D), lambda i, f: (f, 0)),
                  pl.BlockSpec((1, D), lambda i, f: (0, 0))],
        out_specs=pl.BlockSpec((tm, D), lambda i, f: (i, 0), pipeline_mode=single),
        scratch_shapes=[pltpu.VMEM((tm, D), BF16)],
        compiler_params=_params(("parallel", "arbitrary")),
        name="ffn",
    )(h, g_pre, w_up, w_down, g_post)


def _mlstm_chunk(q, k, v, li_row, lf_row, li_col, lf_col, c_state, n_row, m):
    L = q.shape[0]
    t_idx = lax.broadcasted_iota(jnp.int32, (L, L), 0)
    s_idx = lax.broadcasted_iota(jnp.int32, (L, L), 1)
    causal = s_idx <= t_idx
    b_col = jnp.sum(jnp.where(causal, lf_row, 0.0), axis=1, keepdims=True)
    b_row = jnp.sum(jnp.where(t_idx <= s_idx, lf_col, 0.0), axis=0, keepdims=True)
    b_tot = b_col[L - 1:L, :]
    dmat = jnp.where(causal, b_col - b_row + li_row, -jnp.inf)
    m_inter = b_col + m
    m_t = jnp.maximum(m_inter, jnp.max(dmat, axis=1, keepdims=True))
    p = jnp.exp(dmat - m_t)
    w_inter = jnp.exp(m_inter - m_t)
    qk = lax.dot_general(q, k, (((1,), (1,)), ((), ())), preferred_element_type=F32)
    scores = qk * (p * K_SCALE)
    num = _dot(scores.astype(BF16), v) + _dot(q, c_state.astype(BF16)) * w_inter
    qn = jnp.sum(q.astype(F32) * n_row, axis=1, keepdims=True)
    den = jnp.sum(scores, axis=1, keepdims=True) + w_inter * qn
    h = num / jnp.maximum(jnp.abs(den), jnp.exp(-m_t))
    m_new = m_t[L - 1:L, :]
    decay = jnp.exp(b_tot - b_col + li_col - m_new)
    carry = jnp.exp(b_tot + m - m_new)
    kd = k.astype(F32) * (decay * K_SCALE)
    c_new = carry * c_state + lax.dot_general(
        kd.astype(BF16), v, (((0,), (0,)), ((), ())), preferred_element_type=F32)
    n_new = carry * n_row + jnp.sum(kd, axis=0, keepdims=True)
    return h, c_new, n_new, m_new


def _mlstm_kernel(qk_ref, v_ref, o_ref, gcol_ref, grow_ref, hg_ref, c0_ref, n0_ref, m0_ref,
                  out_ref, c_out_ref, n_out_ref, m_out_ref, c_scr, n_scr, m_scr, *, seqs, length):
    c_idx = pl.program_id(1)

    @pl.when(c_idx == 0)
    def _():
        c_scr[...] = c0_ref[...]
        n_scr[...] = n0_ref[...]
        m_scr[...] = m0_ref[...]

    for s in range(seqs):
        rows = slice(s * length, (s + 1) * length)
        gcol = gcol_ref[rows, :]
        grow = grow_ref[0, s]
        for hd in range(HEADS):
            q = qk_ref[rows, hd * DQK:(hd + 1) * DQK]
            k = qk_ref[rows, (HEADS + hd) * DQK:(HEADS + hd + 1) * DQK]
            v = v_ref[rows, hd * DV:(hd + 1) * DV]
            h, c_new, n_new, m_new = _mlstm_chunk(
                q, k, v,
                grow[hd:hd + 1, :], grow[HEADS + hd:HEADS + hd + 1, :],
                gcol[:, hd:hd + 1], gcol[:, HEADS + hd:HEADS + hd + 1],
                c_scr[s, hd], n_scr[s, hd:hd + 1, :], m_scr[s, hd:hd + 1, 0:1])
            c_scr[s, hd] = c_new
            n_scr[s, hd:hd + 1, :] = n_new
            m_scr[s, hd:hd + 1, :] = jnp.broadcast_to(m_new, (1, GATE_LANES))
            hn = _rmsnorm(h, hg_ref[:, hd * DV:(hd + 1) * DV])
            gate = jax.nn.sigmoid(o_ref[rows, hd * DV:(hd + 1) * DV].astype(F32))
            out_ref[rows, hd * DV:(hd + 1) * DV] = (hn * gate).astype(out_ref.dtype)

    @pl.when(c_idx == pl.num_programs(1) - 1)
    def _():
        c_out_ref[...] = c_scr[...]
        n_out_ref[...] = n_scr[...]
        m_out_ref[...] = m_scr[...]


def _mlstm(proj, gcol, grow, head_g, c0, n0, m0, *, row0, n_blocks, seqs, length, n_chunks):
    assert seqs == 1 or n_chunks == 1
    blk = seqs * length
    assert row0 % blk == 0
    b0 = row0 // blk
    n_seq = n_blocks * seqs

    def rows_map(col):
        return lambda b, c: (b0 + b * n_chunks + c, col)

    kernel = functools.partial(_mlstm_kernel, seqs=seqs, length=length)
    return pl.pallas_call(
        kernel,
        out_shape=(jax.ShapeDtypeStruct((n_blocks * n_chunks * blk, HEADS * DV), BF16),
                   jax.ShapeDtypeStruct((n_seq, HEADS, DQK, DV), F32),
                   jax.ShapeDtypeStruct((n_seq, HEADS, DQK), F32),
                   jax.ShapeDtypeStruct((n_seq, HEADS, GATE_LANES), F32)),
        grid=(n_blocks, n_chunks),
        in_specs=[pl.BlockSpec((blk, 2 * HEADS * DQK), rows_map(0)),
                  pl.BlockSpec((blk, HEADS * DV), rows_map(1)),
                  pl.BlockSpec((blk, HEADS * DV), rows_map(2)),
                  pl.BlockSpec((blk, GATE_LANES), rows_map(0)),
                  pl.BlockSpec((1, seqs, 2 * HEADS, length), lambda b, c: (b * n_chunks + c, 0, 0, 0)),
                  pl.BlockSpec((1, HEADS * DV), lambda b, c: (0, 0)),
                  pl.BlockSpec((seqs, HEADS, DQK, DV), lambda b, c: (b, 0, 0, 0)),
                  pl.BlockSpec((seqs, HEADS, DQK), lambda b, c: (b, 0, 0)),
                  pl.BlockSpec((seqs, HEADS, GATE_LANES), lambda b, c: (b, 0, 0))],
        out_specs=(pl.BlockSpec((blk, HEADS * DV), lambda b, c: (b * n_chunks + c, 0)),
                   pl.BlockSpec((seqs, HEADS, DQK, DV), lambda b, c: (b, 0, 0, 0)),
                   pl.BlockSpec((seqs, HEADS, DQK), lambda b, c: (b, 0, 0)),
                   pl.BlockSpec((seqs, HEADS, GATE_LANES), lambda b, c: (b, 0, 0))),
        scratch_shapes=[pltpu.VMEM((seqs, HEADS, DQK, DV), F32),
                        pltpu.VMEM((seqs, HEADS, DQK), F32),
                        pltpu.VMEM((seqs, HEADS, GATE_LANES), F32)],
        compiler_params=_params(("parallel", "arbitrary")),
        name=f"mlstm_l{length}",
    )(proj, proj, proj, gcol, grow, head_g, c0, n0, m0)


def _pool_mix(ext, first, n_rows, pos, wg_ref, scale_ref):
    outs = []
    for g, w in enumerate(POOL_WINDOWS):
        cols = slice(g * POOL_GROUP_DIM, (g + 1) * POOL_GROUP_DIM)
        e = ext[:, cols]
        total, span = e, 1
        while span < w:
            total = total + pltpu.roll(total, span, 0)
            span *= 2
        cnt = jnp.minimum(float(w), pos + 1.0)
        pooled = total[first:first + n_rows] / cnt - e[first:first + n_rows]
        outs.append(_dot(pooled.astype(BF16), wg_ref[g].astype(BF16)))
    mixed = jnp.concatenate(outs, axis=-1) * scale_ref[...]
    return mixed.astype(BF16)


def _pool_rows_kernel(u_ref, prev_ref, first_ref, wg_ref, scale_ref, o_ref, *, tiles_per_seq, pos0):
    tr = u_ref.shape[0]
    tile = pl.program_id(0) % tiles_per_seq
    halo = jnp.where(tile == 0, first_ref[...], prev_ref[...])
    ext = jnp.concatenate([halo, u_ref[...]], axis=0)
    pos = (lax.broadcasted_iota(jnp.int32, (tr, 1), 0) + (tile * tr + pos0)).astype(F32)
    o_ref[...] = _pool_mix(ext, N_META, tr, pos, wg_ref, scale_ref)


def _pool_rows(u, first_src, w_group, scale, *, row0, n_tiles, tr, tiles_per_seq, pos0, first_block):
    hb = tr // N_META
    t0 = row0 // tr
    kernel = functools.partial(_pool_rows_kernel, tiles_per_seq=tiles_per_seq, pos0=pos0)
    return pl.pallas_call(
        kernel,
        out_shape=jax.ShapeDtypeStruct((n_tiles * tr, D), BF16),
        grid=(n_tiles,),
        in_specs=[pl.BlockSpec((tr, D), lambda i: (t0 + i, 0)),
                  pl.BlockSpec((N_META, D), lambda i: (jnp.maximum((t0 + i) * hb - 1, 0), 0)),
                  pl.BlockSpec((N_META, D), lambda i: (first_block, 0)),
                  pl.BlockSpec((len(POOL_WINDOWS), POOL_GROUP_DIM, POOL_GROUP_DIM), lambda i: (0, 0, 0)),
                  pl.BlockSpec((1, D), lambda i: (0, 0))],
        out_specs=pl.BlockSpec((tr, D), lambda i: (i, 0)),
        compiler_params=_params(("parallel",)),
        name=f"pool_rows_{tr}",
    )(u, u, first_src, w_group, scale)


def _pool_seqs_kernel(ext_ref, wg_ref, scale_ref, o_ref, *, pos0):
    g, r, _ = ext_ref.shape
    n_new = r - 1 - POOL_BUF
    ext = ext_ref[...].reshape(g * r, D)
    outs = []
    pos = (lax.broadcasted_iota(jnp.int32, (g, n_new, 1), 1) + pos0).astype(F32)
    for gi, w in enumerate(POOL_WINDOWS):
        cols = slice(gi * POOL_GROUP_DIM, (gi + 1) * POOL_GROUP_DIM)
        e = ext[:, cols]
        total, span = e, 1
        while span < w:
            total = total + pltpu.roll(total, span, 0)
            span *= 2
        cnt = jnp.minimum(float(w), pos + 1.0)
        tot3 = total.reshape(g, r, POOL_GROUP_DIM)[:, 1 + POOL_BUF:, :]
        e3 = e.reshape(g, r, POOL_GROUP_DIM)[:, 1 + POOL_BUF:, :]
        pooled = (tot3 / cnt - e3).reshape(g * n_new, POOL_GROUP_DIM)
        outs.append(_dot(pooled.astype(BF16), wg_ref[gi].astype(BF16)))
    mixed = jnp.concatenate(outs, axis=-1) * scale_ref[...]
    o_ref[...] = mixed.astype(BF16)


def _pool_seqs(ext3, w_group, scale, *, seqs_per_block, pos0):
    n_seq, r, _ = ext3.shape
    n_new = r - 1 - POOL_BUF
    kernel = functools.partial(_pool_seqs_kernel, pos0=pos0)
    return pl.pallas_call(
        kernel,
        out_shape=jax.ShapeDtypeStruct((n_seq * n_new, D), BF16),
        grid=(n_seq // seqs_per_block,),
        in_specs=[pl.BlockSpec((seqs_per_block, r, D), lambda i: (i, 0, 0)),
                  pl.BlockSpec((len(POOL_WINDOWS), POOL_GROUP_DIM, POOL_GROUP_DIM), lambda i: (0, 0, 0)),
                  pl.BlockSpec((1, D), lambda i: (0, 0))],
        out_specs=pl.BlockSpec((seqs_per_block * n_new, D), lambda i: (i, 0)),
        compiler_params=_params(("parallel",)),
        name="pool_seqs",
    )(ext3, w_group, scale)


TM_PROJ = 1024
TN_PROJ = 1024
TM = 512
TM_AUX = 544
TM_FFN = 1024
TF = 512
CHUNK = 256
SAMPLE_SEQS = 2
POOL_TR = 512
POOL_SEQS = 16


def kernel(x_prompt, x_sample, state_mlstm_c, state_mlstm_n, state_mlstm_m, state_pool, meta_tokens,
           norm_mix_pre, norm_mix_post, norm_ffn_pre, norm_ffn_post, mlstm_w_in, mlstm_b_i, mlstm_b_f,
           mlstm_head_norm, mlstm_w_out, pool_w_in, pool_w_group, pool_scale, pool_w_out,
           ffn_w_up, ffn_w_down):
    B, S, _ = x_prompt.shape
    DB, DS, _ = x_sample.shape
    n_p, n_s = B * S, DB * DS
    rows_a = 2 * TM_AUX
    row_m = n_s
    assert n_s + N_META <= rows_a and n_p % TM_PROJ == 0 and n_p % TM == 0
    assert S % CHUNK == 0 and S % POOL_TR == 0 and row_m % N_META == 0

    h_p = x_prompt.reshape(n_p, D)
    h_a = jnp.concatenate([x_sample.reshape(n_s, D), meta_tokens.astype(F32),
                           jnp.zeros((rows_a - n_s - N_META, D), F32)], axis=0)
    pad_a = jnp.zeros((rows_a - n_s - N_META, D), BF16)

    def gain(a):
        return a.reshape(1, D).astype(F32)

    def norm_matmul(w, n, g, out_dtype, gates=None):
        return (_norm_matmul(h_p, g, w, n=n, tm=TM_PROJ, tn=TN_PROJ, out_dtype=out_dtype, gates=gates),
                _norm_matmul(h_a, g, w, n=n, tm=TM_AUX, tn=TN_PROJ, out_dtype=out_dtype, gates=gates))

    def matmul_norm_res(a_p, a_a, w, g):
        return _matmul_norm_res(a_p, w, g, h_p, tm=TM), _matmul_norm_res(a_a, w, g, h_a, tm=TM_AUX)

    def ffn(layer):
        args = (gain(norm_ffn_pre[layer]), ffn_w_up[layer].astype(F32), ffn_w_down[layer].astype(F32),
                gain(norm_ffn_post[layer]))
        return _ffn(h_p, *args, tm=TM_FFN, tf=TF), _ffn(h_a, *args, tm=TM_AUX, tf=TF)

    w_in = mlstm_w_in[0].astype(F32)
    w_gate = jnp.pad(w_in[:, PROJ_MAIN:], ((0, 0), (0, GATE_LANES - 2 * HEADS))).astype(BF16)
    b_gate = jnp.pad(jnp.concatenate([mlstm_b_i[0], mlstm_b_f[0]]).astype(F32),
                     (0, GATE_LANES - 2 * HEADS)).reshape(1, GATE_LANES)
    (proj_p, gcol_p), (proj_a, gcol_a) = norm_matmul(w_in, PROJ_MAIN, gain(norm_mix_pre[0]), BF16,
                                                     gates=(w_gate, b_gate))

    def gate_rows(gcol, r0, n_blk, seqs, length):
        g8 = gcol[r0:r0 + n_blk * seqs * length, :2 * HEADS]
        return g8.reshape(n_blk, seqs, length, 2 * HEADS).transpose(0, 1, 3, 2)

    head_g = mlstm_head_norm[0].reshape(1, HEADS * DV).astype(F32)
    zc = jnp.zeros((1, HEADS, DQK, DV), F32)
    zn = jnp.zeros((1, HEADS, DQK), F32)
    zm = jnp.zeros((1, HEADS, GATE_LANES), F32)
    mix_m, c_m, n_m, m_m = _mlstm(proj_a, gcol_a, gate_rows(gcol_a, row_m, 1, 1, N_META), head_g, zc, zn, zm,
                                  row0=row_m, n_blocks=1, seqs=1, length=N_META, n_chunks=1)
    mix_p, c_p, n_p_, m_p = _mlstm(proj_p, gcol_p, gate_rows(gcol_p, 0, B * (S // CHUNK), 1, CHUNK), head_g,
                                   jnp.broadcast_to(c_m, (B,) + c_m.shape[1:]),
                                   jnp.broadcast_to(n_m, (B,) + n_m.shape[1:]),
                                   jnp.broadcast_to(m_m, (B,) + m_m.shape[1:]),
                                   row0=0, n_blocks=B, seqs=1, length=CHUNK, n_chunks=S // CHUNK)
    m0_s = jnp.broadcast_to(state_mlstm_m[0].astype(F32)[:, :, None], (DB, HEADS, GATE_LANES))
    mix_s, c_s, n_s_, m_s = _mlstm(proj_a, gcol_a, gate_rows(gcol_a, 0, DB // SAMPLE_SEQS, SAMPLE_SEQS, DS),
                                   head_g, state_mlstm_c[0].astype(F32), state_mlstm_n[0].astype(F32), m0_s,
                                   row0=0, n_blocks=DB // SAMPLE_SEQS, seqs=SAMPLE_SEQS, length=DS, n_chunks=1)
    mix_a = jnp.concatenate([mix_s, mix_m, pad_a], axis=0)
    h_p, h_a = matmul_norm_res(mix_p, mix_a, mlstm_w_out[0].astype(F32), gain(norm_mix_post[0]))
    h_p, h_a = ffn(0)

    u_p, u_a = norm_matmul(pool_w_in[0].astype(F32), D, gain(norm_mix_pre[1]), F32)
    w_group = pool_w_group[0].astype(F32)
    p_scale = pool_scale[0].reshape(1, D).astype(F32)
    pmix_m = _pool_rows(u_a, jnp.zeros((N_META, D), F32), w_group, p_scale, row0=row_m, n_tiles=1,
                        tr=N_META, tiles_per_seq=1, pos0=0, first_block=0)
    pmix_p = _pool_rows(u_p, u_a, w_group, p_scale, row0=0, n_tiles=n_p // POOL_TR, tr=POOL_TR,
                        tiles_per_seq=S // POOL_TR, pos0=N_META, first_block=row_m // N_META)
    ext_s = jnp.concatenate([jnp.zeros((DB, 1, D), F32), state_pool[0].astype(F32),
                             u_a[:n_s].reshape(DB, DS, D)], axis=1)
    pmix_s = _pool_seqs(ext_s, w_group, p_scale, seqs_per_block=POOL_SEQS, pos0=PAST_LEN)
    pmix_a = jnp.concatenate([pmix_s, pmix_m, pad_a], axis=0)
    h_p, h_a = matmul_norm_res(pmix_p, pmix_a, pool_w_out[0].astype(F32), gain(norm_mix_post[1]))
    h_p, h_a = ffn(1)

    y_prompt = h_p.reshape(B, S, D)
    y_sample = h_a[:n_s].reshape(DB, DS, D)
    dt_c, dt_n, dt_m, dt_pool = state_mlstm_c.dtype, state_mlstm_n.dtype, state_mlstm_m.dtype, state_pool.dtype
    u_prompt = u_p.reshape(B, S, D)
    return (y_prompt, y_sample,
            c_p[None].astype(dt_c), n_p_[None].astype(dt_n), m_p[None, :, :, 0].astype(dt_m),
            u_prompt[None, :, S - POOL_BUF:].astype(dt_pool),
            c_s[None].astype(dt_c), n_s_[None].astype(dt_n), m_s[None, :, :, 0].astype(dt_m),
            ext_s[None, :, -POOL_BUF:].astype(dt_pool))
```

```python
import functools

import jax
import jax.numpy as jnp
from jax import lax
from jax.experimental import pallas as pl
from jax.experimental.pallas import tpu as pltpu

D = 2048
N_META = 16
HEADS = 4
DQK = 256
DV = 512
D_FF = 4 * D
POOL_WINDOWS = (2, 4, 8, 16)
POOL_GROUP_DIM = D // len(POOL_WINDOWS)
POOL_BUF = 15
PAST_LEN = 16384
EPS = 1e-6
K_SCALE = DQK ** -0.5
PROJ_MAIN = 2 * HEADS * DQK + 2 * HEADS * DV
GATE_LANES = 128

VMEM_LIMIT = 56 * 1024 * 1024

F32 = jnp.float32
BF16 = jnp.bfloat16


def _params(semantics):
    return pltpu.CompilerParams(dimension_semantics=semantics, vmem_limit_bytes=VMEM_LIMIT)


def _rmsnorm(x, g):
    return x * lax.rsqrt(jnp.mean(x * x, axis=-1, keepdims=True) + EPS) * g


def _dot(a, b):
    return jnp.dot(a, b, preferred_element_type=F32)


def _norm_matmul_kernel(x_ref, g_ref, w_ref, o_ref, xn_ref):
    @pl.when(pl.program_id(1) == 0)
    def _():
        xn_ref[...] = _rmsnorm(x_ref[...], g_ref[...]).astype(BF16)

    o_ref[...] = _dot(xn_ref[...], w_ref[...].astype(BF16)).astype(o_ref.dtype)


def _norm_matmul_gates_kernel(x_ref, g_ref, w_ref, wg_ref, bg_ref, o_ref, gate_ref, xn_ref):
    @pl.when(pl.program_id(1) == 0)
    def _():
        xn = _rmsnorm(x_ref[...], g_ref[...]).astype(BF16)
        xn_ref[...] = xn
        z = _dot(xn, wg_ref[...]) + bg_ref[...]
        lane = lax.broadcasted_iota(jnp.int32, z.shape, 1)
        log_sig = jnp.minimum(z, 0.0) - jnp.log1p(jnp.exp(-jnp.abs(z)))
        gate_ref[...] = jnp.where(lane >= HEADS, log_sig, z)

    o_ref[...] = _dot(xn_ref[...], w_ref[...].astype(BF16)).astype(o_ref.dtype)


def _norm_matmul(x, g, w, *, n, tm, tn, out_dtype, gates=None):
    rows = x.shape[0]
    grid = (rows // tm, n // tn)
    x_spec = pl.BlockSpec((tm, D), lambda i, j: (i, 0))
    g_spec = pl.BlockSpec((1, D), lambda i, j: (0, 0))
    w_spec = pl.BlockSpec((None, D, tn), lambda i, j: (0, 0, j))
    o_spec = pl.BlockSpec((tm, tn), lambda i, j: (i, j))
    scratch = [pltpu.VMEM((tm, D), BF16)]
    if gates is None:
        return pl.pallas_call(
            _norm_matmul_kernel,
            out_shape=jax.ShapeDtypeStruct((rows, n), out_dtype),
            grid=grid, in_specs=[x_spec, g_spec, w_spec], out_specs=o_spec,
            scratch_shapes=scratch, compiler_params=_params(("parallel", "arbitrary")),
            name="norm_matmul",
        )(x, g, w)
    wg, bg = gates
    return pl.pallas_call(
        _norm_matmul_gates_kernel,
        out_shape=(jax.ShapeDtypeStruct((rows, n), out_dtype),
                   jax.ShapeDtypeStruct((rows, GATE_LANES), F32)),
        grid=grid,
        in_specs=[x_spec, g_spec, w_spec,
                  pl.BlockSpec((D, GATE_LANES), lambda i, j: (0, 0)),
                  pl.BlockSpec((1, GATE_LANES), lambda i, j: (0, 0))],
        out_specs=(o_spec, pl.BlockSpec((tm, GATE_LANES), lambda i, j: (i, 0))),
        scratch_shapes=scratch, compiler_params=_params(("parallel", "arbitrary")),
        name="norm_matmul_gates",
    )(x, g, w, wg, bg)


def _matmul_norm_res_kernel(a_ref, w_ref, g_ref, h_ref, o_ref, wb_ref):
    @pl.when(pl.program_id(0) == 0)
    def _():
        wb_ref[...] = w_ref[...].astype(BF16)

    y = _dot(a_ref[...], wb_ref[...])
    o_ref[...] = h_ref[...] + _rmsnorm(y, g_ref[...])


def _matmul_norm_res(a, w, g, h, *, tm):
    rows = a.shape[0]
    return pl.pallas_call(
        _matmul_norm_res_kernel,
        out_shape=jax.ShapeDtypeStruct((rows, D), F32),
        grid=(rows // tm,),
        in_specs=[pl.BlockSpec((tm, D), lambda i: (i, 0)),
                  pl.BlockSpec((None, D, D), lambda i: (0, 0, 0), pipeline_mode=pl.Buffered(1)),
                  pl.BlockSpec((1, D), lambda i: (0, 0)),
                  pl.BlockSpec((tm, D), lambda i: (i, 0))],
        out_specs=pl.BlockSpec((tm, D), lambda i: (i, 0)),
        scratch_shapes=[pltpu.VMEM((D, D), BF16)],
        compiler_params=_params(("arbitrary",)),
        name="matmul_norm_res",
    )(a, w, g, h)


def _ffn_step(f, n_f, h_ref, gpre_ref, w_up, w_down, gpost_ref, o_ref, xn_ref):
    @pl.when(f == 0)
    def _():
        xn_ref[...] = _rmsnorm(h_ref[...], gpre_ref[...]).astype(BF16)
        o_ref[...] = jnp.zeros_like(o_ref)

    a = jnp.maximum(_dot(xn_ref[...], w_up), 0.0)
    o_ref[...] += _dot((a * a).astype(BF16), w_down)

    @pl.when(f == n_f - 1)
    def _():
        o_ref[...] = h_ref[...] + _rmsnorm(o_ref[...], gpost_ref[...])


def _ffn_kernel(h_ref, gpre_ref, wup_ref, wdown_ref, gpost_ref, o_ref, xn_ref):
    _ffn_step(pl.program_id(1), pl.num_programs(1), h_ref, gpre_ref, wup_ref[...], wdown_ref[...],
              gpost_ref, o_ref, xn_ref)


def _ffn_cast_kernel(h_ref, gpre_ref, wup_ref, wdown_ref, gpost_ref, o_ref, wupb_ref, wdownb_ref, xn_ref):
    w_up = wup_ref[...].astype(BF16)
    w_down = wdown_ref[...].astype(BF16)
    wupb_ref[...] = w_up
    wdownb_ref[...] = w_down
    _ffn_step(pl.program_id(0), pl.num_programs(0), h_ref, gpre_ref, w_up, w_down, gpost_ref, o_ref, xn_ref)


def _ffn(h, g_pre, w_up, w_down, g_post, *, tm, tf):
    rows = h.shape[0]
    return pl.pallas_call(
        _ffn_kernel,
        out_shape=jax.ShapeDtypeStruct((rows, D), F32),
        grid=(rows // tm, D_FF // tf),
        in_specs=[pl.BlockSpec((tm, D), lambda i, f: (i, 0)),
                  pl.BlockSpec((1, D), lambda i, f: (0, 0)),
                  pl.BlockSpec((D, tf), lambda i, f: (0, f)),
                  pl.BlockSpec((tf, D), lambda i, f: (f, 0)),
                  pl.BlockSpec((1, D), lambda i, f: (0, 0))],
        out_specs=pl.BlockSpec((tm, D), lambda i, f: (i, 0)),
        scratch_shapes=[pltpu.VMEM((tm, D), BF16)],
        compiler_params=_params(("parallel", "arbitrary")),
        name="ffn",
    )(h, g_pre, w_up, w_down, g_post)


def _ffn_cast(h, g_pre, w_up, w_down, g_post, *, layer, tf):
    rows = h.shape[0]
    single = pl.Buffered(1)
    return pl.pallas_call(
        _ffn_cast_kernel,
        out_shape=(jax.ShapeDtypeStruct((rows, D), F32),
                   jax.ShapeDtypeStruct((D, D_FF), BF16),
                   jax.ShapeDtypeStruct((D_FF, D), BF16)),
        grid=(D_FF // tf,),
        in_specs=[pl.BlockSpec((rows, D), lambda f: (0, 0), pipeline_mode=single),
                  pl.BlockSpec((1, D), lambda f: (0, 0)),
                  pl.BlockSpec((None, D, tf), lambda f: (layer, 0, f)),
                  pl.BlockSpec((None, tf, D), lambda f: (layer, f, 0)),
                  pl.BlockSpec((1, D), lambda f: (0, 0))],
        out_specs=(pl.BlockSpec((rows, D), lambda f: (0, 0), pipeline_mode=single),
                   pl.BlockSpec((D, tf), lambda f: (0, f)),
                   pl.BlockSpec((tf, D), lambda f: (f, 0))),
        scratch_shapes=[pltpu.VMEM((rows, D), BF16)],
        compiler_params=_params(("arbitrary",)),
        name="ffn_cast",
    )(h, g_pre, w_up, w_down, g_post)


def _mlstm_chunk(q, k, v, li_row, lf_row, li_col, lf_col, c_state, n_row, m):
    L = q.shape[0]
    t_idx = lax.broadcasted_iota(jnp.int32, (L, L), 0)
    s_idx = lax.broadcasted_iota(jnp.int32, (L, L), 1)
    causal = s_idx <= t_idx
    b_col = jnp.sum(jnp.where(causal, lf_row, 0.0), axis=1, keepdims=True)
    b_row = jnp.sum(jnp.where(t_idx <= s_idx, lf_col, 0.0), axis=0, keepdims=True)
    b_tot = b_col[L - 1:L, :]
    dmat = jnp.where(causal, b_col - b_row + li_row, -jnp.inf)
    m_inter = b_col + m
    m_t = jnp.maximum(m_inter, jnp.max(dmat, axis=1, keepdims=True))
    p = jnp.exp(dmat - m_t)
    w_inter = jnp.exp(m_inter - m_t)
    qk = lax.dot_general(q, k, (((1,), (1,)), ((), ())), preferred_element_type=F32)
    scores = qk * (p * K_SCALE)
    num = _dot(scores.astype(BF16), v) + _dot(q, c_state.astype(BF16)) * w_inter
    qn = jnp.sum(q.astype(F32) * n_row, axis=1, keepdims=True)
    den = jnp.sum(scores, axis=1, keepdims=True) + w_inter * qn
    h = num / jnp.maximum(jnp.abs(den), jnp.exp(-m_t))
    m_new = m_t[L - 1:L, :]
    decay = jnp.exp(b_tot - b_col + li_col - m_new)
    carry = jnp.exp(b_tot + m - m_new)
    kd = k.astype(F32) * (decay * K_SCALE)
    c_new = carry * c_state + lax.dot_general(
        kd.astype(BF16), v, (((0,), (0,)), ((), ())), preferred_element_type=F32)
    n_new = carry * n_row + jnp.sum(kd, axis=0, keepdims=True)
    return h, c_new, n_new, m_new


def _mlstm_kernel(qk_ref, v_ref, o_ref, gcol_ref, grow_ref, hg_ref, c0_ref, n0_ref, m0_ref,
                  out_ref, c_out_ref, n_out_ref, m_out_ref, c_scr, n_scr, m_scr, *, seqs, length):
    c_idx = pl.program_id(1)

    @pl.when(c_idx == 0)
    def _():
        c_scr[...] = c0_ref[...]
        n_scr[...] = n0_ref[...]
        m_scr[...] = m0_ref[...]

    for s in range(seqs):
        rows = slice(s * length, (s + 1) * length)
        gcol = gcol_ref[rows, :]
        grow = grow_ref[0, s]
        for hd in range(HEADS):
            q = qk_ref[rows, hd * DQK:(hd + 1) * DQK]
            k = qk_ref[rows, (HEADS + hd) * DQK:(HEADS + hd + 1) * DQK]
            v = v_ref[rows, hd * DV:(hd + 1) * DV]
            h, c_new, n_new, m_new = _mlstm_chunk(
                q, k, v,
                grow[hd:hd + 1, :], grow[HEADS + hd:HEADS + hd + 1, :],
                gcol[:, hd:hd + 1], gcol[:, HEADS + hd:HEADS + hd + 1],
                c_scr[s, hd], n_scr[s, hd:hd + 1, :], m_scr[s, hd:hd + 1, 0:1])
            c_scr[s, hd] = c_new
            n_scr[s, hd:hd + 1, :] = n_new
            m_scr[s, hd:hd + 1, :] = jnp.broadcast_to(m_new, (1, GATE_LANES))
            hn = _rmsnorm(h, hg_ref[:, hd * DV:(hd + 1) * DV])
            gate = jax.nn.sigmoid(o_ref[rows, hd * DV:(hd + 1) * DV].astype(F32))
            out_ref[rows, hd * DV:(hd + 1) * DV] = (hn * gate).astype(out_ref.dtype)

    @pl.when(c_idx == pl.num_programs(1) - 1)
    def _():
        c_out_ref[...] = c_scr[...]
        n_out_ref[...] = n_scr[...]
        m_out_ref[...] = m_scr[...]


def _mlstm(proj, gcol, grow, head_g, c0, n0, m0, *, row0, n_blocks, seqs, length, n_chunks):
    assert seqs == 1 or n_chunks == 1
    blk = seqs * length
    assert row0 % blk == 0
    b0 = row0 // blk
    n_seq = n_blocks * seqs

    def rows_map(col):
        return lambda b, c: (b0 + b * n_chunks + c, col)

    kernel = functools.partial(_mlstm_kernel, seqs=seqs, length=length)
    return pl.pallas_call(
        kernel,
        out_shape=(jax.ShapeDtypeStruct((n_blocks * n_chunks * blk, HEADS * DV), BF16),
                   jax.ShapeDtypeStruct((n_seq, HEADS, DQK, DV), F32),
                   jax.ShapeDtypeStruct((n_seq, HEADS, DQK), F32),
                   jax.ShapeDtypeStruct((n_seq, HEADS, GATE_LANES), F32)),
        grid=(n_blocks, n_chunks),
        in_specs=[pl.BlockSpec((blk, 2 * HEADS * DQK), rows_map(0)),
                  pl.BlockSpec((blk, HEADS * DV), rows_map(1)),
                  pl.BlockSpec((blk, HEADS * DV), rows_map(2)),
                  pl.BlockSpec((blk, GATE_LANES), rows_map(0)),
                  pl.BlockSpec((1, seqs, 2 * HEADS, length), lambda b, c: (b * n_chunks + c, 0, 0, 0)),
                  pl.BlockSpec((1, HEADS * DV), lambda b, c: (0, 0)),
                  pl.BlockSpec((seqs, HEADS, DQK, DV), lambda b, c: (b, 0, 0, 0)),
                  pl.BlockSpec((seqs, HEADS, DQK), lambda b, c: (b, 0, 0)),
                  pl.BlockSpec((seqs, HEADS, GATE_LANES), lambda b, c: (b, 0, 0))],
        out_specs=(pl.BlockSpec((blk, HEADS * DV), lambda b, c: (b * n_chunks + c, 0)),
                   pl.BlockSpec((seqs, HEADS, DQK, DV), lambda b, c: (b, 0, 0, 0)),
                   pl.BlockSpec((seqs, HEADS, DQK), lambda b, c: (b, 0, 0)),
                   pl.BlockSpec((seqs, HEADS, GATE_LANES), lambda b, c: (b, 0, 0))),
        scratch_shapes=[pltpu.VMEM((seqs, HEADS, DQK, DV), F32),
                        pltpu.VMEM((seqs, HEADS, DQK), F32),
                        pltpu.VMEM((seqs, HEADS, GATE_LANES), F32)],
        compiler_params=_params(("parallel", "arbitrary")),
        name=f"mlstm_l{length}",
    )(proj, proj, proj, gcol, grow, head_g, c0, n0, m0)


def _pool_mix(ext, first, n_rows, pos, wg_ref, scale_ref):
    outs = []
    for g, w in enumerate(POOL_WINDOWS):
        cols = slice(g * POOL_GROUP_DIM, (g + 1) * POOL_GROUP_DIM)
        e = ext[:, cols]
        total, span = e, 1
        while span < w:
            total = total + pltpu.roll(total, span, 0)
            span *= 2
        cnt = jnp.minimum(float(w), pos + 1.0)
        pooled = total[first:first + n_rows] / cnt - e[first:first + n_rows]
        outs.append(_dot(pooled.astype(BF16), wg_ref[g].astype(BF16)))
    mixed = jnp.concatenate(outs, axis=-1) * scale_ref[...]
    return mixed.astype(BF16)


def _pool_rows_kernel(u_ref, prev_ref, first_ref, wg_ref, scale_ref, o_ref, *, tiles_per_seq, pos0):
    tr = u_ref.shape[0]
    tile = pl.program_id(0) % tiles_per_seq
    halo = jnp.where(tile == 0, first_ref[...], prev_ref[...])
    ext = jnp.concatenate([halo, u_ref[...]], axis=0)
    pos = (lax.broadcasted_iota(jnp.int32, (tr, 1), 0) + (tile * tr + pos0)).astype(F32)
    o_ref[...] = _pool_mix(ext, N_META, tr, pos, wg_ref, scale_ref)


def _pool_rows(u, first_src, w_group, scale, *, row0, n_tiles, tr, tiles_per_seq, pos0, first_block):
    hb = tr // N_META
    t0 = row0 // tr
    kernel = functools.partial(_pool_rows_kernel, tiles_per_seq=tiles_per_seq, pos0=pos0)
    return pl.pallas_call(
        kernel,
        out_shape=jax.ShapeDtypeStruct((n_tiles * tr, D), BF16),
        grid=(n_tiles,),
        in_specs=[pl.BlockSpec((tr, D), lambda i: (t0 + i, 0)),
                  pl.BlockSpec((N_META, D), lambda i: (jnp.maximum((t0 + i) * hb - 1, 0), 0)),
                  pl.BlockSpec((N_META, D), lambda i: (first_block, 0)),
                  pl.BlockSpec((None, len(POOL_WINDOWS), POOL_GROUP_DIM, POOL_GROUP_DIM),
                               lambda i: (0, 0, 0, 0)),
                  pl.BlockSpec((1, D), lambda i: (0, 0))],
        out_specs=pl.BlockSpec((tr, D), lambda i: (i, 0)),
        compiler_params=_params(("parallel",)),
        name=f"pool_rows_{tr}",
    )(u, u, first_src, w_group, scale)


def _pool_seqs_kernel(ext_ref, wg_ref, scale_ref, o_ref, *, pos0):
    g, r, _ = ext_ref.shape
    n_new = r - 1 - POOL_BUF
    ext = ext_ref[...].reshape(g * r, D)
    outs = []
    pos = (lax.broadcasted_iota(jnp.int32, (g, n_new, 1), 1) + pos0).astype(F32)
    for gi, w in enumerate(POOL_WINDOWS):
        cols = slice(gi * POOL_GROUP_DIM, (gi + 1) * POOL_GROUP_DIM)
        e = ext[:, cols]
        total, span = e, 1
        while span < w:
            total = total + pltpu.roll(total, span, 0)
            span *= 2
        cnt = jnp.minimum(float(w), pos + 1.0)
        tot3 = total.reshape(g, r, POOL_GROUP_DIM)[:, 1 + POOL_BUF:, :]
        e3 = e.reshape(g, r, POOL_GROUP_DIM)[:, 1 + POOL_BUF:, :]
        pooled = (tot3 / cnt - e3).reshape(g * n_new, POOL_GROUP_DIM)
        outs.append(_dot(pooled.astype(BF16), wg_ref[gi].astype(BF16)))
    mixed = jnp.concatenate(outs, axis=-1) * scale_ref[...]
    o_ref[...] = mixed.astype(BF16)


def _pool_seqs(ext3, w_group, scale, *, seqs_per_block, pos0):
    n_seq, r, _ = ext3.shape
    n_new = r - 1 - POOL_BUF
    kernel = functools.partial(_pool_seqs_kernel, pos0=pos0)
    return pl.pallas_call(
        kernel,
        out_shape=jax.ShapeDtypeStruct((n_seq * n_new, D), BF16),
        grid=(n_seq // seqs_per_block,),
        in_specs=[pl.BlockSpec((seqs_per_block, r, D), lambda i: (i, 0, 0)),
                  pl.BlockSpec((None, len(POOL_WINDOWS), POOL_GROUP_DIM, POOL_GROUP_DIM),
                               lambda i: (0, 0, 0, 0)),
                  pl.BlockSpec((1, D), lambda i: (0, 0))],
        out_specs=pl.BlockSpec((seqs_per_block * n_new, D), lambda i: (i, 0)),
        compiler_params=_params(("parallel",)),
        name="pool_seqs",
    )(ext3, w_group, scale)


TM_PROJ = 1024
TN_PROJ = 1024
TM = 512
TM_AUX = 544
TM_FFN = 1024
TF = 512
TF_AUX = 256
CHUNK = 256
SAMPLE_SEQS = 2
POOL_TR = 512
POOL_SEQS = 16


def kernel(x_prompt, x_sample, state_mlstm_c, state_mlstm_n, state_mlstm_m, state_pool, meta_tokens,
           norm_mix_pre, norm_mix_post, norm_ffn_pre, norm_ffn_post, mlstm_w_in, mlstm_b_i, mlstm_b_f,
           mlstm_head_norm, mlstm_w_out, pool_w_in, pool_w_group, pool_scale, pool_w_out,
           ffn_w_up, ffn_w_down):
    B, S, _ = x_prompt.shape
    DB, DS, _ = x_sample.shape
    n_p, n_s = B * S, DB * DS
    rows_a = 2 * TM_AUX
    row_m = n_s
    assert n_s + N_META <= rows_a and n_p % TM_PROJ == 0 and n_p % TM == 0
    assert S % CHUNK == 0 and S % POOL_TR == 0 and row_m % N_META == 0

    h_p = x_prompt.reshape(n_p, D)
    h_a = jnp.concatenate([x_sample.reshape(n_s, D), meta_tokens.astype(F32),
                           jnp.zeros((rows_a - n_s - N_META, D), F32)], axis=0)
    pad_a = jnp.zeros((rows_a - n_s - N_META, D), BF16)

    def gain(a):
        return a.reshape(1, D).astype(F32)

    def norm_matmul(w, n, g, out_dtype, gates=None):
        return (_norm_matmul(h_p, g, w, n=n, tm=TM_PROJ, tn=TN_PROJ, out_dtype=out_dtype, gates=gates),
                _norm_matmul(h_a, g, w, n=n, tm=TM_AUX, tn=TN_PROJ, out_dtype=out_dtype, gates=gates))

    def matmul_norm_res(a_p, a_a, w, g):
        return _matmul_norm_res(a_p, w, g, h_p, tm=TM), _matmul_norm_res(a_a, w, g, h_a, tm=TM_AUX)

    def ffn(layer):
        g_pre, g_post = gain(norm_ffn_pre[layer]), gain(norm_ffn_post[layer])
        out_a, w_up, w_down = _ffn_cast(h_a, g_pre, ffn_w_up.astype(F32), ffn_w_down.astype(F32), g_post,
                                        layer=layer, tf=TF_AUX)
        return _ffn(h_p, g_pre, w_up, w_down, g_post, tm=TM_FFN, tf=TF), out_a

    w_in = mlstm_w_in[:1].astype(F32)
    w_gate = jnp.pad(w_in[0, :, PROJ_MAIN:], ((0, 0), (0, GATE_LANES - 2 * HEADS))).astype(BF16)
    b_gate = jnp.pad(jnp.concatenate([mlstm_b_i[0], mlstm_b_f[0]]).astype(F32),
                     (0, GATE_LANES - 2 * HEADS)).reshape(1, GATE_LANES)
    (proj_p, gcol_p), (proj_a, gcol_a) = norm_matmul(w_in, PROJ_MAIN, gain(norm_mix_pre[0]), BF16,
                                                     gates=(w_gate, b_gate))

    def gate_rows(gcol, r0, n_blk, seqs, length):
        g8 = gcol[r0:r0 + n_blk * seqs * length, :2 * HEADS]
        return g8.reshape(n_blk, seqs, length, 2 * HEADS).transpose(0, 1, 3, 2)

    head_g = mlstm_head_norm[0].reshape(1, HEADS * DV).astype(F32)
    zc = jnp.zeros((1, HEADS, DQK, DV), F32)
    zn = jnp.zeros((1, HEADS, DQK), F32)
    zm = jnp.zeros((1, HEADS, GATE_LANES), F32)
    mix_m, c_m, n_m, m_m = _mlstm(proj_a, gcol_a, gate_rows(gcol_a, row_m, 1, 1, N_META), head_g, zc, zn, zm,
                                  row0=row_m, n_blocks=1, seqs=1, length=N_META, n_chunks=1)
    mix_p, c_p, n_p_, m_p = _mlstm(proj_p, gcol_p, gate_rows(gcol_p, 0, B * (S // CHUNK), 1, CHUNK), head_g,
                                   jnp.broadcast_to(c_m, (B,) + c_m.shape[1:]),
                                   jnp.broadcast_to(n_m, (B,) + n_m.shape[1:]),
                                   jnp.broadcast_to(m_m, (B,) + m_m.shape[1:]),
                                   row0=0, n_blocks=B, seqs=1, length=CHUNK, n_chunks=S // CHUNK)
    m0_s = jnp.broadcast_to(state_mlstm_m[0].astype(F32)[:, :, None], (DB, HEADS, GATE_LANES))
    mix_s, c_s, n_s_, m_s = _mlstm(proj_a, gcol_a, gate_rows(gcol_a, 0, DB // SAMPLE_SEQS, SAMPLE_SEQS, DS),
                                   head_g, state_mlstm_c[0].astype(F32), state_mlstm_n[0].astype(F32), m0_s,
                                   row0=0, n_blocks=DB // SAMPLE_SEQS, seqs=SAMPLE_SEQS, length=DS, n_chunks=1)
    mix_a = jnp.concatenate([mix_s, mix_m, pad_a], axis=0)
    h_p, h_a = matmul_norm_res(mix_p, mix_a, mlstm_w_out[:1].astype(F32), gain(norm_mix_post[0]))
    h_p, h_a = ffn(0)

    u_p, u_a = norm_matmul(pool_w_in[:1].astype(F32), D, gain(norm_mix_pre[1]), F32)
    w_group = pool_w_group[:1].astype(F32)
    p_scale = pool_scale[0].reshape(1, D).astype(F32)
    pmix_m = _pool_rows(u_a, jnp.zeros((N_META, D), F32), w_group, p_scale, row0=row_m, n_tiles=1,
                        tr=N_META, tiles_per_seq=1, pos0=0, first_block=0)
    pmix_p = _pool_rows(u_p, u_a, w_group, p_scale, row0=0, n_tiles=n_p // POOL_TR, tr=POOL_TR,
                        tiles_per_seq=S // POOL_TR, pos0=N_META, first_block=row_m // N_META)
    ext_s = jnp.concatenate([jnp.zeros((DB, 1, D), F32), state_pool[0].astype(F32),
                             u_a[:n_s].reshape(DB, DS, D)], axis=1)
    pmix_s = _pool_seqs(ext_s, w_group, p_scale, seqs_per_block=POOL_SEQS, pos0=PAST_LEN)
    pmix_a = jnp.concatenate([pmix_s, pmix_m, pad_a], axis=0)
    h_p, h_a = matmul_norm_res(pmix_p, pmix_a, pool_w_out[:1].astype(F32), gain(norm_mix_post[1]))
    h_p, h_a = ffn(1)

    y_prompt = h_p.reshape(B, S, D)
    y_sample = h_a[:n_s].reshape(DB, DS, D)
    dt_c, dt_n, dt_m, dt_pool = state_mlstm_c.dtype, state_mlstm_n.dtype, state_mlstm_m.dtype, state_pool.dtype
    u_prompt = u_p.reshape(B, S, D)
    return (y_prompt, y_sample,
            c_p[None].astype(dt_c), n_p_[None].astype(dt_n), m_p[None, :, :, 0].astype(dt_m),
            u_prompt[None, :, S - POOL_BUF:].astype(dt_pool),
            c_s[None].astype(dt_c), n_s_[None].astype(dt_n), m_s[None, :, :, 0].astype(dt_m),
            ext_s[None, :, -POOL_BUF:].astype(dt_pool))
```

```python
import functools

import jax
import jax.numpy as jnp
from jax import lax
from jax.experimental import pallas as pl
from jax.experimental.pallas import tpu as pltpu

D = 2048
N_META = 16
HEADS = 4
DQK = 256
DV = 512
D_FF = 4 * D
POOL_WINDOWS = (2, 4, 8, 16)
POOL_GROUP_DIM = D // len(POOL_WINDOWS)
POOL_BUF = 15
PAST_LEN = 16384
EPS = 1e-6
K_SCALE = DQK ** -0.5
PROJ_MAIN = 2 * HEADS * DQK + 2 * HEADS * DV
GATE_LANES = 128

VMEM_LIMIT = 56 * 1024 * 1024

F32 = jnp.float32
BF16 = jnp.bfloat16


def _params(semantics):
    return pltpu.CompilerParams(dimension_semantics=semantics, vmem_limit_bytes=VMEM_LIMIT)


def _rmsnorm(x, g):
    return x * lax.rsqrt(jnp.mean(x * x, axis=-1, keepdims=True) + EPS) * g


def _dot(a, b):
    return jnp.dot(a, b, preferred_element_type=F32)


def _norm_matmul_kernel(x_ref, g_ref, w_ref, o_ref, xn_ref):
    @pl.when(pl.program_id(1) == 0)
    def _():
        xn_ref[...] = _rmsnorm(x_ref[...], g_ref[...]).astype(BF16)

    o_ref[...] = _dot(xn_ref[...], w_ref[...].astype(BF16)).astype(o_ref.dtype)


def _dot_t(a, b):
    return lax.dot_general(a, b, (((1,), (1,)), ((), ())), preferred_element_type=F32)


def _norm_matmul_gates_kernel(x_ref, g_ref, wt_ref, wgt_ref, bg_ref, o_ref, gate_ref, xn_ref):
    @pl.when(pl.program_id(1) == 0)
    def _():
        xn = _rmsnorm(x_ref[...], g_ref[...]).astype(BF16)
        xn_ref[...] = xn
        z = _dot_t(xn, wgt_ref[...]) + bg_ref[...]
        lane = lax.broadcasted_iota(jnp.int32, z.shape, 1)
        log_sig = jnp.minimum(z, 0.0) - jnp.log1p(jnp.exp(-jnp.abs(z)))
        gate_ref[...] = jnp.where(lane >= HEADS, log_sig, z)

    o_ref[...] = _dot_t(xn_ref[...], wt_ref[...].astype(BF16)).astype(o_ref.dtype)


def _norm_matmul(x, g, w, *, tm, tn, out_dtype):
    rows, n = x.shape[0], w.shape[2]
    return pl.pallas_call(
        _norm_matmul_kernel,
        out_shape=jax.ShapeDtypeStruct((rows, n), out_dtype),
        grid=(rows // tm, n // tn),
        in_specs=[pl.BlockSpec((tm, D), lambda i, j: (i, 0)),
                  pl.BlockSpec((1, D), lambda i, j: (0, 0)),
                  pl.BlockSpec((None, D, tn), lambda i, j: (0, 0, j))],
        out_specs=pl.BlockSpec((tm, tn), lambda i, j: (i, j)),
        scratch_shapes=[pltpu.VMEM((tm, D), BF16)],
        compiler_params=_params(("parallel", "arbitrary")),
        name="norm_matmul",
    )(x, g, w)


def _norm_matmul_gates(x, g, w_t, wg_t, bg, *, n, tm, tn):
    rows = x.shape[0]
    return pl.pallas_call(
        _norm_matmul_gates_kernel,
        out_shape=(jax.ShapeDtypeStruct((rows, n), BF16),
                   jax.ShapeDtypeStruct((rows, GATE_LANES), F32)),
        grid=(rows // tm, n // tn),
        in_specs=[pl.BlockSpec((tm, D), lambda i, j: (i, 0)),
                  pl.BlockSpec((1, D), lambda i, j: (0, 0)),
                  pl.BlockSpec((None, tn, D), lambda i, j: (0, j, 0)),
                  pl.BlockSpec((GATE_LANES, D), lambda i, j: (0, 0)),
                  pl.BlockSpec((1, GATE_LANES), lambda i, j: (0, 0))],
        out_specs=(pl.BlockSpec((tm, tn), lambda i, j: (i, j)),
                   pl.BlockSpec((tm, GATE_LANES), lambda i, j: (i, 0))),
        scratch_shapes=[pltpu.VMEM((tm, D), BF16)],
        compiler_params=_params(("parallel", "arbitrary")),
        name="norm_matmul_gates",
    )(x, g, w_t, wg_t, bg)


def _matmul_norm_res_kernel(a_ref, w_ref, g_ref, h_ref, o_ref, wb_ref):
    @pl.when(pl.program_id(0) == 0)
    def _():
        wb_ref[...] = w_ref[...].astype(BF16)

    y = _dot(a_ref[...], wb_ref[...])
    o_ref[...] = h_ref[...] + _rmsnorm(y, g_ref[...])


def _matmul_norm_res(a, w, g, h, *, tm):
    rows = a.shape[0]
    return pl.pallas_call(
        _matmul_norm_res_kernel,
        out_shape=jax.ShapeDtypeStruct((rows, D), F32),
        grid=(rows // tm,),
        in_specs=[pl.BlockSpec((tm, D), lambda i: (i, 0)),
                  pl.BlockSpec((None, D, D), lambda i: (0, 0, 0), pipeline_mode=pl.Buffered(1)),
                  pl.BlockSpec((1, D), lambda i: (0, 0)),
                  pl.BlockSpec((tm, D), lambda i: (i, 0))],
        out_specs=pl.BlockSpec((tm, D), lambda i: (i, 0)),
        scratch_shapes=[pltpu.VMEM((D, D), BF16)],
        compiler_params=_params(("arbitrary",)),
        name="matmul_norm_res",
    )(a, w, g, h)


def _ffn_step(f, n_f, h_ref, gpre_ref, w_up, w_down, gpost_ref, o_ref, xn_ref):
    @pl.when(f == 0)
    def _():
        xn_ref[...] = _rmsnorm(h_ref[...], gpre_ref[...]).astype(BF16)
        o_ref[...] = jnp.zeros_like(o_ref)

    a = jnp.maximum(_dot(xn_ref[...], w_up), 0.0)
    o_ref[...] += _dot((a * a).astype(BF16), w_down)

    @pl.when(f == n_f - 1)
    def _():
        o_ref[...] = h_ref[...] + _rmsnorm(o_ref[...], gpost_ref[...])


def _ffn_kernel(h_ref, gpre_ref, wup_ref, wdown_ref, gpost_ref, o_ref, xn_ref):
    _ffn_step(pl.program_id(1), pl.num_programs(1), h_ref, gpre_ref, wup_ref[...], wdown_ref[...],
              gpost_ref, o_ref, xn_ref)


def _ffn_cast_kernel(h_ref, gpre_ref, wup_ref, wdown_ref, gpost_ref, o_ref, wupb_ref, wdownb_ref, xn_ref):
    w_up = wup_ref[...].astype(BF16)
    w_down = wdown_ref[...].astype(BF16)

    @pl.when(pl.program_id(0) == 0)
    def _():
        wupb_ref[...] = w_up
        wdownb_ref[...] = w_down

    _ffn_step(pl.program_id(1), pl.num_programs(1), h_ref, gpre_ref, w_up, w_down, gpost_ref, o_ref, xn_ref)


def _ffn(h, g_pre, w_up, w_down, g_post, *, tm, tf):
    rows = h.shape[0]
    return pl.pallas_call(
        _ffn_kernel,
        out_shape=jax.ShapeDtypeStruct((rows, D), F32),
        grid=(rows // tm, D_FF // tf),
        in_specs=[pl.BlockSpec((tm, D), lambda i, f: (i, 0)),
                  pl.BlockSpec((1, D), lambda i, f: (0, 0)),
                  pl.BlockSpec((D, tf), lambda i, f: (0, f)),
                  pl.BlockSpec((tf, D), lambda i, f: (f, 0)),
                  pl.BlockSpec((1, D), lambda i, f: (0, 0))],
        out_specs=pl.BlockSpec((tm, D), lambda i, f: (i, 0)),
        scratch_shapes=[pltpu.VMEM((tm, D), BF16)],
        compiler_params=_params(("parallel", "arbitrary")),
        name="ffn",
    )(h, g_pre, w_up, w_down, g_post)


def _ffn_cast(h, g_pre, w_up, w_down, g_post, *, layer, tm, tf):
    rows = h.shape[0]
    n_f = D_FF // tf

    def copy_chunk(i, f):
        return jnp.where(i == 0, f, n_f - 1)

    return pl.pallas_call(
        _ffn_cast_kernel,
        out_shape=(jax.ShapeDtypeStruct((rows, D), F32),
                   jax.ShapeDtypeStruct((D, D_FF), BF16),
                   jax.ShapeDtypeStruct((D_FF, D), BF16)),
        grid=(rows // tm, n_f),
        in_specs=[pl.BlockSpec((tm, D), lambda i, f: (i, 0)),
                  pl.BlockSpec((1, D), lambda i, f: (0, 0)),
                  pl.BlockSpec((None, D, tf), lambda i, f: (layer, 0, f)),
                  pl.BlockSpec((None, tf, D), lambda i, f: (layer, f, 0)),
                  pl.BlockSpec((1, D), lambda i, f: (0, 0))],
        out_specs=(pl.BlockSpec((tm, D), lambda i, f: (i, 0)),
                   pl.BlockSpec((D, tf), lambda i, f: (0, copy_chunk(i, f))),
                   pl.BlockSpec((tf, D), lambda i, f: (copy_chunk(i, f), 0))),
        scratch_shapes=[pltpu.VMEM((tm, D), BF16)],
        compiler_params=_params(("arbitrary", "arbitrary")),
        name="ffn_cast",
    )(h, g_pre, w_up, w_down, g_post)


def _mlstm_chunk(q, k, v, li_row, lf_row, li_col, lf_col, c_state, n_row, m):
    L = q.shape[0]
    t_idx = lax.broadcasted_iota(jnp.int32, (L, L), 0)
    s_idx = lax.broadcasted_iota(jnp.int32, (L, L), 1)
    causal = s_idx <= t_idx
    b_col = jnp.sum(jnp.where(causal, lf_row, 0.0), axis=1, keepdims=True)
    b_row = jnp.sum(jnp.where(t_idx <= s_idx, lf_col, 0.0), axis=0, keepdims=True)
    b_tot = b_col[L - 1:L, :]
    dmat = jnp.where(causal, b_col - b_row + li_row, -jnp.inf)
    m_inter = b_col + m
    m_t = jnp.maximum(m_inter, jnp.max(dmat, axis=1, keepdims=True))
    p = jnp.exp(dmat - m_t)
    w_inter = jnp.exp(m_inter - m_t)
    qk = lax.dot_general(q, k, (((1,), (1,)), ((), ())), preferred_element_type=F32)
    scores = qk * (p * K_SCALE)
    num = _dot(scores.astype(BF16), v) + _dot(q, c_state.astype(BF16)) * w_inter
    qn = jnp.sum(q.astype(F32) * n_row, axis=1, keepdims=True)
    den = jnp.sum(scores, axis=1, keepdims=True) + w_inter * qn
    h = num / jnp.maximum(jnp.abs(den), jnp.exp(-m_t))
    m_new = m_t[L - 1:L, :]
    decay = jnp.exp(b_tot - b_col + li_col - m_new)
    carry = jnp.exp(b_tot + m - m_new)
    kd = k.astype(F32) * (decay * K_SCALE)
    c_new = carry * c_state + lax.dot_general(
        kd.astype(BF16), v, (((0,), (0,)), ((), ())), preferred_element_type=F32)
    n_new = carry * n_row + jnp.sum(kd, axis=0, keepdims=True)
    return h, c_new, n_new, m_new


def _mlstm_kernel(qk_ref, v_ref, o_ref, gcol_ref, grow_ref, hg_ref, c0_ref, n0_ref, m0_ref,
                  out_ref, c_out_ref, n_out_ref, m_out_ref, c_scr, n_scr, m_scr, *, seqs, length):
    c_idx = pl.program_id(1)

    @pl.when(c_idx == 0)
    def _():
        c_scr[...] = c0_ref[...]
        n_scr[...] = n0_ref[...]
        m_scr[...] = m0_ref[...]

    for s in range(seqs):
        rows = slice(s * length, (s + 1) * length)
        gcol = gcol_ref[rows, :]
        grow = grow_ref[0, s]
        for hd in range(HEADS):
            q = qk_ref[rows, hd * DQK:(hd + 1) * DQK]
            k = qk_ref[rows, (HEADS + hd) * DQK:(HEADS + hd + 1) * DQK]
            v = v_ref[rows, hd * DV:(hd + 1) * DV]
            h, c_new, n_new, m_new = _mlstm_chunk(
                q, k, v,
                grow[hd:hd + 1, :], grow[HEADS + hd:HEADS + hd + 1, :],
                gcol[:, hd:hd + 1], gcol[:, HEADS + hd:HEADS + hd + 1],
                c_scr[s, hd], n_scr[s, hd:hd + 1, :], m_scr[s, hd:hd + 1, 0:1])
            c_scr[s, hd] = c_new
            n_scr[s, hd:hd + 1, :] = n_new
            m_scr[s, hd:hd + 1, :] = jnp.broadcast_to(m_new, (1, GATE_LANES))
            hn = _rmsnorm(h, hg_ref[:, hd * DV:(hd + 1) * DV])
            gate = jax.nn.sigmoid(o_ref[rows, hd * DV:(hd + 1) * DV].astype(F32))
            out_ref[rows, hd * DV:(hd + 1) * DV] = (hn * gate).astype(out_ref.dtype)

    @pl.when(c_idx == pl.num_programs(1) - 1)
    def _():
        c_out_ref[...] = c_scr[...]
        n_out_ref[...] = n_scr[...]
        m_out_ref[...] = m_scr[...]


def _mlstm(proj, gcol, grow, head_g, c0, n0, m0, *, row0, n_blocks, seqs, length, n_chunks):
    assert seqs == 1 or n_chunks == 1
    blk = seqs * length
    assert row0 % blk == 0
    b0 = row0 // blk
    n_seq = n_blocks * seqs

    def rows_map(col):
        return lambda b, c: (b0 + b * n_chunks + c, col)

    kernel = functools.partial(_mlstm_kernel, seqs=seqs, length=length)
    return pl.pallas_call(
        kernel,
        out_shape=(jax.ShapeDtypeStruct((n_blocks * n_chunks * blk, HEADS * DV), BF16),
                   jax.ShapeDtypeStruct((n_seq, HEADS, DQK, DV), F32),
                   jax.ShapeDtypeStruct((n_seq, HEADS, DQK), F32),
                   jax.ShapeDtypeStruct((n_seq, HEADS, GATE_LANES), F32)),
        grid=(n_blocks, n_chunks),
        in_specs=[pl.BlockSpec((blk, 2 * HEADS * DQK), rows_map(0)),
                  pl.BlockSpec((blk, HEADS * DV), rows_map(1)),
                  pl.BlockSpec((blk, HEADS * DV), rows_map(2)),
                  pl.BlockSpec((blk, GATE_LANES), rows_map(0)),
                  pl.BlockSpec((1, seqs, 2 * HEADS, length), lambda b, c: (b * n_chunks + c, 0, 0, 0)),
                  pl.BlockSpec((1, HEADS * DV), lambda b, c: (0, 0)),
                  pl.BlockSpec((seqs, HEADS, DQK, DV), lambda b, c: (b, 0, 0, 0)),
                  pl.BlockSpec((seqs, HEADS, DQK), lambda b, c: (b, 0, 0)),
                  pl.BlockSpec((seqs, HEADS, GATE_LANES), lambda b, c: (b, 0, 0))],
        out_specs=(pl.BlockSpec((blk, HEADS * DV), lambda b, c: (b * n_chunks + c, 0)),
                   pl.BlockSpec((seqs, HEADS, DQK, DV), lambda b, c: (b, 0, 0, 0)),
                   pl.BlockSpec((seqs, HEADS, DQK), lambda b, c: (b, 0, 0)),
                   pl.BlockSpec((seqs, HEADS, GATE_LANES), lambda b, c: (b, 0, 0))),
        scratch_shapes=[pltpu.VMEM((seqs, HEADS, DQK, DV), F32),
                        pltpu.VMEM((seqs, HEADS, DQK), F32),
                        pltpu.VMEM((seqs, HEADS, GATE_LANES), F32)],
        compiler_params=_params(("parallel", "arbitrary")),
        name=f"mlstm_l{length}",
    )(proj, proj, proj, gcol, grow, head_g, c0, n0, m0)


def _pool_mix(ext, first, n_rows, pos, wg_ref, scale_ref):
    outs = []
    for g, w in enumerate(POOL_WINDOWS):
        cols = slice(g * POOL_GROUP_DIM, (g + 1) * POOL_GROUP_DIM)
        e = ext[:, cols]
        total, span = e, 1
        while span < w:
            total = total + pltpu.roll(total, span, 0)
            span *= 2
        cnt = jnp.minimum(float(w), pos + 1.0)
        pooled = total[first:first + n_rows] / cnt - e[first:first + n_rows]
        outs.append(_dot(pooled.astype(BF16), wg_ref[g].astype(BF16)))
    mixed = jnp.concatenate(outs, axis=-1) * scale_ref[...]
    return mixed.astype(BF16)


def _pool_rows_kernel(u_ref, prev_ref, first_ref, wg_ref, scale_ref, o_ref, *, tiles_per_seq, pos0):
    tr = u_ref.shape[0]
    tile = pl.program_id(0) % tiles_per_seq
    halo = jnp.where(tile == 0, first_ref[...], prev_ref[...])
    ext = jnp.concatenate([halo, u_ref[...]], axis=0)
    pos = (lax.broadcasted_iota(jnp.int32, (tr, 1), 0) + (tile * tr + pos0)).astype(F32)
    o_ref[...] = _pool_mix(ext, N_META, tr, pos, wg_ref, scale_ref)


def _pool_rows(u, first_src, w_group, scale, *, row0, n_tiles, tr, tiles_per_seq, pos0, first_block):
    hb = tr // N_META
    t0 = row0 // tr
    kernel = functools.partial(_pool_rows_kernel, tiles_per_seq=tiles_per_seq, pos0=pos0)
    return pl.pallas_call(
        kernel,
        out_shape=jax.ShapeDtypeStruct((n_tiles * tr, D), BF16),
        grid=(n_tiles,),
        in_specs=[pl.BlockSpec((tr, D), lambda i: (t0 + i, 0)),
                  pl.BlockSpec((N_META, D), lambda i: (jnp.maximum((t0 + i) * hb - 1, 0), 0)),
                  pl.BlockSpec((N_META, D), lambda i: (first_block, 0)),
                  pl.BlockSpec((None, len(POOL_WINDOWS), POOL_GROUP_DIM, POOL_GROUP_DIM),
                               lambda i: (0, 0, 0, 0)),
                  pl.BlockSpec((1, D), lambda i: (0, 0))],
        out_specs=pl.BlockSpec((tr, D), lambda i: (i, 0)),
        compiler_params=_params(("parallel",)),
        name=f"pool_rows_{tr}",
    )(u, u, first_src, w_group, scale)


def _pool_seqs_kernel(ext_ref, wg_ref, scale_ref, o_ref, *, pos0):
    g, r, _ = ext_ref.shape
    n_new = r - 1 - POOL_BUF
    ext = ext_ref[...].reshape(g * r, D)
    outs = []
    pos = (lax.broadcasted_iota(jnp.int32, (g, n_new, 1), 1) + pos0).astype(F32)
    for gi, w in enumerate(POOL_WINDOWS):
        cols = slice(gi * POOL_GROUP_DIM, (gi + 1) * POOL_GROUP_DIM)
        e = ext[:, cols]
        total, span = e, 1
        while span < w:
            total = total + pltpu.roll(total, span, 0)
            span *= 2
        cnt = jnp.minimum(float(w), pos + 1.0)
        tot3 = total.reshape(g, r, POOL_GROUP_DIM)[:, 1 + POOL_BUF:, :]
        e3 = e.reshape(g, r, POOL_GROUP_DIM)[:, 1 + POOL_BUF:, :]
        pooled = (tot3 / cnt - e3).reshape(g * n_new, POOL_GROUP_DIM)
        outs.append(_dot(pooled.astype(BF16), wg_ref[gi].astype(BF16)))
    mixed = jnp.concatenate(outs, axis=-1) * scale_ref[...]
    o_ref[...] = mixed.astype(BF16)


def _pool_seqs(ext3, w_group, scale, *, seqs_per_block, pos0):
    n_seq, r, _ = ext3.shape
    n_new = r - 1 - POOL_BUF
    kernel = functools.partial(_pool_seqs_kernel, pos0=pos0)
    return pl.pallas_call(
        kernel,
        out_shape=jax.ShapeDtypeStruct((n_seq * n_new, D), BF16),
        grid=(n_seq // seqs_per_block,),
        in_specs=[pl.BlockSpec((seqs_per_block, r, D), lambda i: (i, 0, 0)),
                  pl.BlockSpec((None, len(POOL_WINDOWS), POOL_GROUP_DIM, POOL_GROUP_DIM),
                               lambda i: (0, 0, 0, 0)),
                  pl.BlockSpec((1, D), lambda i: (0, 0))],
        out_specs=pl.BlockSpec((seqs_per_block * n_new, D), lambda i: (i, 0)),
        compiler_params=_params(("parallel",)),
        name="pool_seqs",
    )(ext3, w_group, scale)


TM_PROJ = 1024
TN_PROJ = 1024
TM = 512
TM_AUX = 544
TM_FFN = 512
TF = 1024
TF_AUX = 512
CHUNK = 256
SAMPLE_SEQS = 2
POOL_TR = 512
POOL_SEQS = 16


def kernel(x_prompt, x_sample, state_mlstm_c, state_mlstm_n, state_mlstm_m, state_pool, meta_tokens,
           norm_mix_pre, norm_mix_post, norm_ffn_pre, norm_ffn_post, mlstm_w_in, mlstm_b_i, mlstm_b_f,
           mlstm_head_norm, mlstm_w_out, pool_w_in, pool_w_group, pool_scale, pool_w_out,
           ffn_w_up, ffn_w_down):
    B, S, _ = x_prompt.shape
    DB, DS, _ = x_sample.shape
    n_p, n_s = B * S, DB * DS
    rows_a = 2 * TM_AUX
    row_m = n_s
    assert n_s + N_META <= rows_a and n_p % TM_PROJ == 0 and n_p % TM == 0
    assert S % CHUNK == 0 and S % POOL_TR == 0 and row_m % N_META == 0

    h_p = x_prompt.reshape(n_p, D)
    h_a = jnp.concatenate([x_sample.reshape(n_s, D), meta_tokens.astype(F32),
                           jnp.zeros((rows_a - n_s - N_META, D), F32)], axis=0)
    pad_a = jnp.zeros((rows_a - n_s - N_META, D), BF16)

    def gain(a):
        return a.reshape(1, D).astype(F32)


    def matmul_norm_res(a_p, a_a, w, g):
        return _matmul_norm_res(a_p, w, g, h_p, tm=TM), _matmul_norm_res(a_a, w, g, h_a, tm=TM_AUX)

    def ffn(layer):
        g_pre, g_post = gain(norm_ffn_pre[layer]), gain(norm_ffn_post[layer])
        out_a, w_up, w_down = _ffn_cast(h_a, g_pre, ffn_w_up.astype(F32), ffn_w_down.astype(F32), g_post,
                                        layer=layer, tm=TM_AUX, tf=TF_AUX)
        return _ffn(h_p, g_pre, w_up, w_down, g_post, tm=TM_FFN, tf=TF), out_a

    w_in_t = jnp.swapaxes(mlstm_w_in[:1].astype(F32), 1, 2)
    w_gate_t = jnp.pad(w_in_t[0, PROJ_MAIN:], ((0, GATE_LANES - 2 * HEADS), (0, 0))).astype(BF16)
    b_gate = jnp.pad(jnp.concatenate([mlstm_b_i[0], mlstm_b_f[0]]).astype(F32),
                     (0, GATE_LANES - 2 * HEADS)).reshape(1, GATE_LANES)
    g_mix = gain(norm_mix_pre[0])
    proj_p, gcol_p = _norm_matmul_gates(h_p, g_mix, w_in_t, w_gate_t, b_gate, n=PROJ_MAIN, tm=TM_PROJ, tn=TN_PROJ)
    proj_a, gcol_a = _norm_matmul_gates(h_a, g_mix, w_in_t, w_gate_t, b_gate, n=PROJ_MAIN, tm=TM_AUX, tn=TN_PROJ)

    def gate_rows(gcol, r0, n_blk, seqs, length):
        g8 = gcol[r0:r0 + n_blk * seqs * length, :2 * HEADS]
        return g8.reshape(n_blk, seqs, length, 2 * HEADS).transpose(0, 1, 3, 2)

    head_g = mlstm_head_norm[0].reshape(1, HEADS * DV).astype(F32)
    zc = jnp.zeros((1, HEADS, DQK, DV), F32)
    zn = jnp.zeros((1, HEADS, DQK), F32)
    zm = jnp.zeros((1, HEADS, GATE_LANES), F32)
    mix_m, c_m, n_m, m_m = _mlstm(proj_a, gcol_a, gate_rows(gcol_a, row_m, 1, 1, N_META), head_g, zc, zn, zm,
                                  row0=row_m, n_blocks=1, seqs=1, length=N_META, n_chunks=1)
    mix_p, c_p, n_p_, m_p = _mlstm(proj_p, gcol_p, gate_rows(gcol_p, 0, B * (S // CHUNK), 1, CHUNK), head_g,
                                   jnp.broadcast_to(c_m, (B,) + c_m.shape[1:]),
                                   jnp.broadcast_to(n_m, (B,) + n_m.shape[1:]),
                                   jnp.broadcast_to(m_m, (B,) + m_m.shape[1:]),
                                   row0=0, n_blocks=B, seqs=1, length=CHUNK, n_chunks=S // CHUNK)
    m0_s = jnp.broadcast_to(state_mlstm_m[0].astype(F32)[:, :, None], (DB, HEADS, GATE_LANES))
    mix_s, c_s, n_s_, m_s = _mlstm(proj_a, gcol_a, gate_rows(gcol_a, 0, DB // SAMPLE_SEQS, SAMPLE_SEQS, DS),
                                   head_g, state_mlstm_c[0].astype(F32), state_mlstm_n[0].astype(F32), m0_s,
                                   row0=0, n_blocks=DB // SAMPLE_SEQS, seqs=SAMPLE_SEQS, length=DS, n_chunks=1)
    mix_a = jnp.concatenate([mix_s, mix_m, pad_a], axis=0)
    h_p, h_a = matmul_norm_res(mix_p, mix_a, mlstm_w_out[:1].astype(F32), gain(norm_mix_post[0]))
    h_p, h_a = ffn(0)

    w_pool_in, g_mix = pool_w_in[:1].astype(F32), gain(norm_mix_pre[1])
    u_p = _norm_matmul(h_p, g_mix, w_pool_in, tm=TM_PROJ, tn=TN_PROJ, out_dtype=F32)
    u_a = _norm_matmul(h_a, g_mix, w_pool_in, tm=TM_AUX, tn=TN_PROJ, out_dtype=F32)
    w_group = pool_w_group[:1].astype(F32)
    p_scale = pool_scale[0].reshape(1, D).astype(F32)
    pmix_m = _pool_rows(u_a, jnp.zeros((N_META, D), F32), w_group, p_scale, row0=row_m, n_tiles=1,
                        tr=N_META, tiles_per_seq=1, pos0=0, first_block=0)
    pmix_p = _pool_rows(u_p, u_a, w_group, p_scale, row0=0, n_tiles=n_p // POOL_TR, tr=POOL_TR,
                        tiles_per_seq=S // POOL_TR, pos0=N_META, first_block=row_m // N_META)
    ext_s = jnp.concatenate([jnp.zeros((DB, 1, D), F32), state_pool[0].astype(F32),
                             u_a[:n_s].reshape(DB, DS, D)], axis=1)
    pmix_s = _pool_seqs(ext_s, w_group, p_scale, seqs_per_block=POOL_SEQS, pos0=PAST_LEN)
    pmix_a = jnp.concatenate([pmix_s, pmix_m, pad_a], axis=0)
    h_p, h_a = matmul_norm_res(pmix_p, pmix_a, pool_w_out[:1].astype(F32), gain(norm_mix_post[1]))
    h_p, h_a = ffn(1)

    y_prompt = h_p.reshape(B, S, D)
    y_sample = h_a[:n_s].reshape(DB, DS, D)
    dt_c, dt_n, dt_m, dt_pool = state_mlstm_c.dtype, state_mlstm_n.dtype, state_mlstm_m.dtype, state_pool.dtype
    u_prompt = u_p.reshape(B, S, D)
    return (y_prompt, y_sample,
            c_p[None].astype(dt_c), n_p_[None].astype(dt_n), m_p[None, :, :, 0].astype(dt_m),
            u_prompt[None, :, S - POOL_BUF:].astype(dt_pool),
            c_s[None].astype(dt_c), n_s_[None].astype(dt_n), m_s[None, :, :, 0].astype(dt_m),
            ext_s[None, :, -POOL_BUF:].astype(dt_pool))
```

```python
import functools

import jax
import jax.numpy as jnp
from jax import lax
from jax.experimental import pallas as pl
from jax.experimental.pallas import tpu as pltpu

D = 2048
N_META = 16
HEADS = 4
DQK = 256
DV = 512
D_FF = 4 * D
POOL_WINDOWS = (2, 4, 8, 16)
POOL_GROUP_DIM = D // len(POOL_WINDOWS)
POOL_BUF = 15
PAST_LEN = 16384
EPS = 1e-6
K_SCALE = DQK ** -0.5
PROJ_MAIN = 2 * HEADS * DQK + 2 * HEADS * DV
GATE_LANES = 128

VMEM_LIMIT = 56 * 1024 * 1024

F32 = jnp.float32
BF16 = jnp.bfloat16


def _params(semantics):
    return pltpu.CompilerParams(dimension_semantics=semantics, vmem_limit_bytes=VMEM_LIMIT)


def _rmsnorm(x, g):
    return x * lax.rsqrt(jnp.mean(x * x, axis=-1, keepdims=True) + EPS) * g


def _dot(a, b):
    return jnp.dot(a, b, preferred_element_type=F32)


def _norm_matmul_kernel(x_ref, g_ref, w_ref, o_ref, xn_ref):
    @pl.when(pl.program_id(1) == 0)
    def _():
        xn_ref[...] = _rmsnorm(x_ref[...], g_ref[...]).astype(BF16)

    o_ref[...] = _dot(xn_ref[...], w_ref[...].astype(BF16)).astype(o_ref.dtype)


def _dot_t(a, b):
    return lax.dot_general(a, b, (((1,), (1,)), ((), ())), preferred_element_type=F32)


def _norm_matmul_gates_kernel(x_ref, g_ref, wt_ref, wgt_ref, bg_ref, o_ref, gate_ref, xn_ref):
    @pl.when(pl.program_id(1) == 0)
    def _():
        xn = _rmsnorm(x_ref[...], g_ref[...]).astype(BF16)
        xn_ref[...] = xn
        z = _dot_t(xn, wgt_ref[...]) + bg_ref[...]
        lane = lax.broadcasted_iota(jnp.int32, z.shape, 1)
        log_sig = jnp.minimum(z, 0.0) - jnp.log1p(jnp.exp(-jnp.abs(z)))
        gate_ref[...] = jnp.where(lane >= HEADS, log_sig, z)

    o_ref[...] = _dot_t(xn_ref[...], wt_ref[...].astype(BF16)).astype(o_ref.dtype)


def _norm_matmul(x, g, w, *, tm, tn, out_dtype):
    rows, n = x.shape[0], w.shape[2]
    return pl.pallas_call(
        _norm_matmul_kernel,
        out_shape=jax.ShapeDtypeStruct((rows, n), out_dtype),
        grid=(rows // tm, n // tn),
        in_specs=[pl.BlockSpec((tm, D), lambda i, j: (i, 0)),
                  pl.BlockSpec((1, D), lambda i, j: (0, 0)),
                  pl.BlockSpec((None, D, tn), lambda i, j: (0, 0, j))],
        out_specs=pl.BlockSpec((tm, tn), lambda i, j: (i, j)),
        scratch_shapes=[pltpu.VMEM((tm, D), BF16)],
        compiler_params=_params(("parallel", "arbitrary")),
        name="norm_matmul",
    )(x, g, w)


def _norm_matmul_gates(x, g, w_t, wg_t, bg, *, n, tm, tn):
    rows = x.shape[0]
    return pl.pallas_call(
        _norm_matmul_gates_kernel,
        out_shape=(jax.ShapeDtypeStruct((rows, n), BF16),
                   jax.ShapeDtypeStruct((rows, GATE_LANES), F32)),
        grid=(rows // tm, n // tn),
        in_specs=[pl.BlockSpec((tm, D), lambda i, j: (i, 0)),
                  pl.BlockSpec((1, D), lambda i, j: (0, 0)),
                  pl.BlockSpec((None, tn, D), lambda i, j: (0, j, 0)),
                  pl.BlockSpec((GATE_LANES, D), lambda i, j: (0, 0)),
                  pl.BlockSpec((1, GATE_LANES), lambda i, j: (0, 0))],
        out_specs=(pl.BlockSpec((tm, tn), lambda i, j: (i, j)),
                   pl.BlockSpec((tm, GATE_LANES), lambda i, j: (i, 0))),
        scratch_shapes=[pltpu.VMEM((tm, D), BF16)],
        compiler_params=_params(("parallel", "arbitrary")),
        name="norm_matmul_gates",
    )(x, g, w_t, wg_t, bg)


def _matmul_norm_res_kernel(a_ref, w_ref, g_ref, h_ref, o_ref, wb_ref):
    @pl.when(pl.program_id(0) == 0)
    def _():
        wb_ref[...] = w_ref[...].astype(BF16)

    y = _dot(a_ref[...], wb_ref[...])
    o_ref[...] = h_ref[...] + _rmsnorm(y, g_ref[...])


def _matmul_norm_res(a, w, g, h, *, tm):
    rows = a.shape[0]
    return pl.pallas_call(
        _matmul_norm_res_kernel,
        out_shape=jax.ShapeDtypeStruct((rows, D), F32),
        grid=(rows // tm,),
        in_specs=[pl.BlockSpec((tm, D), lambda i: (i, 0)),
                  pl.BlockSpec((None, D, D), lambda i: (0, 0, 0), pipeline_mode=pl.Buffered(1)),
                  pl.BlockSpec((1, D), lambda i: (0, 0)),
                  pl.BlockSpec((tm, D), lambda i: (i, 0))],
        out_specs=pl.BlockSpec((tm, D), lambda i: (i, 0)),
        scratch_shapes=[pltpu.VMEM((D, D), BF16)],
        compiler_params=_params(("arbitrary",)),
        name="matmul_norm_res",
    )(a, w, g, h)


def _ffn_begin(h_ref, gpre_ref, o_ref, xn_ref):
    @pl.when(pl.program_id(1) == 0)
    def _():
        xn_ref[...] = _rmsnorm(h_ref[...], gpre_ref[...]).astype(BF16)
        o_ref[...] = jnp.zeros_like(o_ref)


def _ffn_end(h_ref, gpost_ref, o_ref):
    @pl.when(pl.program_id(1) == pl.num_programs(1) - 1)
    def _():
        o_ref[...] = h_ref[...] + _rmsnorm(o_ref[...], gpost_ref[...])


def _ffn_kernel(h_ref, gpre_ref, wup_ref, wdown_ref, gpost_ref, o_ref, xn_ref):
    _ffn_begin(h_ref, gpre_ref, o_ref, xn_ref)
    a = jnp.maximum(_dot(xn_ref[...], wup_ref[...]), 0.0)
    o_ref[...] += _dot((a * a).astype(BF16), wdown_ref[...])
    _ffn_end(h_ref, gpost_ref, o_ref)


def _ffn_cast_kernel(h_ref, gpre_ref, wup_ref, wdown_ref, gpost_ref, o_ref, wupb_ref, wdownb_ref, xn_ref,
                     wup_scr, wdown_scr):
    _ffn_begin(h_ref, gpre_ref, o_ref, xn_ref)
    wup_scr[...] = wup_ref[...].astype(BF16)
    wdown_scr[...] = wdown_ref[...].astype(BF16)
    a = jnp.maximum(_dot(xn_ref[...], wup_scr[...]), 0.0)
    o_ref[...] += _dot((a * a).astype(BF16), wdown_scr[...])

    @pl.when(pl.program_id(0) == 0)
    def _():
        wupb_ref[...] = wup_scr[...]
        wdownb_ref[...] = wdown_scr[...]

    _ffn_end(h_ref, gpost_ref, o_ref)


def _ffn(h, g_pre, w_up, w_down, g_post, *, tm, tf):
    rows = h.shape[0]
    return pl.pallas_call(
        _ffn_kernel,
        out_shape=jax.ShapeDtypeStruct((rows, D), F32),
        grid=(rows // tm, D_FF // tf),
        in_specs=[pl.BlockSpec((tm, D), lambda i, f: (i, 0)),
                  pl.BlockSpec((1, D), lambda i, f: (0, 0)),
                  pl.BlockSpec((D, tf), lambda i, f: (0, f)),
                  pl.BlockSpec((tf, D), lambda i, f: (f, 0)),
                  pl.BlockSpec((1, D), lambda i, f: (0, 0))],
        out_specs=pl.BlockSpec((tm, D), lambda i, f: (i, 0)),
        scratch_shapes=[pltpu.VMEM((tm, D), BF16)],
        compiler_params=_params(("parallel", "arbitrary")),
        name="ffn",
    )(h, g_pre, w_up, w_down, g_post)


def _ffn_cast(h, g_pre, w_up, w_down, g_post, *, layer, tm, tf):
    rows = h.shape[0]
    n_f = D_FF // tf

    def copy_chunk(i, f):
        return jnp.where(i == 0, f, n_f - 1)

    return pl.pallas_call(
        _ffn_cast_kernel,
        out_shape=(jax.ShapeDtypeStruct((rows, D), F32),
                   jax.ShapeDtypeStruct((D, D_FF), BF16),
                   jax.ShapeDtypeStruct((D_FF, D), BF16)),
        grid=(rows // tm, n_f),
        in_specs=[pl.BlockSpec((tm, D), lambda i, f: (i, 0)),
                  pl.BlockSpec((1, D), lambda i, f: (0, 0)),
                  pl.BlockSpec((None, D, tf), lambda i, f: (layer, 0, f)),
                  pl.BlockSpec((None, tf, D), lambda i, f: (layer, f, 0)),
                  pl.BlockSpec((1, D), lambda i, f: (0, 0))],
        out_specs=(pl.BlockSpec((tm, D), lambda i, f: (i, 0)),
                   pl.BlockSpec((D, tf), lambda i, f: (0, copy_chunk(i, f))),
                   pl.BlockSpec((tf, D), lambda i, f: (copy_chunk(i, f), 0))),
        scratch_shapes=[pltpu.VMEM((tm, D), BF16), pltpu.VMEM((D, tf), BF16), pltpu.VMEM((tf, D), BF16)],
        compiler_params=_params(("arbitrary", "arbitrary")),
        name="ffn_cast",
    )(h, g_pre, w_up, w_down, g_post)


def _mlstm_chunk(q, k, v, li_row, lf_row, li_col, lf_col, c_state, n_row, m):
    L = q.shape[0]
    t_idx = lax.broadcasted_iota(jnp.int32, (L, L), 0)
    s_idx = lax.broadcasted_iota(jnp.int32, (L, L), 1)
    causal = s_idx <= t_idx
    b_col = jnp.sum(jnp.where(causal, lf_row, 0.0), axis=1, keepdims=True)
    b_row = jnp.sum(jnp.where(t_idx <= s_idx, lf_col, 0.0), axis=0, keepdims=True)
    b_tot = b_col[L - 1:L, :]
    dmat = jnp.where(causal, b_col - b_row + li_row, -jnp.inf)
    m_inter = b_col + m
    m_t = jnp.maximum(m_inter, jnp.max(dmat, axis=1, keepdims=True))
    p = jnp.exp(dmat - m_t)
    w_inter = jnp.exp(m_inter - m_t)
    qk = lax.dot_general(q, k, (((1,), (1,)), ((), ())), preferred_element_type=F32)
    scores = qk * (p * K_SCALE)
    num = _dot(scores.astype(BF16), v) + _dot(q, c_state.astype(BF16)) * w_inter
    qn = jnp.sum(q.astype(F32) * n_row, axis=1, keepdims=True)
    den = jnp.sum(scores, axis=1, keepdims=True) + w_inter * qn
    h = num / jnp.maximum(jnp.abs(den), jnp.exp(-m_t))
    m_new = m_t[L - 1:L, :]
    decay = jnp.exp(b_tot - b_col + li_col - m_new)
    carry = jnp.exp(b_tot + m - m_new)
    kd = k.astype(F32) * (decay * K_SCALE)
    c_new = carry * c_state + lax.dot_general(
        kd.astype(BF16), v, (((0,), (0,)), ((), ())), preferred_element_type=F32)
    n_new = carry * n_row + jnp.sum(kd, axis=0, keepdims=True)
    return h, c_new, n_new, m_new


def _mlstm_kernel(qk_ref, v_ref, o_ref, gcol_ref, grow_ref, hg_ref, c0_ref, n0_ref, m0_ref,
                  out_ref, c_out_ref, n_out_ref, m_out_ref, c_scr, n_scr, m_scr, *, seqs, length):
    c_idx = pl.program_id(1)

    @pl.when(c_idx == 0)
    def _():
        c_scr[...] = c0_ref[...]
        n_scr[...] = n0_ref[...]
        m_scr[...] = m0_ref[...]

    for s in range(seqs):
        rows = slice(s * length, (s + 1) * length)
        gcol = gcol_ref[rows, :]
        grow = grow_ref[0, s]
        for hd in range(HEADS):
            q = qk_ref[rows, hd * DQK:(hd + 1) * DQK]
            k = qk_ref[rows, (HEADS + hd) * DQK:(HEADS + hd + 1) * DQK]
            v = v_ref[rows, hd * DV:(hd + 1) * DV]
            h, c_new, n_new, m_new = _mlstm_chunk(
                q, k, v,
                grow[hd:hd + 1, :], grow[HEADS + hd:HEADS + hd + 1, :],
                gcol[:, hd:hd + 1], gcol[:, HEADS + hd:HEADS + hd + 1],
                c_scr[s, hd], n_scr[s, hd:hd + 1, :], m_scr[s, hd:hd + 1, 0:1])
            c_scr[s, hd] = c_new
            n_scr[s, hd:hd + 1, :] = n_new
            m_scr[s, hd:hd + 1, :] = jnp.broadcast_to(m_new, (1, GATE_LANES))
            hn = _rmsnorm(h, hg_ref[:, hd * DV:(hd + 1) * DV])
            gate = jax.nn.sigmoid(o_ref[rows, hd * DV:(hd + 1) * DV].astype(F32))
            out_ref[rows, hd * DV:(hd + 1) * DV] = (hn * gate).astype(out_ref.dtype)

    @pl.when(c_idx == pl.num_programs(1) - 1)
    def _():
        c_out_ref[...] = c_scr[...]
        n_out_ref[...] = n_scr[...]
        m_out_ref[...] = m_scr[...]


def _mlstm(proj, gcol, grow, head_g, c0, n0, m0, *, row0, n_blocks, seqs, length, n_chunks):
    assert seqs == 1 or n_chunks == 1
    blk = seqs * length
    assert row0 % blk == 0
    b0 = row0 // blk
    n_seq = n_blocks * seqs

    def rows_map(col):
        return lambda b, c: (b0 + b * n_chunks + c, col)

    kernel = functools.partial(_mlstm_kernel, seqs=seqs, length=length)
    return pl.pallas_call(
        kernel,
        out_shape=(jax.ShapeDtypeStruct((n_blocks * n_chunks * blk, HEADS * DV), BF16),
                   jax.ShapeDtypeStruct((n_seq, HEADS, DQK, DV), F32),
                   jax.ShapeDtypeStruct((n_seq, HEADS, DQK), F32),
                   jax.ShapeDtypeStruct((n_seq, HEADS, GATE_LANES), F32)),
        grid=(n_blocks, n_chunks),
        in_specs=[pl.BlockSpec((blk, 2 * HEADS * DQK), rows_map(0)),
                  pl.BlockSpec((blk, HEADS * DV), rows_map(1)),
                  pl.BlockSpec((blk, HEADS * DV), rows_map(2)),
                  pl.BlockSpec((blk, GATE_LANES), rows_map(0)),
                  pl.BlockSpec((1, seqs, 2 * HEADS, length), lambda b, c: (b * n_chunks + c, 0, 0, 0)),
                  pl.BlockSpec((1, HEADS * DV), lambda b, c: (0, 0)),
                  pl.BlockSpec((seqs, HEADS, DQK, DV), lambda b, c: (b, 0, 0, 0)),
                  pl.BlockSpec((seqs, HEADS, DQK), lambda b, c: (b, 0, 0)),
                  pl.BlockSpec((seqs, HEADS, GATE_LANES), lambda b, c: (b, 0, 0))],
        out_specs=(pl.BlockSpec((blk, HEADS * DV), lambda b, c: (b * n_chunks + c, 0)),
                   pl.BlockSpec((seqs, HEADS, DQK, DV), lambda b, c: (b, 0, 0, 0)),
                   pl.BlockSpec((seqs, HEADS, DQK), lambda b, c: (b, 0, 0)),
                   pl.BlockSpec((seqs, HEADS, GATE_LANES), lambda b, c: (b, 0, 0))),
        scratch_shapes=[pltpu.VMEM((seqs, HEADS, DQK, DV), F32),
                        pltpu.VMEM((seqs, HEADS, DQK), F32),
                        pltpu.VMEM((seqs, HEADS, GATE_LANES), F32)],
        compiler_params=_params(("parallel", "arbitrary")),
        name=f"mlstm_l{length}",
    )(proj, proj, proj, gcol, grow, head_g, c0, n0, m0)


def _pool_mix(ext, first, n_rows, pos, wg_ref, scale_ref):
    outs = []
    for g, w in enumerate(POOL_WINDOWS):
        cols = slice(g * POOL_GROUP_DIM, (g + 1) * POOL_GROUP_DIM)
        e = ext[:, cols]
        total, span = e, 1
        while span < w:
            total = total + pltpu.roll(total, span, 0)
            span *= 2
        cnt = jnp.minimum(float(w), pos + 1.0)
        pooled = total[first:first + n_rows] / cnt - e[first:first + n_rows]
        outs.append(_dot(pooled.astype(BF16), wg_ref[g].astype(BF16)))
    mixed = jnp.concatenate(outs, axis=-1) * scale_ref[...]
    return mixed.astype(BF16)


def _pool_rows_kernel(u_ref, prev_ref, first_ref, wg_ref, scale_ref, o_ref, *, tiles_per_seq, pos0):
    tr = u_ref.shape[0]
    tile = pl.program_id(0) % tiles_per_seq
    halo = jnp.where(tile == 0, first_ref[...], prev_ref[...])
    ext = jnp.concatenate([halo, u_ref[...]], axis=0)
    pos = (lax.broadcasted_iota(jnp.int32, (tr, 1), 0) + (tile * tr + pos0)).astype(F32)
    o_ref[...] = _pool_mix(ext, N_META, tr, pos, wg_ref, scale_ref)


def _pool_rows(u, first_src, w_group, scale, *, row0, n_tiles, tr, tiles_per_seq, pos0, first_block):
    hb = tr // N_META
    t0 = row0 // tr
    kernel = functools.partial(_pool_rows_kernel, tiles_per_seq=tiles_per_seq, pos0=pos0)
    return pl.pallas_call(
        kernel,
        out_shape=jax.ShapeDtypeStruct((n_tiles * tr, D), BF16),
        grid=(n_tiles,),
        in_specs=[pl.BlockSpec((tr, D), lambda i: (t0 + i, 0)),
                  pl.BlockSpec((N_META, D), lambda i: (jnp.maximum((t0 + i) * hb - 1, 0), 0)),
                  pl.BlockSpec((N_META, D), lambda i: (first_block, 0)),
                  pl.BlockSpec((None, len(POOL_WINDOWS), POOL_GROUP_DIM, POOL_GROUP_DIM),
                               lambda i: (0, 0, 0, 0)),
                  pl.BlockSpec((1, D), lambda i: (0, 0))],
        out_specs=pl.BlockSpec((tr, D), lambda i: (i, 0)),
        compiler_params=_params(("parallel",)),
        name=f"pool_rows_{tr}",
    )(u, u, first_src, w_group, scale)


def _pool_seqs_kernel(ext_ref, wg_ref, scale_ref, o_ref, *, pos0):
    g, r, _ = ext_ref.shape
    n_new = r - 1 - POOL_BUF
    ext = ext_ref[...].reshape(g * r, D)
    outs = []
    pos = (lax.broadcasted_iota(jnp.int32, (g, n_new, 1), 1) + pos0).astype(F32)
    for gi, w in enumerate(POOL_WINDOWS):
        cols = slice(gi * POOL_GROUP_DIM, (gi + 1) * POOL_GROUP_DIM)
        e = ext[:, cols]
        total, span = e, 1
        while span < w:
            total = total + pltpu.roll(total, span, 0)
            span *= 2
        cnt = jnp.minimum(float(w), pos + 1.0)
        tot3 = total.reshape(g, r, POOL_GROUP_DIM)[:, 1 + POOL_BUF:, :]
        e3 = e.reshape(g, r, POOL_GROUP_DIM)[:, 1 + POOL_BUF:, :]
        pooled = (tot3 / cnt - e3).reshape(g * n_new, POOL_GROUP_DIM)
        outs.append(_dot(pooled.astype(BF16), wg_ref[gi].astype(BF16)))
    mixed = jnp.concatenate(outs, axis=-1) * scale_ref[...]
    o_ref[...] = mixed.astype(BF16)


def _pool_seqs(ext3, w_group, scale, *, seqs_per_block, pos0):
    n_seq, r, _ = ext3.shape
    n_new = r - 1 - POOL_BUF
    kernel = functools.partial(_pool_seqs_kernel, pos0=pos0)
    return pl.pallas_call(
        kernel,
        out_shape=jax.ShapeDtypeStruct((n_seq * n_new, D), BF16),
        grid=(n_seq // seqs_per_block,),
        in_specs=[pl.BlockSpec((seqs_per_block, r, D), lambda i: (i, 0, 0)),
                  pl.BlockSpec((None, len(POOL_WINDOWS), POOL_GROUP_DIM, POOL_GROUP_DIM),
                               lambda i: (0, 0, 0, 0)),
                  pl.BlockSpec((1, D), lambda i: (0, 0))],
        out_specs=pl.BlockSpec((seqs_per_block * n_new, D), lambda i: (i, 0)),
        compiler_params=_params(("parallel",)),
        name="pool_seqs",
    )(ext3, w_group, scale)


TM_PROJ = 1024
TN_PROJ = 1024
TM = 512
TM_AUX = 544
TM_FFN = 512
TF = 1024
TF_AUX = 512
CHUNK = 256
SAMPLE_SEQS = 2
POOL_TR = 512
POOL_SEQS = 16


def kernel(x_prompt, x_sample, state_mlstm_c, state_mlstm_n, state_mlstm_m, state_pool, meta_tokens,
           norm_mix_pre, norm_mix_post, norm_ffn_pre, norm_ffn_post, mlstm_w_in, mlstm_b_i, mlstm_b_f,
           mlstm_head_norm, mlstm_w_out, pool_w_in, pool_w_group, pool_scale, pool_w_out,
           ffn_w_up, ffn_w_down):
    B, S, _ = x_prompt.shape
    DB, DS, _ = x_sample.shape
    n_p, n_s = B * S, DB * DS
    rows_a = 2 * TM_AUX
    row_m = n_s
    assert n_s + N_META <= rows_a and n_p % TM_PROJ == 0 and n_p % TM == 0
    assert S % CHUNK == 0 and S % POOL_TR == 0 and row_m % N_META == 0

    h_p = x_prompt.reshape(n_p, D)
    h_a = jnp.concatenate([x_sample.reshape(n_s, D), meta_tokens.astype(F32),
                           jnp.zeros((rows_a - n_s - N_META, D), F32)], axis=0)
    pad_a = jnp.zeros((rows_a - n_s - N_META, D), BF16)

    def gain(a):
        return a.reshape(1, D).astype(F32)


    def matmul_norm_res(a_p, a_a, w, g):
        return _matmul_norm_res(a_p, w, g, h_p, tm=TM), _matmul_norm_res(a_a, w, g, h_a, tm=TM_AUX)

    def ffn(layer):
        g_pre, g_post = gain(norm_ffn_pre[layer]), gain(norm_ffn_post[layer])
        out_a, w_up, w_down = _ffn_cast(h_a, g_pre, ffn_w_up.astype(F32), ffn_w_down.astype(F32), g_post,
                                        layer=layer, tm=TM_AUX, tf=TF_AUX)
        return _ffn(h_p, g_pre, w_up, w_down, g_post, tm=TM_FFN, tf=TF), out_a

    w_in_t = jnp.swapaxes(mlstm_w_in[:1].astype(F32), 1, 2)
    w_gate_t = jnp.pad(w_in_t[0, PROJ_MAIN:], ((0, GATE_LANES - 2 * HEADS), (0, 0))).astype(BF16)
    b_gate = jnp.pad(jnp.concatenate([mlstm_b_i[0], mlstm_b_f[0]]).astype(F32),
                     (0, GATE_LANES - 2 * HEADS)).reshape(1, GATE_LANES)
    g_mix = gain(norm_mix_pre[0])
    proj_p, gcol_p = _norm_matmul_gates(h_p, g_mix, w_in_t, w_gate_t, b_gate, n=PROJ_MAIN, tm=TM_PROJ, tn=TN_PROJ)
    proj_a, gcol_a = _norm_matmul_gates(h_a, g_mix, w_in_t, w_gate_t, b_gate, n=PROJ_MAIN, tm=TM_AUX, tn=TN_PROJ)

    def gate_rows(gcol, r0, n_blk, seqs, length):
        g8 = gcol[r0:r0 + n_blk * seqs * length, :2 * HEADS]
        return g8.reshape(n_blk, seqs, length, 2 * HEADS).transpose(0, 1, 3, 2)

    head_g = mlstm_head_norm[0].reshape(1, HEADS * DV).astype(F32)
    zc = jnp.zeros((1, HEADS, DQK, DV), F32)
    zn = jnp.zeros((1, HEADS, DQK), F32)
    zm = jnp.zeros((1, HEADS, GATE_LANES), F32)
    mix_m, c_m, n_m, m_m = _mlstm(proj_a, gcol_a, gate_rows(gcol_a, row_m, 1, 1, N_META), head_g, zc, zn, zm,
                                  row0=row_m, n_blocks=1, seqs=1, length=N_META, n_chunks=1)
    mix_p, c_p, n_p_, m_p = _mlstm(proj_p, gcol_p, gate_rows(gcol_p, 0, B * (S // CHUNK), 1, CHUNK), head_g,
                                   jnp.broadcast_to(c_m, (B,) + c_m.shape[1:]),
                                   jnp.broadcast_to(n_m, (B,) + n_m.shape[1:]),
                                   jnp.broadcast_to(m_m, (B,) + m_m.shape[1:]),
                                   row0=0, n_blocks=B, seqs=1, length=CHUNK, n_chunks=S // CHUNK)
    m0_s = jnp.broadcast_to(state_mlstm_m[0].astype(F32)[:, :, None], (DB, HEADS, GATE_LANES))
    mix_s, c_s, n_s_, m_s = _mlstm(proj_a, gcol_a, gate_rows(gcol_a, 0, DB // SAMPLE_SEQS, SAMPLE_SEQS, DS),
                                   head_g, state_mlstm_c[0].astype(F32), state_mlstm_n[0].astype(F32), m0_s,
                                   row0=0, n_blocks=DB // SAMPLE_SEQS, seqs=SAMPLE_SEQS, length=DS, n_chunks=1)
    mix_a = jnp.concatenate([mix_s, mix_m, pad_a], axis=0)
    h_p, h_a = matmul_norm_res(mix_p, mix_a, mlstm_w_out[:1].astype(F32), gain(norm_mix_post[0]))
    h_p, h_a = ffn(0)

    w_pool_in, g_mix = pool_w_in[:1].astype(F32), gain(norm_mix_pre[1])
    u_p = _norm_matmul(h_p, g_mix, w_pool_in, tm=TM_PROJ, tn=TN_PROJ, out_dtype=F32)
    u_a = _norm_matmul(h_a, g_mix, w_pool_in, tm=TM_AUX, tn=TN_PROJ, out_dtype=F32)
    w_group = pool_w_group[:1].astype(F32)
    p_scale = pool_scale[0].reshape(1, D).astype(F32)
    pmix_m = _pool_rows(u_a, jnp.zeros((N_META, D), F32), w_group, p_scale, row0=row_m, n_tiles=1,
                        tr=N_META, tiles_per_seq=1, pos0=0, first_block=0)
    pmix_p = _pool_rows(u_p, u_a, w_group, p_scale, row0=0, n_tiles=n_p // POOL_TR, tr=POOL_TR,
                        tiles_per_seq=S // POOL_TR, pos0=N_META, first_block=row_m // N_META)
    ext_s = jnp.concatenate([jnp.zeros((DB, 1, D), F32), state_pool[0].astype(F32),
                             u_a[:n_s].reshape(DB, DS, D)], axis=1)
    pmix_s = _pool_seqs(ext_s, w_group, p_scale, seqs_per_block=POOL_SEQS, pos0=PAST_LEN)
    pmix_a = jnp.concatenate([pmix_s, pmix_m, pad_a], axis=0)
    h_p, h_a = matmul_norm_res(pmix_p, pmix_a, pool_w_out[:1].astype(F32), gain(norm_mix_post[1]))
    h_p, h_a = ffn(1)

    y_prompt = h_p.reshape(B, S, D)
    y_sample = h_a[:n_s].reshape(DB, DS, D)
    dt_c, dt_n, dt_m, dt_pool = state_mlstm_c.dtype, state_mlstm_n.dtype, state_mlstm_m.dtype, state_pool.dtype
    u_prompt = u_p.reshape(B, S, D)
    return (y_prompt, y_sample,
            c_p[None].astype(dt_c), n_p_[None].astype(dt_n), m_p[None, :, :, 0].astype(dt_m),
            u_prompt[None, :, S - POOL_BUF:].astype(dt_pool),
            c_s[None].astype(dt_c), n_s_[None].astype(dt_n), m_s[None, :, :, 0].astype(dt_m),
            ext_s[None, :, -POOL_BUF:].astype(dt_pool))
```

```python
import functools

import jax
import jax.numpy as jnp
from jax import lax
from jax.experimental import pallas as pl
from jax.experimental.pallas import tpu as pltpu

D = 2048
N_META = 16
HEADS = 4
DQK = 256
DV = 512
D_FF = 4 * D
POOL_WINDOWS = (2, 4, 8, 16)
POOL_GROUP_DIM = D // len(POOL_WINDOWS)
POOL_BUF = 15
PAST_LEN = 16384
EPS = 1e-6
K_SCALE = DQK ** -0.5
PROJ_MAIN = 2 * HEADS * DQK + 2 * HEADS * DV
GATE_LANES = 128

VMEM_LIMIT = 56 * 1024 * 1024

F32 = jnp.float32
BF16 = jnp.bfloat16


def _params(semantics):
    return pltpu.CompilerParams(dimension_semantics=semantics, vmem_limit_bytes=VMEM_LIMIT)


def _rmsnorm(x, g):
    return x * lax.rsqrt(jnp.mean(x * x, axis=-1, keepdims=True) + EPS) * g


def _dot(a, b):
    return jnp.dot(a, b, preferred_element_type=F32)


def _norm_matmul_kernel(x_ref, g_ref, w_ref, o_ref, xn_ref):
    @pl.when(pl.program_id(1) == 0)
    def _():
        xn_ref[...] = _rmsnorm(x_ref[...], g_ref[...]).astype(BF16)

    o_ref[...] = _dot(xn_ref[...], w_ref[...].astype(BF16)).astype(o_ref.dtype)


def _dot_t(a, b):
    return lax.dot_general(a, b, (((1,), (1,)), ((), ())), preferred_element_type=F32)


def _norm_matmul_gates_kernel(x_ref, g_ref, wt_ref, wgt_ref, bg_ref, o_ref, gate_ref, xn_ref):
    @pl.when(pl.program_id(1) == 0)
    def _():
        xn = _rmsnorm(x_ref[...], g_ref[...]).astype(BF16)
        xn_ref[...] = xn
        z = _dot_t(xn, wgt_ref[...]) + bg_ref[...]
        lane = lax.broadcasted_iota(jnp.int32, z.shape, 1)
        log_sig = jnp.minimum(z, 0.0) - jnp.log1p(jnp.exp(-jnp.abs(z)))
        gate_ref[...] = jnp.where(lane >= HEADS, log_sig, z)

    o_ref[...] = _dot_t(xn_ref[...], wt_ref[...].astype(BF16)).astype(o_ref.dtype)


def _norm_matmul(x, g, w, *, tm, tn, out_dtype):
    rows, n = x.shape[0], w.shape[2]
    return pl.pallas_call(
        _norm_matmul_kernel,
        out_shape=jax.ShapeDtypeStruct((rows, n), out_dtype),
        grid=(rows // tm, n // tn),
        in_specs=[pl.BlockSpec((tm, D), lambda i, j: (i, 0)),
                  pl.BlockSpec((1, D), lambda i, j: (0, 0)),
                  pl.BlockSpec((None, D, tn), lambda i, j: (0, 0, j))],
        out_specs=pl.BlockSpec((tm, tn), lambda i, j: (i, j)),
        scratch_shapes=[pltpu.VMEM((tm, D), BF16)],
        compiler_params=_params(("parallel", "arbitrary")),
        name="norm_matmul",
    )(x, g, w)


def _norm_matmul_gates(x, g, w_t, wg_t, bg, *, n, tm, tn):
    rows = x.shape[0]
    return pl.pallas_call(
        _norm_matmul_gates_kernel,
        out_shape=(jax.ShapeDtypeStruct((rows, n), BF16),
                   jax.ShapeDtypeStruct((rows, GATE_LANES), F32)),
        grid=(rows // tm, n // tn),
        in_specs=[pl.BlockSpec((tm, D), lambda i, j: (i, 0)),
                  pl.BlockSpec((1, D), lambda i, j: (0, 0)),
                  pl.BlockSpec((None, tn, D), lambda i, j: (0, j, 0)),
                  pl.BlockSpec((GATE_LANES, D), lambda i, j: (0, 0)),
                  pl.BlockSpec((1, GATE_LANES), lambda i, j: (0, 0))],
        out_specs=(pl.BlockSpec((tm, tn), lambda i, j: (i, j)),
                   pl.BlockSpec((tm, GATE_LANES), lambda i, j: (i, 0))),
        scratch_shapes=[pltpu.VMEM((tm, D), BF16)],
        compiler_params=_params(("parallel", "arbitrary")),
        name="norm_matmul_gates",
    )(x, g, w_t, wg_t, bg)


def _matmul_norm_res_kernel(a_ref, w_ref, g_ref, h_ref, o_ref, wb_ref):
    @pl.when(pl.program_id(0) == 0)
    def _():
        wb_ref[...] = w_ref[...].astype(BF16)

    y = _dot(a_ref[...], wb_ref[...])
    o_ref[...] = h_ref[...] + _rmsnorm(y, g_ref[...])


def _matmul_norm_res(a, w, g, h, *, tm):
    rows = a.shape[0]
    return pl.pallas_call(
        _matmul_norm_res_kernel,
        out_shape=jax.ShapeDtypeStruct((rows, D), F32),
        grid=(rows // tm,),
        in_specs=[pl.BlockSpec((tm, D), lambda i: (i, 0)),
                  pl.BlockSpec((None, D, D), lambda i: (0, 0, 0), pipeline_mode=pl.Buffered(1)),
                  pl.BlockSpec((1, D), lambda i: (0, 0)),
                  pl.BlockSpec((tm, D), lambda i: (i, 0))],
        out_specs=pl.BlockSpec((tm, D), lambda i: (i, 0)),
        scratch_shapes=[pltpu.VMEM((D, D), BF16)],
        compiler_params=_params(("arbitrary",)),
        name="matmul_norm_res",
    )(a, w, g, h)


def _ffn_begin(h_ref, gpre_ref, o_ref, xn_ref):
    @pl.when(pl.program_id(1) == 0)
    def _():
        xn_ref[...] = _rmsnorm(h_ref[...], gpre_ref[...]).astype(BF16)
        o_ref[...] = jnp.zeros_like(o_ref)


def _ffn_end(h_ref, gpost_ref, o_ref):
    @pl.when(pl.program_id(1) == pl.num_programs(1) - 1)
    def _():
        o_ref[...] = h_ref[...] + _rmsnorm(o_ref[...], gpost_ref[...])


def _ffn_kernel(h_ref, gpre_ref, wup_ref, wdown_ref, gpost_ref, o_ref, xn_ref):
    _ffn_begin(h_ref, gpre_ref, o_ref, xn_ref)
    a = jnp.maximum(_dot(xn_ref[...], wup_ref[...]), 0.0)
    o_ref[...] += _dot((a * a).astype(BF16), wdown_ref[...])
    _ffn_end(h_ref, gpost_ref, o_ref)


def _ffn_cast_kernel(h_ref, gpre_ref, wup_ref, wdown_ref, gpost_ref, o_ref, wupb_ref, wdownb_ref, xn_ref,
                     wup_scr, wdown_scr):
    _ffn_begin(h_ref, gpre_ref, o_ref, xn_ref)
    wup_scr[...] = wup_ref[...].astype(BF16)
    wdown_scr[...] = wdown_ref[...].astype(BF16)
    a = jnp.maximum(_dot(xn_ref[...], wup_scr[...]), 0.0)
    o_ref[...] += _dot((a * a).astype(BF16), wdown_scr[...])

    @pl.when(pl.program_id(0) == 0)
    def _():
        wupb_ref[...] = wup_scr[...]
        wdownb_ref[...] = wdown_scr[...]

    _ffn_end(h_ref, gpost_ref, o_ref)


def _ffn(h, g_pre, w_up, w_down, g_post, *, tm, tf):
    rows = h.shape[0]
    return pl.pallas_call(
        _ffn_kernel,
        out_shape=jax.ShapeDtypeStruct((rows, D), F32),
        grid=(rows // tm, D_FF // tf),
        in_specs=[pl.BlockSpec((tm, D), lambda i, f: (i, 0)),
                  pl.BlockSpec((1, D), lambda i, f: (0, 0)),
                  pl.BlockSpec((D, tf), lambda i, f: (0, f)),
                  pl.BlockSpec((tf, D), lambda i, f: (f, 0)),
                  pl.BlockSpec((1, D), lambda i, f: (0, 0))],
        out_specs=pl.BlockSpec((tm, D), lambda i, f: (i, 0)),
        scratch_shapes=[pltpu.VMEM((tm, D), BF16)],
        compiler_params=_params(("parallel", "arbitrary")),
        name="ffn",
    )(h, g_pre, w_up, w_down, g_post)


def _ffn_cast(h, g_pre, w_up, w_down, g_post, *, layer, tm, tf):
    rows = h.shape[0]
    n_f = D_FF // tf

    def copy_chunk(i, f):
        return jnp.where(i == 0, f, n_f - 1)

    return pl.pallas_call(
        _ffn_cast_kernel,
        out_shape=(jax.ShapeDtypeStruct((rows, D), F32),
                   jax.ShapeDtypeStruct((D, D_FF), BF16),
                   jax.ShapeDtypeStruct((D_FF, D), BF16)),
        grid=(rows // tm, n_f),
        in_specs=[pl.BlockSpec((tm, D), lambda i, f: (i, 0)),
                  pl.BlockSpec((1, D), lambda i, f: (0, 0)),
                  pl.BlockSpec((None, D, tf), lambda i, f: (layer, 0, f)),
                  pl.BlockSpec((None, tf, D), lambda i, f: (layer, f, 0)),
                  pl.BlockSpec((1, D), lambda i, f: (0, 0))],
        out_specs=(pl.BlockSpec((tm, D), lambda i, f: (i, 0)),
                   pl.BlockSpec((D, tf), lambda i, f: (0, copy_chunk(i, f))),
                   pl.BlockSpec((tf, D), lambda i, f: (copy_chunk(i, f), 0))),
        scratch_shapes=[pltpu.VMEM((tm, D), BF16), pltpu.VMEM((D, tf), BF16), pltpu.VMEM((tf, D), BF16)],
        compiler_params=_params(("arbitrary", "arbitrary")),
        name="ffn_cast",
    )(h, g_pre, w_up, w_down, g_post)


def _mlstm_kernel(qk_ref, v_ref, o_ref, gcol_ref, grow_ref, hg_ref, c0_ref, n0_ref, m0_ref,
                  out_ref, c_ref, n_ref, m_ref, *, seqs, length, single_chunk):
    if single_chunk:
        c_in, n_in, m_in = c0_ref, n0_ref, m0_ref
    else:
        c_in, n_in, m_in = c_ref, n_ref, m_ref

        @pl.when(pl.program_id(1) == 0)
        def _():
            c_ref[...] = c0_ref[...]
            n_ref[...] = n0_ref[...]
            m_ref[...] = m0_ref[...]

    pairs = [(s, hd) for s in range(seqs) for hd in range(HEADS)]
    L = length
    t_idx = lax.broadcasted_iota(jnp.int32, (L, L), 0)
    s_idx = lax.broadcasted_iota(jnp.int32, (L, L), 1)
    causal = s_idx <= t_idx

    def rows_of(s):
        return slice(s * L, (s + 1) * L)

    def q_of(s, hd):
        return qk_ref[rows_of(s), hd * DQK:(hd + 1) * DQK]

    def k_of(s, hd):
        return qk_ref[rows_of(s), (HEADS + hd) * DQK:(HEADS + hd + 1) * DQK]

    def v_of(s, hd):
        return v_ref[rows_of(s), hd * DV:(hd + 1) * DV]

    qk = [_dot_t(q_of(s, hd), k_of(s, hd)) for s, hd in pairs]
    qc = [_dot(q_of(s, hd), c_in[s, hd].astype(BF16)) for s, hd in pairs]

    gate_vals = []
    for s, hd in pairs:
        gcol = gcol_ref[rows_of(s), :]
        grow = grow_ref[0, s]
        li_row, lf_row = grow[hd:hd + 1, :], grow[HEADS + hd:HEADS + hd + 1, :]
        li_col, lf_col = gcol[:, hd:hd + 1], gcol[:, HEADS + hd:HEADS + hd + 1]
        m = m_in[s, hd:hd + 1, 0:1]
        b_col = jnp.sum(jnp.where(causal, lf_row, 0.0), axis=1, keepdims=True)
        b_row = jnp.sum(jnp.where(t_idx <= s_idx, lf_col, 0.0), axis=0, keepdims=True)
        b_tot = b_col[L - 1:L, :]
        dmat = jnp.where(causal, b_col - b_row + li_row, -jnp.inf)
        m_inter = b_col + m
        m_t = jnp.maximum(m_inter, jnp.max(dmat, axis=1, keepdims=True))
        p = jnp.exp(dmat - m_t)
        w_inter = jnp.exp(m_inter - m_t)
        m_new = m_t[L - 1:L, :]
        decay = jnp.exp(b_tot - b_col + li_col - m_new)
        carry = jnp.exp(b_tot + m - m_new)
        gate_vals.append((p, w_inter, m_t, m_new, decay, carry))

    scores = [qk[i] * (gate_vals[i][0] * K_SCALE) for i in range(len(pairs))]
    kd = [k_of(s, hd).astype(F32) * (gate_vals[i][4] * K_SCALE) for i, (s, hd) in enumerate(pairs)]
    sv = [_dot(scores[i].astype(BF16), v_of(s, hd)) for i, (s, hd) in enumerate(pairs)]
    kv = [lax.dot_general(kd[i].astype(BF16), v_of(s, hd), (((0,), (0,)), ((), ())),
                          preferred_element_type=F32) for i, (s, hd) in enumerate(pairs)]

    for i, (s, hd) in enumerate(pairs):
        p, w_inter, m_t, m_new, decay, carry = gate_vals[i]
        n_row = n_in[s, hd:hd + 1, :]
        c_new = carry * c_in[s, hd] + kv[i]
        n_new = carry * n_row + jnp.sum(kd[i], axis=0, keepdims=True)
        num = sv[i] + qc[i] * w_inter
        qn = jnp.sum(q_of(s, hd).astype(F32) * n_row, axis=1, keepdims=True)
        den = jnp.sum(scores[i], axis=1, keepdims=True) + w_inter * qn
        h = num / jnp.maximum(jnp.abs(den), jnp.exp(-m_t))
        c_ref[s, hd] = c_new
        n_ref[s, hd:hd + 1, :] = n_new
        m_ref[s, hd:hd + 1, :] = jnp.broadcast_to(m_new, (1, GATE_LANES))
        hn = _rmsnorm(h, hg_ref[:, hd * DV:(hd + 1) * DV])
        gate = jax.nn.sigmoid(o_ref[rows_of(s), hd * DV:(hd + 1) * DV].astype(F32))
        out_ref[rows_of(s), hd * DV:(hd + 1) * DV] = (hn * gate).astype(out_ref.dtype)


def _mlstm(proj, gcol, grow, head_g, c0, n0, m0, *, row0, n_blocks, seqs, length, n_chunks):
    assert seqs == 1 or n_chunks == 1
    blk = seqs * length
    assert row0 % blk == 0
    b0 = row0 // blk
    n_seq = n_blocks * seqs

    def rows_map(col):
        return lambda b, c: (b0 + b * n_chunks + c, col)

    kernel = functools.partial(_mlstm_kernel, seqs=seqs, length=length, single_chunk=n_chunks == 1)
    return pl.pallas_call(
        kernel,
        out_shape=(jax.ShapeDtypeStruct((n_blocks * n_chunks * blk, HEADS * DV), BF16),
                   jax.ShapeDtypeStruct((n_seq, HEADS, DQK, DV), F32),
                   jax.ShapeDtypeStruct((n_seq, HEADS, DQK), F32),
                   jax.ShapeDtypeStruct((n_seq, HEADS, GATE_LANES), F32)),
        grid=(n_blocks, n_chunks),
        in_specs=[pl.BlockSpec((blk, 2 * HEADS * DQK), rows_map(0)),
                  pl.BlockSpec((blk, HEADS * DV), rows_map(1)),
                  pl.BlockSpec((blk, HEADS * DV), rows_map(2)),
                  pl.BlockSpec((blk, GATE_LANES), rows_map(0)),
                  pl.BlockSpec((1, seqs, 2 * HEADS, length), lambda b, c: (b * n_chunks + c, 0, 0, 0)),
                  pl.BlockSpec((1, HEADS * DV), lambda b, c: (0, 0)),
                  pl.BlockSpec((seqs, HEADS, DQK, DV), lambda b, c: (b, 0, 0, 0)),
                  pl.BlockSpec((seqs, HEADS, DQK), lambda b, c: (b, 0, 0)),
                  pl.BlockSpec((seqs, HEADS, GATE_LANES), lambda b, c: (b, 0, 0))],
        out_specs=(pl.BlockSpec((blk, HEADS * DV), lambda b, c: (b * n_chunks + c, 0)),
                   pl.BlockSpec((seqs, HEADS, DQK, DV), lambda b, c: (b, 0, 0, 0)),
                   pl.BlockSpec((seqs, HEADS, DQK), lambda b, c: (b, 0, 0)),
                   pl.BlockSpec((seqs, HEADS, GATE_LANES), lambda b, c: (b, 0, 0))),
        compiler_params=_params(("parallel", "arbitrary")),
        name=f"mlstm_l{length}",
    )(proj, proj, proj, gcol, grow, head_g, c0, n0, m0)


def _pool_mix(ext, first, n_rows, pos, wg_ref, scale_ref):
    outs = []
    for g, w in enumerate(POOL_WINDOWS):
        cols = slice(g * POOL_GROUP_DIM, (g + 1) * POOL_GROUP_DIM)
        e = ext[:, cols]
        total, span = e, 1
        while span < w:
            total = total + pltpu.roll(total, span, 0)
            span *= 2
        cnt = jnp.minimum(float(w), pos + 1.0)
        pooled = total[first:first + n_rows] / cnt - e[first:first + n_rows]
        outs.append(_dot(pooled.astype(BF16), wg_ref[g].astype(BF16)))
    mixed = jnp.concatenate(outs, axis=-1) * scale_ref[...]
    return mixed.astype(BF16)


def _pool_rows_kernel(u_ref, prev_ref, first_ref, wg_ref, scale_ref, o_ref, *, tiles_per_seq, pos0):
    tr = u_ref.shape[0]
    tile = pl.program_id(0) % tiles_per_seq
    halo = jnp.where(tile == 0, first_ref[...], prev_ref[...])
    ext = jnp.concatenate([halo, u_ref[...]], axis=0)
    pos = (lax.broadcasted_iota(jnp.int32, (tr, 1), 0) + (tile * tr + pos0)).astype(F32)
    o_ref[...] = _pool_mix(ext, N_META, tr, pos, wg_ref, scale_ref)


def _pool_rows(u, first_src, w_group, scale, *, row0, n_tiles, tr, tiles_per_seq, pos0, first_block):
    hb = tr // N_META
    t0 = row0 // tr
    kernel = functools.partial(_pool_rows_kernel, tiles_per_seq=tiles_per_seq, pos0=pos0)
    return pl.pallas_call(
        kernel,
        out_shape=jax.ShapeDtypeStruct((n_tiles * tr, D), BF16),
        grid=(n_tiles,),
        in_specs=[pl.BlockSpec((tr, D), lambda i: (t0 + i, 0)),
                  pl.BlockSpec((N_META, D), lambda i: (jnp.maximum((t0 + i) * hb - 1, 0), 0)),
                  pl.BlockSpec((N_META, D), lambda i: (first_block, 0)),
                  pl.BlockSpec((None, len(POOL_WINDOWS), POOL_GROUP_DIM, POOL_GROUP_DIM),
                               lambda i: (0, 0, 0, 0)),
                  pl.BlockSpec((1, D), lambda i: (0, 0))],
        out_specs=pl.BlockSpec((tr, D), lambda i: (i, 0)),
        compiler_params=_params(("parallel",)),
        name=f"pool_rows_{tr}",
    )(u, u, first_src, w_group, scale)


def _pool_seqs_kernel(ext_ref, wg_ref, scale_ref, o_ref, *, pos0):
    g, r, _ = ext_ref.shape
    n_new = r - 1 - POOL_BUF
    ext = ext_ref[...].reshape(g * r, D)
    outs = []
    pos = (lax.broadcasted_iota(jnp.int32, (g, n_new, 1), 1) + pos0).astype(F32)
    for gi, w in enumerate(POOL_WINDOWS):
        cols = slice(gi * POOL_GROUP_DIM, (gi + 1) * POOL_GROUP_DIM)
        e = ext[:, cols]
        total, span = e, 1
        while span < w:
            total = total + pltpu.roll(total, span, 0)
            span *= 2
        cnt = jnp.minimum(float(w), pos + 1.0)
        tot3 = total.reshape(g, r, POOL_GROUP_DIM)[:, 1 + POOL_BUF:, :]
        e3 = e.reshape(g, r, POOL_GROUP_DIM)[:, 1 + POOL_BUF:, :]
        pooled = (tot3 / cnt - e3).reshape(g * n_new, POOL_GROUP_DIM)
        outs.append(_dot(pooled.astype(BF16), wg_ref[gi].astype(BF16)))
    mixed = jnp.concatenate(outs, axis=-1) * scale_ref[...]
    o_ref[...] = mixed.astype(BF16)


def _pool_seqs(ext3, w_group, scale, *, seqs_per_block, pos0):
    n_seq, r, _ = ext3.shape
    n_new = r - 1 - POOL_BUF
    kernel = functools.partial(_pool_seqs_kernel, pos0=pos0)
    return pl.pallas_call(
        kernel,
        out_shape=jax.ShapeDtypeStruct((n_seq * n_new, D), BF16),
        grid=(n_seq // seqs_per_block,),
        in_specs=[pl.BlockSpec((seqs_per_block, r, D), lambda i: (i, 0, 0)),
                  pl.BlockSpec((None, len(POOL_WINDOWS), POOL_GROUP_DIM, POOL_GROUP_DIM),
                               lambda i: (0, 0, 0, 0)),
                  pl.BlockSpec((1, D), lambda i: (0, 0))],
        out_specs=pl.BlockSpec((seqs_per_block * n_new, D), lambda i: (i, 0)),
        compiler_params=_params(("parallel",)),
        name="pool_seqs",
    )(ext3, w_group, scale)


TM_PROJ = 1024
TN_PROJ = 1024
TM = 512
TM_AUX = 544
TM_FFN = 512
TF = 1024
TF_AUX = 512
CHUNK = 256
SAMPLE_SEQS = 4
POOL_TR = 512
POOL_SEQS = 16


def kernel(x_prompt, x_sample, state_mlstm_c, state_mlstm_n, state_mlstm_m, state_pool, meta_tokens,
           norm_mix_pre, norm_mix_post, norm_ffn_pre, norm_ffn_post, mlstm_w_in, mlstm_b_i, mlstm_b_f,
           mlstm_head_norm, mlstm_w_out, pool_w_in, pool_w_group, pool_scale, pool_w_out,
           ffn_w_up, ffn_w_down):
    B, S, _ = x_prompt.shape
    DB, DS, _ = x_sample.shape
    n_p, n_s = B * S, DB * DS
    rows_a = 2 * TM_AUX
    row_m = n_s
    assert n_s + N_META <= rows_a and n_p % TM_PROJ == 0 and n_p % TM == 0
    assert S % CHUNK == 0 and S % POOL_TR == 0 and row_m % N_META == 0

    h_p = x_prompt.reshape(n_p, D)
    h_a = jnp.concatenate([x_sample.reshape(n_s, D), meta_tokens.astype(F32),
                           jnp.zeros((rows_a - n_s - N_META, D), F32)], axis=0)
    pad_a = jnp.zeros((rows_a - n_s - N_META, D), BF16)

    def gain(a):
        return a.reshape(1, D).astype(F32)


    def matmul_norm_res(a_p, a_a, w, g):
        return _matmul_norm_res(a_p, w, g, h_p, tm=TM), _matmul_norm_res(a_a, w, g, h_a, tm=TM_AUX)

    def ffn(layer):
        g_pre, g_post = gain(norm_ffn_pre[layer]), gain(norm_ffn_post[layer])
        out_a, w_up, w_down = _ffn_cast(h_a, g_pre, ffn_w_up.astype(F32), ffn_w_down.astype(F32), g_post,
                                        layer=layer, tm=TM_AUX, tf=TF_AUX)
        return _ffn(h_p, g_pre, w_up, w_down, g_post, tm=TM_FFN, tf=TF), out_a

    w_in_t = jnp.swapaxes(mlstm_w_in[:1].astype(F32), 1, 2)
    w_gate_t = jnp.pad(w_in_t[0, PROJ_MAIN:], ((0, GATE_LANES - 2 * HEADS), (0, 0))).astype(BF16)
    b_gate = jnp.pad(jnp.concatenate([mlstm_b_i[0], mlstm_b_f[0]]).astype(F32),
                     (0, GATE_LANES - 2 * HEADS)).reshape(1, GATE_LANES)
    g_mix = gain(norm_mix_pre[0])
    proj_p, gcol_p = _norm_matmul_gates(h_p, g_mix, w_in_t, w_gate_t, b_gate, n=PROJ_MAIN, tm=TM_PROJ, tn=TN_PROJ)
    proj_a, gcol_a = _norm_matmul_gates(h_a, g_mix, w_in_t, w_gate_t, b_gate, n=PROJ_MAIN, tm=TM_AUX, tn=TN_PROJ)

    def gate_rows(gcol, r0, n_blk, seqs, length):
        g8 = gcol[r0:r0 + n_blk * seqs * length, :2 * HEADS]
        return g8.reshape(n_blk, seqs, length, 2 * HEADS).transpose(0, 1, 3, 2)

    head_g = mlstm_head_norm[0].reshape(1, HEADS * DV).astype(F32)
    zc = jnp.zeros((1, HEADS, DQK, DV), F32)
    zn = jnp.zeros((1, HEADS, DQK), F32)
    zm = jnp.zeros((1, HEADS, GATE_LANES), F32)
    mix_m, c_m, n_m, m_m = _mlstm(proj_a, gcol_a, gate_rows(gcol_a, row_m, 1, 1, N_META), head_g, zc, zn, zm,
                                  row0=row_m, n_blocks=1, seqs=1, length=N_META, n_chunks=1)
    mix_p, c_p, n_p_, m_p = _mlstm(proj_p, gcol_p, gate_rows(gcol_p, 0, B * (S // CHUNK), 1, CHUNK), head_g,
                                   jnp.broadcast_to(c_m, (B,) + c_m.shape[1:]),
                                   jnp.broadcast_to(n_m, (B,) + n_m.shape[1:]),
                                   jnp.broadcast_to(m_m, (B,) + m_m.shape[1:]),
                                   row0=0, n_blocks=B, seqs=1, length=CHUNK, n_chunks=S // CHUNK)
    m0_s = jnp.broadcast_to(state_mlstm_m[0].astype(F32)[:, :, None], (DB, HEADS, GATE_LANES))
    mix_s, c_s, n_s_, m_s = _mlstm(proj_a, gcol_a, gate_rows(gcol_a, 0, DB // SAMPLE_SEQS, SAMPLE_SEQS, DS),
                                   head_g, state_mlstm_c[0].astype(F32), state_mlstm_n[0].astype(F32), m0_s,
                                   row0=0, n_blocks=DB // SAMPLE_SEQS, seqs=SAMPLE_SEQS, length=DS, n_chunks=1)
    mix_a = jnp.concatenate([mix_s, mix_m, pad_a], axis=0)
    h_p, h_a = matmul_norm_res(mix_p, mix_a, mlstm_w_out[:1].astype(F32), gain(norm_mix_post[0]))
    h_p, h_a = ffn(0)

    w_pool_in, g_mix = pool_w_in[:1].astype(F32), gain(norm_mix_pre[1])
    u_p = _norm_matmul(h_p, g_mix, w_pool_in, tm=TM_PROJ, tn=TN_PROJ, out_dtype=F32)
    u_a = _norm_matmul(h_a, g_mix, w_pool_in, tm=TM_AUX, tn=TN_PROJ, out_dtype=F32)
    w_group = pool_w_group[:1].astype(F32)
    p_scale = pool_scale[0].reshape(1, D).astype(F32)
    pmix_m = _pool_rows(u_a, jnp.zeros((N_META, D), F32), w_group, p_scale, row0=row_m, n_tiles=1,
                        tr=N_META, tiles_per_seq=1, pos0=0, first_block=0)
    pmix_p = _pool_rows(u_p, u_a, w_group, p_scale, row0=0, n_tiles=n_p // POOL_TR, tr=POOL_TR,
                        tiles_per_seq=S // POOL_TR, pos0=N_META, first_block=row_m // N_META)
    ext_s = jnp.concatenate([jnp.zeros((DB, 1, D), F32), state_pool[0].astype(F32),
                             u_a[:n_s].reshape(DB, DS, D)], axis=1)
    pmix_s = _pool_seqs(ext_s, w_group, p_scale, seqs_per_block=POOL_SEQS, pos0=PAST_LEN)
    pmix_a = jnp.concatenate([pmix_s, pmix_m, pad_a], axis=0)
    h_p, h_a = matmul_norm_res(pmix_p, pmix_a, pool_w_out[:1].astype(F32), gain(norm_mix_post[1]))
    h_p, h_a = ffn(1)

    y_prompt = h_p.reshape(B, S, D)
    y_sample = h_a[:n_s].reshape(DB, DS, D)
    dt_c, dt_n, dt_m, dt_pool = state_mlstm_c.dtype, state_mlstm_n.dtype, state_mlstm_m.dtype, state_pool.dtype
    u_prompt = u_p.reshape(B, S, D)
    return (y_prompt, y_sample,
            c_p[None].astype(dt_c), n_p_[None].astype(dt_n), m_p[None, :, :, 0].astype(dt_m),
            u_prompt[None, :, S - POOL_BUF:].astype(dt_pool),
            c_s[None].astype(dt_c), n_s_[None].astype(dt_n), m_s[None, :, :, 0].astype(dt_m),
            ext_s[None, :, -POOL_BUF:].astype(dt_pool))
```

```python
import functools

import jax
import jax.numpy as jnp
from jax import lax
from jax.experimental import pallas as pl
from jax.experimental.pallas import tpu as pltpu

D = 2048
N_META = 16
HEADS = 4
DQK = 256
DV = 512
D_FF = 4 * D
POOL_WINDOWS = (2, 4, 8, 16)
POOL_GROUP_DIM = D // len(POOL_WINDOWS)
POOL_BUF = 15
PAST_LEN = 16384
EPS = 1e-6
K_SCALE = DQK ** -0.5
PROJ_MAIN = 2 * HEADS * DQK + 2 * HEADS * DV
GATE_LANES = 128

VMEM_LIMIT = 56 * 1024 * 1024

F32 = jnp.float32
BF16 = jnp.bfloat16


def _params(semantics):
    return pltpu.CompilerParams(dimension_semantics=semantics, vmem_limit_bytes=VMEM_LIMIT)


def _rmsnorm(x, g):
    return x * lax.rsqrt(jnp.mean(x * x, axis=-1, keepdims=True) + EPS) * g


def _dot(a, b):
    return jnp.dot(a, b, preferred_element_type=F32)


SUB_ROWS = 256


def _norm_matmul_kernel(x_ref, g_ref, w_ref, o_ref, wb_ref):
    @pl.when(pl.program_id(0) == 0)
    def _():
        wb_ref[...] = w_ref[...].astype(BF16)

    for r in range(0, x_ref.shape[0], SUB_ROWS):
        rows = slice(r, min(r + SUB_ROWS, x_ref.shape[0]))
        xn = _rmsnorm(x_ref[rows, :], g_ref[...]).astype(BF16)
        o_ref[rows, :] = _dot(xn, wb_ref[...]).astype(o_ref.dtype)


def _dot_t(a, b):
    return lax.dot_general(a, b, (((1,), (1,)), ((), ())), preferred_element_type=F32)


def _norm_matmul_gates_kernel(x_ref, g_ref, wt_ref, wgt_ref, bg_ref, o_ref, gate_ref, xn_ref):
    @pl.when(pl.program_id(1) == 0)
    def _():
        xn = _rmsnorm(x_ref[...], g_ref[...]).astype(BF16)
        xn_ref[...] = xn
        z = _dot_t(xn, wgt_ref[...]) + bg_ref[...]
        lane = lax.broadcasted_iota(jnp.int32, z.shape, 1)
        log_sig = jnp.minimum(z, 0.0) - jnp.log1p(jnp.exp(-jnp.abs(z)))
        gate_ref[...] = jnp.where(lane >= HEADS, log_sig, z)

    o_ref[...] = _dot_t(xn_ref[...], wt_ref[...].astype(BF16)).astype(o_ref.dtype)


def _norm_matmul(x, g, w, *, tm, out_dtype):
    rows = x.shape[0]
    return pl.pallas_call(
        _norm_matmul_kernel,
        out_shape=jax.ShapeDtypeStruct((rows, D), out_dtype),
        grid=(rows // tm,),
        in_specs=[pl.BlockSpec((tm, D), lambda i: (i, 0)),
                  pl.BlockSpec((1, D), lambda i: (0, 0)),
                  pl.BlockSpec((None, D, D), lambda i: (0, 0, 0), pipeline_mode=pl.Buffered(1))],
        out_specs=pl.BlockSpec((tm, D), lambda i: (i, 0)),
        scratch_shapes=[pltpu.VMEM((D, D), BF16)],
        compiler_params=_params(("arbitrary",)),
        name="norm_matmul",
    )(x, g, w)


def _norm_matmul_gates(x, g, w_t, wg_t, bg, *, n, tm, tn):
    rows = x.shape[0]
    return pl.pallas_call(
        _norm_matmul_gates_kernel,
        out_shape=(jax.ShapeDtypeStruct((rows, n), BF16),
                   jax.ShapeDtypeStruct((rows, GATE_LANES), F32)),
        grid=(rows // tm, n // tn),
        in_specs=[pl.BlockSpec((tm, D), lambda i, j: (i, 0)),
                  pl.BlockSpec((1, D), lambda i, j: (0, 0)),
                  pl.BlockSpec((None, tn, D), lambda i, j: (0, j, 0)),
                  pl.BlockSpec((GATE_LANES, D), lambda i, j: (0, 0)),
                  pl.BlockSpec((1, GATE_LANES), lambda i, j: (0, 0))],
        out_specs=(pl.BlockSpec((tm, tn), lambda i, j: (i, j)),
                   pl.BlockSpec((tm, GATE_LANES), lambda i, j: (i, 0))),
        scratch_shapes=[pltpu.VMEM((tm, D), BF16)],
        compiler_params=_params(("parallel", "arbitrary")),
        name="norm_matmul_gates",
    )(x, g, w_t, wg_t, bg)


def _matmul_norm_res_kernel(a_ref, w_ref, g_ref, h_ref, o_ref, wb_ref):
    @pl.when(pl.program_id(0) == 0)
    def _():
        wb_ref[...] = w_ref[...].astype(BF16)

    for r in range(0, a_ref.shape[0], SUB_ROWS):
        rows = slice(r, min(r + SUB_ROWS, a_ref.shape[0]))
        y = _dot(a_ref[rows, :], wb_ref[...])
        o_ref[rows, :] = h_ref[rows, :] + _rmsnorm(y, g_ref[...])


def _matmul_norm_res(a, w, g, h, *, tm):
    rows = a.shape[0]
    return pl.pallas_call(
        _matmul_norm_res_kernel,
        out_shape=jax.ShapeDtypeStruct((rows, D), F32),
        grid=(rows // tm,),
        in_specs=[pl.BlockSpec((tm, D), lambda i: (i, 0)),
                  pl.BlockSpec((None, D, D), lambda i: (0, 0, 0), pipeline_mode=pl.Buffered(1)),
                  pl.BlockSpec((1, D), lambda i: (0, 0)),
                  pl.BlockSpec((tm, D), lambda i: (i, 0))],
        out_specs=pl.BlockSpec((tm, D), lambda i: (i, 0)),
        scratch_shapes=[pltpu.VMEM((D, D), BF16)],
        compiler_params=_params(("arbitrary",)),
        name="matmul_norm_res",
    )(a, w, g, h)


def _ffn_begin(h_ref, gpre_ref, o_ref, xn_ref):
    @pl.when(pl.program_id(1) == 0)
    def _():
        xn_ref[...] = _rmsnorm(h_ref[...], gpre_ref[...]).astype(BF16)
        o_ref[...] = jnp.zeros_like(o_ref)


def _ffn_end(h_ref, gpost_ref, o_ref):
    @pl.when(pl.program_id(1) == pl.num_programs(1) - 1)
    def _():
        o_ref[...] = h_ref[...] + _rmsnorm(o_ref[...], gpost_ref[...])


def _ffn_kernel(h_ref, gpre_ref, wup_ref, wdown_ref, gpost_ref, o_ref, xn_ref):
    _ffn_begin(h_ref, gpre_ref, o_ref, xn_ref)
    a = jnp.maximum(_dot(xn_ref[...], wup_ref[...]), 0.0)
    o_ref[...] += _dot((a * a).astype(BF16), wdown_ref[...])
    _ffn_end(h_ref, gpost_ref, o_ref)


def _ffn_cast_kernel(h_ref, gpre_ref, wup_ref, wdown_ref, gpost_ref, o_ref, wupb_ref, wdownb_ref, xn_ref,
                     wup_scr, wdown_scr):
    _ffn_begin(h_ref, gpre_ref, o_ref, xn_ref)
    wup_scr[...] = wup_ref[...].astype(BF16)
    wdown_scr[...] = wdown_ref[...].astype(BF16)
    a = jnp.maximum(_dot(xn_ref[...], wup_scr[...]), 0.0)
    o_ref[...] += _dot((a * a).astype(BF16), wdown_scr[...])

    @pl.when(pl.program_id(0) == 0)
    def _():
        wupb_ref[...] = wup_scr[...]
        wdownb_ref[...] = wdown_scr[...]

    _ffn_end(h_ref, gpost_ref, o_ref)


def _ffn(h, g_pre, w_up, w_down, g_post, *, tm, tf):
    rows = h.shape[0]
    return pl.pallas_call(
        _ffn_kernel,
        out_shape=jax.ShapeDtypeStruct((rows, D), F32),
        grid=(rows // tm, D_FF // tf),
        in_specs=[pl.BlockSpec((tm, D), lambda i, f: (i, 0)),
                  pl.BlockSpec((1, D), lambda i, f: (0, 0)),
                  pl.BlockSpec((D, tf), lambda i, f: (0, f)),
                  pl.BlockSpec((tf, D), lambda i, f: (f, 0)),
                  pl.BlockSpec((1, D), lambda i, f: (0, 0))],
        out_specs=pl.BlockSpec((tm, D), lambda i, f: (i, 0)),
        scratch_shapes=[pltpu.VMEM((tm, D), BF16)],
        compiler_params=_params(("parallel", "arbitrary")),
        name="ffn",
    )(h, g_pre, w_up, w_down, g_post)


def _ffn_cast(h, g_pre, w_up, w_down, g_post, *, layer, tm, tf):
    rows = h.shape[0]
    n_f = D_FF // tf

    def copy_chunk(i, f):
        return jnp.where(i == 0, f, n_f - 1)

    return pl.pallas_call(
        _ffn_cast_kernel,
        out_shape=(jax.ShapeDtypeStruct((rows, D), F32),
                   jax.ShapeDtypeStruct((D, D_FF), BF16),
                   jax.ShapeDtypeStruct((D_FF, D), BF16)),
        grid=(rows // tm, n_f),
        in_specs=[pl.BlockSpec((tm, D), lambda i, f: (i, 0)),
                  pl.BlockSpec((1, D), lambda i, f: (0, 0)),
                  pl.BlockSpec((None, D, tf), lambda i, f: (layer, 0, f)),
                  pl.BlockSpec((None, tf, D), lambda i, f: (layer, f, 0)),
                  pl.BlockSpec((1, D), lambda i, f: (0, 0))],
        out_specs=(pl.BlockSpec((tm, D), lambda i, f: (i, 0)),
                   pl.BlockSpec((D, tf), lambda i, f: (0, copy_chunk(i, f))),
                   pl.BlockSpec((tf, D), lambda i, f: (copy_chunk(i, f), 0))),
        scratch_shapes=[pltpu.VMEM((tm, D), BF16), pltpu.VMEM((D, tf), BF16), pltpu.VMEM((tf, D), BF16)],
        compiler_params=_params(("arbitrary", "arbitrary")),
        name="ffn_cast",
    )(h, g_pre, w_up, w_down, g_post)


def _mlstm_kernel(qk_ref, v_ref, o_ref, gcol_ref, grow_ref, hg_ref, c0_ref, n0_ref, m0_ref,
                  out_ref, c_ref, n_ref, m_ref, *, seqs, length, single_chunk):
    if single_chunk:
        c_in, n_in, m_in = c0_ref, n0_ref, m0_ref
    else:
        c_in, n_in, m_in = c_ref, n_ref, m_ref

        @pl.when(pl.program_id(1) == 0)
        def _():
            c_ref[...] = c0_ref[...]
            n_ref[...] = n0_ref[...]
            m_ref[...] = m0_ref[...]

    pairs = [(s, hd) for s in range(seqs) for hd in range(HEADS)]
    L = length
    t_idx = lax.broadcasted_iota(jnp.int32, (L, L), 0)
    s_idx = lax.broadcasted_iota(jnp.int32, (L, L), 1)
    causal = s_idx <= t_idx

    def rows_of(s):
        return slice(s * L, (s + 1) * L)

    def q_of(s, hd):
        return qk_ref[rows_of(s), hd * DQK:(hd + 1) * DQK]

    def k_of(s, hd):
        return qk_ref[rows_of(s), (HEADS + hd) * DQK:(HEADS + hd + 1) * DQK]

    def v_of(s, hd):
        return v_ref[rows_of(s), hd * DV:(hd + 1) * DV]

    qk = [_dot_t(q_of(s, hd), k_of(s, hd)) for s, hd in pairs]
    qc = [_dot(q_of(s, hd), c_in[s, hd].astype(BF16)) for s, hd in pairs]

    gate_vals = []
    for s, hd in pairs:
        gcol = gcol_ref[rows_of(s), :]
        grow = grow_ref[0, s]
        li_row, lf_row = grow[hd:hd + 1, :], grow[HEADS + hd:HEADS + hd + 1, :]
        li_col, lf_col = gcol[:, hd:hd + 1], gcol[:, HEADS + hd:HEADS + hd + 1]
        m = m_in[s, hd:hd + 1, 0:1]
        b_col = jnp.sum(jnp.where(causal, lf_row, 0.0), axis=1, keepdims=True)
        b_row = jnp.sum(jnp.where(t_idx <= s_idx, lf_col, 0.0), axis=0, keepdims=True)
        b_tot = b_col[L - 1:L, :]
        dmat = jnp.where(causal, b_col - b_row + li_row, -jnp.inf)
        m_inter = b_col + m
        m_t = jnp.maximum(m_inter, jnp.max(dmat, axis=1, keepdims=True))
        p = jnp.exp(dmat - m_t)
        w_inter = jnp.exp(m_inter - m_t)
        m_new = m_t[L - 1:L, :]
        decay = jnp.exp(b_tot - b_col + li_col - m_new)
        carry = jnp.exp(b_tot + m - m_new)
        gate_vals.append((p, w_inter, m_t, m_new, decay, carry))

    scores = [qk[i] * (gate_vals[i][0] * K_SCALE) for i in range(len(pairs))]
    kd = [k_of(s, hd).astype(F32) * (gate_vals[i][4] * K_SCALE) for i, (s, hd) in enumerate(pairs)]
    sv = [_dot(scores[i].astype(BF16), v_of(s, hd)) for i, (s, hd) in enumerate(pairs)]
    kv = [lax.dot_general(kd[i].astype(BF16), v_of(s, hd), (((0,), (0,)), ((), ())),
                          preferred_element_type=F32) for i, (s, hd) in enumerate(pairs)]

    for i, (s, hd) in enumerate(pairs):
        p, w_inter, m_t, m_new, decay, carry = gate_vals[i]
        n_row = n_in[s, hd:hd + 1, :]
        c_new = carry * c_in[s, hd] + kv[i]
        n_new = carry * n_row + jnp.sum(kd[i], axis=0, keepdims=True)
        num = sv[i] + qc[i] * w_inter
        qn = jnp.sum(q_of(s, hd).astype(F32) * n_row, axis=1, keepdims=True)
        den = jnp.sum(scores[i], axis=1, keepdims=True) + w_inter * qn
        h = num / jnp.maximum(jnp.abs(den), jnp.exp(-m_t))
        c_ref[s, hd] = c_new
        n_ref[s, hd:hd + 1, :] = n_new
        m_ref[s, hd:hd + 1, :] = jnp.broadcast_to(m_new, (1, GATE_LANES))
        hn = _rmsnorm(h, hg_ref[:, hd * DV:(hd + 1) * DV])
        o_pre = o_ref[rows_of(s), hd * DV:(hd + 1) * DV].astype(F32)
        gate = 0.5 * jnp.tanh(0.5 * o_pre) + 0.5
        out_ref[rows_of(s), hd * DV:(hd + 1) * DV] = (hn * gate).astype(out_ref.dtype)


def _mlstm(proj, gcol, grow, head_g, c0, n0, m0, *, row0, n_blocks, seqs, length, n_chunks):
    assert seqs == 1 or n_chunks == 1
    blk = seqs * length
    assert row0 % blk == 0
    b0 = row0 // blk
    n_seq = n_blocks * seqs

    def rows_map(col):
        return lambda b, c: (b0 + b * n_chunks + c, col)

    kernel = functools.partial(_mlstm_kernel, seqs=seqs, length=length, single_chunk=n_chunks == 1)
    return pl.pallas_call(
        kernel,
        out_shape=(jax.ShapeDtypeStruct((n_blocks * n_chunks * blk, HEADS * DV), BF16),
                   jax.ShapeDtypeStruct((n_seq, HEADS, DQK, DV), F32),
                   jax.ShapeDtypeStruct((n_seq, HEADS, DQK), F32),
                   jax.ShapeDtypeStruct((n_seq, HEADS, GATE_LANES), F32)),
        grid=(n_blocks, n_chunks),
        in_specs=[pl.BlockSpec((blk, 2 * HEADS * DQK), rows_map(0)),
                  pl.BlockSpec((blk, HEADS * DV), rows_map(1)),
                  pl.BlockSpec((blk, HEADS * DV), rows_map(2)),
                  pl.BlockSpec((blk, GATE_LANES), rows_map(0)),
                  pl.BlockSpec((1, seqs, 2 * HEADS, length), lambda b, c: (b * n_chunks + c, 0, 0, 0)),
                  pl.BlockSpec((1, HEADS * DV), lambda b, c: (0, 0)),
                  pl.BlockSpec((seqs, HEADS, DQK, DV), lambda b, c: (b, 0, 0, 0)),
                  pl.BlockSpec((seqs, HEADS, DQK), lambda b, c: (b, 0, 0)),
                  pl.BlockSpec((seqs, HEADS, GATE_LANES), lambda b, c: (b, 0, 0))],
        out_specs=(pl.BlockSpec((blk, HEADS * DV), lambda b, c: (b * n_chunks + c, 0)),
                   pl.BlockSpec((seqs, HEADS, DQK, DV), lambda b, c: (b, 0, 0, 0)),
                   pl.BlockSpec((seqs, HEADS, DQK), lambda b, c: (b, 0, 0)),
                   pl.BlockSpec((seqs, HEADS, GATE_LANES), lambda b, c: (b, 0, 0))),
        compiler_params=_params(("parallel", "arbitrary")),
        name=f"mlstm_l{length}",
    )(proj, proj, proj, gcol, grow, head_g, c0, n0, m0)


def _pool_mix(ext, first, n_rows, pos, wg_ref, scale_ref):
    outs = []
    for g, w in enumerate(POOL_WINDOWS):
        cols = slice(g * POOL_GROUP_DIM, (g + 1) * POOL_GROUP_DIM)
        e = ext[:, cols]
        total, span = e, 1
        while span < w:
            total = total + pltpu.roll(total, span, 0)
            span *= 2
        cnt = jnp.minimum(float(w), pos + 1.0)
        pooled = total[first:first + n_rows] / cnt - e[first:first + n_rows]
        outs.append(_dot(pooled.astype(BF16), wg_ref[g].astype(BF16)))
    mixed = jnp.concatenate(outs, axis=-1) * scale_ref[...]
    return mixed.astype(BF16)


def _pool_rows_kernel(u_ref, prev_ref, first_ref, wg_ref, scale_ref, o_ref, *, tiles_per_seq, pos0):
    tr = u_ref.shape[0]
    tile = pl.program_id(0) % tiles_per_seq
    halo = jnp.where(tile == 0, first_ref[...], prev_ref[...])
    ext = jnp.concatenate([halo, u_ref[...]], axis=0)
    pos = (lax.broadcasted_iota(jnp.int32, (tr, 1), 0) + (tile * tr + pos0)).astype(F32)
    o_ref[...] = _pool_mix(ext, N_META, tr, pos, wg_ref, scale_ref)


def _pool_rows(u, first_src, w_group, scale, *, row0, n_tiles, tr, tiles_per_seq, pos0, first_block):
    hb = tr // N_META
    t0 = row0 // tr
    kernel = functools.partial(_pool_rows_kernel, tiles_per_seq=tiles_per_seq, pos0=pos0)
    return pl.pallas_call(
        kernel,
        out_shape=jax.ShapeDtypeStruct((n_tiles * tr, D), BF16),
        grid=(n_tiles,),
        in_specs=[pl.BlockSpec((tr, D), lambda i: (t0 + i, 0)),
                  pl.BlockSpec((N_META, D), lambda i: (jnp.maximum((t0 + i) * hb - 1, 0), 0)),
                  pl.BlockSpec((N_META, D), lambda i: (first_block, 0)),
                  pl.BlockSpec((None, len(POOL_WINDOWS), POOL_GROUP_DIM, POOL_GROUP_DIM),
                               lambda i: (0, 0, 0, 0)),
                  pl.BlockSpec((1, D), lambda i: (0, 0))],
        out_specs=pl.BlockSpec((tr, D), lambda i: (i, 0)),
        compiler_params=_params(("parallel",)),
        name=f"pool_rows_{tr}",
    )(u, u, first_src, w_group, scale)


def _pool_seqs_kernel(ext_ref, wg_ref, scale_ref, o_ref, *, pos0):
    g, r, _ = ext_ref.shape
    n_new = r - 1 - POOL_BUF
    ext = ext_ref[...].reshape(g * r, D)
    outs = []
    pos = (lax.broadcasted_iota(jnp.int32, (g, n_new, 1), 1) + pos0).astype(F32)
    for gi, w in enumerate(POOL_WINDOWS):
        cols = slice(gi * POOL_GROUP_DIM, (gi + 1) * POOL_GROUP_DIM)
        e = ext[:, cols]
        total, span = e, 1
        while span < w:
            total = total + pltpu.roll(total, span, 0)
            span *= 2
        cnt = jnp.minimum(float(w), pos + 1.0)
        tot3 = total.reshape(g, r, POOL_GROUP_DIM)[:, 1 + POOL_BUF:, :]
        e3 = e.reshape(g, r, POOL_GROUP_DIM)[:, 1 + POOL_BUF:, :]
        pooled = (tot3 / cnt - e3).reshape(g * n_new, POOL_GROUP_DIM)
        outs.append(_dot(pooled.astype(BF16), wg_ref[gi].astype(BF16)))
    mixed = jnp.concatenate(outs, axis=-1) * scale_ref[...]
    o_ref[...] = mixed.astype(BF16)


def _pool_seqs(ext3, w_group, scale, *, seqs_per_block, pos0):
    n_seq, r, _ = ext3.shape
    n_new = r - 1 - POOL_BUF
    kernel = functools.partial(_pool_seqs_kernel, pos0=pos0)
    return pl.pallas_call(
        kernel,
        out_shape=jax.ShapeDtypeStruct((n_seq * n_new, D), BF16),
        grid=(n_seq // seqs_per_block,),
        in_specs=[pl.BlockSpec((seqs_per_block, r, D), lambda i: (i, 0, 0)),
                  pl.BlockSpec((None, len(POOL_WINDOWS), POOL_GROUP_DIM, POOL_GROUP_DIM),
                               lambda i: (0, 0, 0, 0)),
                  pl.BlockSpec((1, D), lambda i: (0, 0))],
        out_specs=pl.BlockSpec((seqs_per_block * n_new, D), lambda i: (i, 0)),
        compiler_params=_params(("parallel",)),
        name="pool_seqs",
    )(ext3, w_group, scale)


TM_PROJ = 1024
TN_PROJ = 1024
TM = 512
TM_AUX = 544
TM_FFN = 512
TF = 1024
TF_AUX = 512
CHUNK = 512
SAMPLE_SEQS = 4
POOL_TR = 512
POOL_SEQS = 16


def kernel(x_prompt, x_sample, state_mlstm_c, state_mlstm_n, state_mlstm_m, state_pool, meta_tokens,
           norm_mix_pre, norm_mix_post, norm_ffn_pre, norm_ffn_post, mlstm_w_in, mlstm_b_i, mlstm_b_f,
           mlstm_head_norm, mlstm_w_out, pool_w_in, pool_w_group, pool_scale, pool_w_out,
           ffn_w_up, ffn_w_down):
    B, S, _ = x_prompt.shape
    DB, DS, _ = x_sample.shape
    n_p, n_s = B * S, DB * DS
    rows_a = 2 * TM_AUX
    row_m = n_s
    assert n_s + N_META <= rows_a and n_p % TM_PROJ == 0 and n_p % TM == 0
    assert S % CHUNK == 0 and S % POOL_TR == 0 and row_m % N_META == 0

    h_p = x_prompt.reshape(n_p, D)
    h_a = jnp.concatenate([x_sample.reshape(n_s, D), meta_tokens.astype(F32),
                           jnp.zeros((rows_a - n_s - N_META, D), F32)], axis=0)
    pad_a = jnp.zeros((rows_a - n_s - N_META, D), BF16)

    def gain(a):
        return a.reshape(1, D).astype(F32)


    def matmul_norm_res(a_p, a_a, w, g):
        return _matmul_norm_res(a_p, w, g, h_p, tm=TM), _matmul_norm_res(a_a, w, g, h_a, tm=TM_AUX)

    def ffn(layer):
        g_pre, g_post = gain(norm_ffn_pre[layer]), gain(norm_ffn_post[layer])
        out_a, w_up, w_down = _ffn_cast(h_a, g_pre, ffn_w_up.astype(F32), ffn_w_down.astype(F32), g_post,
                                        layer=layer, tm=TM_AUX, tf=TF_AUX)
        return _ffn(h_p, g_pre, w_up, w_down, g_post, tm=TM_FFN, tf=TF), out_a

    w_in_t = jnp.swapaxes(mlstm_w_in[:1].astype(F32), 1, 2)
    w_gate_t = jnp.pad(w_in_t[0, PROJ_MAIN:], ((0, GATE_LANES - 2 * HEADS), (0, 0))).astype(BF16)
    b_gate = jnp.pad(jnp.concatenate([mlstm_b_i[0], mlstm_b_f[0]]).astype(F32),
                     (0, GATE_LANES - 2 * HEADS)).reshape(1, GATE_LANES)
    g_mix = gain(norm_mix_pre[0])
    proj_p, gcol_p = _norm_matmul_gates(h_p, g_mix, w_in_t, w_gate_t, b_gate, n=PROJ_MAIN, tm=TM_PROJ, tn=TN_PROJ)
    proj_a, gcol_a = _norm_matmul_gates(h_a, g_mix, w_in_t, w_gate_t, b_gate, n=PROJ_MAIN, tm=TM_AUX, tn=TN_PROJ)

    def gate_rows(gcol, r0, n_blk, seqs, length):
        g8 = gcol[r0:r0 + n_blk * seqs * length, :2 * HEADS]
        return g8.reshape(n_blk, seqs, length, 2 * HEADS).transpose(0, 1, 3, 2)

    head_g = mlstm_head_norm[0].reshape(1, HEADS * DV).astype(F32)
    zc = jnp.zeros((1, HEADS, DQK, DV), F32)
    zn = jnp.zeros((1, HEADS, DQK), F32)
    zm = jnp.zeros((1, HEADS, GATE_LANES), F32)
    mix_m, c_m, n_m, m_m = _mlstm(proj_a, gcol_a, gate_rows(gcol_a, row_m, 1, 1, N_META), head_g, zc, zn, zm,
                                  row0=row_m, n_blocks=1, seqs=1, length=N_META, n_chunks=1)
    mix_p, c_p, n_p_, m_p = _mlstm(proj_p, gcol_p, gate_rows(gcol_p, 0, B * (S // CHUNK), 1, CHUNK), head_g,
                                   jnp.broadcast_to(c_m, (B,) + c_m.shape[1:]),
                                   jnp.broadcast_to(n_m, (B,) + n_m.shape[1:]),
                                   jnp.broadcast_to(m_m, (B,) + m_m.shape[1:]),
                                   row0=0, n_blocks=B, seqs=1, length=CHUNK, n_chunks=S // CHUNK)
    m0_s = jnp.broadcast_to(state_mlstm_m[0].astype(F32)[:, :, None], (DB, HEADS, GATE_LANES))
    mix_s, c_s, n_s_, m_s = _mlstm(proj_a, gcol_a, gate_rows(gcol_a, 0, DB // SAMPLE_SEQS, SAMPLE_SEQS, DS),
                                   head_g, state_mlstm_c[0].astype(F32), state_mlstm_n[0].astype(F32), m0_s,
                                   row0=0, n_blocks=DB // SAMPLE_SEQS, seqs=SAMPLE_SEQS, length=DS, n_chunks=1)
    mix_a = jnp.concatenate([mix_s, mix_m, pad_a], axis=0)
    h_p, h_a = matmul_norm_res(mix_p, mix_a, mlstm_w_out[:1].astype(F32), gain(norm_mix_post[0]))
    h_p, h_a = ffn(0)

    w_pool_in, g_mix = pool_w_in[:1].astype(F32), gain(norm_mix_pre[1])
    u_p = _norm_matmul(h_p, g_mix, w_pool_in, tm=TM, out_dtype=F32)
    u_a = _norm_matmul(h_a, g_mix, w_pool_in, tm=TM_AUX, out_dtype=F32)
    w_group = pool_w_group[:1].astype(F32)
    p_scale = pool_scale[0].reshape(1, D).astype(F32)
    pmix_m = _pool_rows(u_a, jnp.zeros((N_META, D), F32), w_group, p_scale, row0=row_m, n_tiles=1,
                        tr=N_META, tiles_per_seq=1, pos0=0, first_block=0)
    pmix_p = _pool_rows(u_p, u_a, w_group, p_scale, row0=0, n_tiles=n_p // POOL_TR, tr=POOL_TR,
                        tiles_per_seq=S // POOL_TR, pos0=N_META, first_block=row_m // N_META)
    ext_s = jnp.concatenate([jnp.zeros((DB, 1, D), F32), state_pool[0].astype(F32),
                             u_a[:n_s].reshape(DB, DS, D)], axis=1)
    pmix_s = _pool_seqs(ext_s, w_group, p_scale, seqs_per_block=POOL_SEQS, pos0=PAST_LEN)
    pmix_a = jnp.concatenate([pmix_s, pmix_m, pad_a], axis=0)
    h_p, h_a = matmul_norm_res(pmix_p, pmix_a, pool_w_out[:1].astype(F32), gain(norm_mix_post[1]))
    h_p, h_a = ffn(1)

    y_prompt = h_p.reshape(B, S, D)
    y_sample = h_a[:n_s].reshape(DB, DS, D)
    dt_c, dt_n, dt_m, dt_pool = state_mlstm_c.dtype, state_mlstm_n.dtype, state_mlstm_m.dtype, state_pool.dtype
    u_prompt = u_p.reshape(B, S, D)
    return (y_prompt, y_sample,
            c_p[None].astype(dt_c), n_p_[None].astype(dt_n), m_p[None, :, :, 0].astype(dt_m),
            u_prompt[None, :, S - POOL_BUF:].astype(dt_pool),
            c_s[None].astype(dt_c), n_s_[None].astype(dt_n), m_s[None, :, :, 0].astype(dt_m),
            ext_s[None, :, -POOL_BUF:].astype(dt_pool))
```

```python
import functools

import jax
import jax.numpy as jnp
from jax import lax
from jax.experimental import pallas as pl
from jax.experimental.pallas import tpu as pltpu

D = 2048
N_META = 16
HEADS = 4
DQK = 256
DV = 512
D_FF = 4 * D
POOL_WINDOWS = (2, 4, 8, 16)
POOL_GROUP_DIM = D // len(POOL_WINDOWS)
POOL_BUF = 15
PAST_LEN = 16384
EPS = 1e-6
K_SCALE = DQK ** -0.5
PROJ_MAIN = 2 * HEADS * DQK + 2 * HEADS * DV
GATE_LANES = 128

VMEM_LIMIT = 56 * 1024 * 1024

F32 = jnp.float32
BF16 = jnp.bfloat16


def _params(semantics):
    return pltpu.CompilerParams(dimension_semantics=semantics, vmem_limit_bytes=VMEM_LIMIT)


def _rmsnorm(x, g):
    return x * lax.rsqrt(jnp.mean(x * x, axis=-1, keepdims=True) + EPS) * g


def _dot(a, b):
    return jnp.dot(a, b, preferred_element_type=F32)


SUB_ROWS = 256


def _norm_matmul_kernel(x_ref, g_ref, w_ref, o_ref, wb_ref):
    @pl.when(pl.program_id(0) == 0)
    def _():
        wb_ref[...] = w_ref[...].astype(BF16)

    for r in range(0, x_ref.shape[0], SUB_ROWS):
        rows = slice(r, min(r + SUB_ROWS, x_ref.shape[0]))
        xn = _rmsnorm(x_ref[rows, :], g_ref[...]).astype(BF16)
        o_ref[rows, :] = _dot(xn, wb_ref[...]).astype(o_ref.dtype)


def _dot_t(a, b):
    return lax.dot_general(a, b, (((1,), (1,)), ((), ())), preferred_element_type=F32)


def _norm_matmul_gates_kernel(x_ref, g_ref, wt_ref, wgt_ref, bg_ref, o_ref, gate_ref, xn_ref):
    @pl.when(pl.program_id(1) == 0)
    def _():
        xn = _rmsnorm(x_ref[...], g_ref[...]).astype(BF16)
        xn_ref[...] = xn
        z = _dot_t(xn, wgt_ref[...]) + bg_ref[...]
        lane = lax.broadcasted_iota(jnp.int32, z.shape, 1)
        log_sig = jnp.minimum(z, 0.0) - jnp.log1p(jnp.exp(-jnp.abs(z)))
        gate_ref[...] = jnp.where(lane >= HEADS, log_sig, z)

    o_ref[...] = _dot_t(xn_ref[...], wt_ref[...].astype(BF16)).astype(o_ref.dtype)


def _norm_matmul(x, g, w, *, tm, out_dtype):
    rows = x.shape[0]
    return pl.pallas_call(
        _norm_matmul_kernel,
        out_shape=jax.ShapeDtypeStruct((rows, D), out_dtype),
        grid=(rows // tm,),
        in_specs=[pl.BlockSpec((tm, D), lambda i: (i, 0)),
                  pl.BlockSpec((1, D), lambda i: (0, 0)),
                  pl.BlockSpec((None, D, D), lambda i: (0, 0, 0), pipeline_mode=pl.Buffered(1))],
        out_specs=pl.BlockSpec((tm, D), lambda i: (i, 0)),
        scratch_shapes=[pltpu.VMEM((D, D), BF16)],
        compiler_params=_params(("arbitrary",)),
        name="norm_matmul",
    )(x, g, w)


def _norm_matmul_gates(x, g, w_t, wg_t, bg, *, n, tm, tn):
    rows = x.shape[0]
    return pl.pallas_call(
        _norm_matmul_gates_kernel,
        out_shape=(jax.ShapeDtypeStruct((rows, n), BF16),
                   jax.ShapeDtypeStruct((rows, GATE_LANES), F32)),
        grid=(rows // tm, n // tn),
        in_specs=[pl.BlockSpec((tm, D), lambda i, j: (i, 0)),
                  pl.BlockSpec((1, D), lambda i, j: (0, 0)),
                  pl.BlockSpec((None, tn, D), lambda i, j: (0, j, 0)),
                  pl.BlockSpec((GATE_LANES, D), lambda i, j: (0, 0)),
                  pl.BlockSpec((1, GATE_LANES), lambda i, j: (0, 0))],
        out_specs=(pl.BlockSpec((tm, tn), lambda i, j: (i, j)),
                   pl.BlockSpec((tm, GATE_LANES), lambda i, j: (i, 0))),
        scratch_shapes=[pltpu.VMEM((tm, D), BF16)],
        compiler_params=_params(("parallel", "arbitrary")),
        name="norm_matmul_gates",
    )(x, g, w_t, wg_t, bg)


def _matmul_norm_res_kernel(a_ref, w_ref, g_ref, h_ref, o_ref, wb_ref):
    @pl.when(pl.program_id(0) == 0)
    def _():
        wb_ref[...] = w_ref[...].astype(BF16)

    y = _dot(a_ref[...], wb_ref[...])
    o_ref[...] = h_ref[...] + _rmsnorm(y, g_ref[...])


def _matmul_norm_res(a, w, g, h, *, tm):
    rows = a.shape[0]
    return pl.pallas_call(
        _matmul_norm_res_kernel,
        out_shape=jax.ShapeDtypeStruct((rows, D), F32),
        grid=(rows // tm,),
        in_specs=[pl.BlockSpec((tm, D), lambda i: (i, 0)),
                  pl.BlockSpec((None, D, D), lambda i: (0, 0, 0), pipeline_mode=pl.Buffered(1)),
                  pl.BlockSpec((1, D), lambda i: (0, 0)),
                  pl.BlockSpec((tm, D), lambda i: (i, 0))],
        out_specs=pl.BlockSpec((tm, D), lambda i: (i, 0)),
        scratch_shapes=[pltpu.VMEM((D, D), BF16)],
        compiler_params=_params(("arbitrary",)),
        name="matmul_norm_res",
    )(a, w, g, h)


def _ffn_begin(h_ref, gpre_ref, o_ref, xn_ref):
    @pl.when(pl.program_id(1) == 0)
    def _():
        xn_ref[...] = _rmsnorm(h_ref[...], gpre_ref[...]).astype(BF16)
        o_ref[...] = jnp.zeros_like(o_ref)


def _ffn_end(h_ref, gpost_ref, o_ref):
    @pl.when(pl.program_id(1) == pl.num_programs(1) - 1)
    def _():
        o_ref[...] = h_ref[...] + _rmsnorm(o_ref[...], gpost_ref[...])


def _ffn_kernel(h_ref, gpre_ref, wup_ref, wdown_ref, gpost_ref, o_ref, xn_ref):
    f = pl.program_id(1)
    last = pl.num_programs(1) - 1
    tm = h_ref.shape[0]
    subs = [slice(r, min(r + SUB_ROWS, tm)) for r in range(0, tm, SUB_ROWS)]

    def mlp_chunk(xn):
        a = jnp.maximum(_dot(xn, wup_ref[...]), 0.0)
        return _dot((a * a).astype(BF16), wdown_ref[...])

    @pl.when(f == 0)
    def _():
        for rows in subs:
            xn = _rmsnorm(h_ref[rows, :], gpre_ref[...]).astype(BF16)
            xn_ref[rows, :] = xn
            o_ref[rows, :] = mlp_chunk(xn)

    @pl.when(jnp.logical_and(f > 0, f < last))
    def _():
        o_ref[...] += mlp_chunk(xn_ref[...])

    @pl.when(f == last)
    def _():
        for rows in subs:
            y = o_ref[rows, :] + mlp_chunk(xn_ref[rows, :])
            o_ref[rows, :] = h_ref[rows, :] + _rmsnorm(y, gpost_ref[...])


def _ffn_cast_kernel(h_ref, gpre_ref, wup_ref, wdown_ref, gpost_ref, o_ref, wupb_ref, wdownb_ref, xn_ref,
                     wup_scr, wdown_scr):
    _ffn_begin(h_ref, gpre_ref, o_ref, xn_ref)
    wup_scr[...] = wup_ref[...].astype(BF16)
    wdown_scr[...] = wdown_ref[...].astype(BF16)
    a = jnp.maximum(_dot(xn_ref[...], wup_scr[...]), 0.0)
    o_ref[...] += _dot((a * a).astype(BF16), wdown_scr[...])

    @pl.when(pl.program_id(0) == 0)
    def _():
        wupb_ref[...] = wup_scr[...]
        wdownb_ref[...] = wdown_scr[...]

    _ffn_end(h_ref, gpost_ref, o_ref)


def _ffn(h, g_pre, w_up, w_down, g_post, *, tm, tf):
    rows = h.shape[0]
    return pl.pallas_call(
        _ffn_kernel,
        out_shape=jax.ShapeDtypeStruct((rows, D), F32),
        grid=(rows // tm, D_FF // tf),
        in_specs=[pl.BlockSpec((tm, D), lambda i, f: (i, 0)),
                  pl.BlockSpec((1, D), lambda i, f: (0, 0)),
                  pl.BlockSpec((D, tf), lambda i, f: (0, f)),
                  pl.BlockSpec((tf, D), lambda i, f: (f, 0)),
                  pl.BlockSpec((1, D), lambda i, f: (0, 0))],
        out_specs=pl.BlockSpec((tm, D), lambda i, f: (i, 0)),
        scratch_shapes=[pltpu.VMEM((tm, D), BF16)],
        compiler_params=_params(("parallel", "arbitrary")),
        name="ffn",
    )(h, g_pre, w_up, w_down, g_post)


def _ffn_cast(h, g_pre, w_up, w_down, g_post, *, layer, tm, tf):
    rows = h.shape[0]
    n_f = D_FF // tf

    def copy_chunk(i, f):
        return jnp.where(i == 0, f, n_f - 1)

    return pl.pallas_call(
        _ffn_cast_kernel,
        out_shape=(jax.ShapeDtypeStruct((rows, D), F32),
                   jax.ShapeDtypeStruct((D, D_FF), BF16),
                   jax.ShapeDtypeStruct((D_FF, D), BF16)),
        grid=(rows // tm, n_f),
        in_specs=[pl.BlockSpec((tm, D), lambda i, f: (i, 0)),
                  pl.BlockSpec((1, D), lambda i, f: (0, 0)),
                  pl.BlockSpec((None, D, tf), lambda i, f: (layer, 0, f)),
                  pl.BlockSpec((None, tf, D), lambda i, f: (layer, f, 0)),
                  pl.BlockSpec((1, D), lambda i, f: (0, 0))],
        out_specs=(pl.BlockSpec((tm, D), lambda i, f: (i, 0)),
                   pl.BlockSpec((D, tf), lambda i, f: (0, copy_chunk(i, f))),
                   pl.BlockSpec((tf, D), lambda i, f: (copy_chunk(i, f), 0))),
        scratch_shapes=[pltpu.VMEM((tm, D), BF16), pltpu.VMEM((D, tf), BF16), pltpu.VMEM((tf, D), BF16)],
        compiler_params=_params(("arbitrary", "arbitrary")),
        name="ffn_cast",
    )(h, g_pre, w_up, w_down, g_post)


def _mlstm_kernel(qk_ref, v_ref, o_ref, gcol_ref, grow_ref, hg_ref, c0_ref, n0_ref, m0_ref,
                  out_ref, c_ref, n_ref, m_ref, *, seqs, length, single_chunk):
    if single_chunk:
        c_in, n_in, m_in = c0_ref, n0_ref, m0_ref
    else:
        c_in, n_in, m_in = c_ref, n_ref, m_ref

        @pl.when(pl.program_id(1) == 0)
        def _():
            c_ref[...] = c0_ref[...]
            n_ref[...] = n0_ref[...]
            m_ref[...] = m0_ref[...]

    pairs = [(s, hd) for s in range(seqs) for hd in range(HEADS)]
    L = length
    t_idx = lax.broadcasted_iota(jnp.int32, (L, L), 0)
    s_idx = lax.broadcasted_iota(jnp.int32, (L, L), 1)
    causal = s_idx <= t_idx

    def rows_of(s):
        return slice(s * L, (s + 1) * L)

    def q_of(s, hd):
        return qk_ref[rows_of(s), hd * DQK:(hd + 1) * DQK]

    def k_of(s, hd):
        return qk_ref[rows_of(s), (HEADS + hd) * DQK:(HEADS + hd + 1) * DQK]

    def v_of(s, hd):
        return v_ref[rows_of(s), hd * DV:(hd + 1) * DV]

    qk = [_dot_t(q_of(s, hd), k_of(s, hd)) for s, hd in pairs]
    qc = [_dot(q_of(s, hd), c_in[s, hd].astype(BF16)) for s, hd in pairs]

    gate_vals = []
    for s, hd in pairs:
        gcol = gcol_ref[rows_of(s), :]
        grow = grow_ref[0, s]
        li_row, lf_row = grow[hd:hd + 1, :], grow[HEADS + hd:HEADS + hd + 1, :]
        li_col, lf_col = gcol[:, hd:hd + 1], gcol[:, HEADS + hd:HEADS + hd + 1]
        m = m_in[s, hd:hd + 1, 0:1]
        b_col = jnp.sum(jnp.where(causal, lf_row, 0.0), axis=1, keepdims=True)
        b_row = jnp.sum(jnp.where(t_idx <= s_idx, lf_col, 0.0), axis=0, keepdims=True)
        b_tot = b_col[L - 1:L, :]
        dmat = jnp.where(causal, b_col - b_row + li_row, -jnp.inf)
        m_inter = b_col + m
        m_t = jnp.maximum(m_inter, jnp.max(dmat, axis=1, keepdims=True))
        p = jnp.exp(dmat - m_t)
        w_inter = jnp.exp(m_inter - m_t)
        m_new = m_t[L - 1:L, :]
        decay = jnp.exp(b_tot - b_col + li_col - m_new)
        carry = jnp.exp(b_tot + m - m_new)
        gate_vals.append((p, w_inter, m_t, m_new, decay, carry))

    scores = [qk[i] * (gate_vals[i][0] * K_SCALE) for i in range(len(pairs))]
    kd = [k_of(s, hd).astype(F32) * (gate_vals[i][4] * K_SCALE) for i, (s, hd) in enumerate(pairs)]
    sv = [_dot(scores[i].astype(BF16), v_of(s, hd)) for i, (s, hd) in enumerate(pairs)]
    kv = [lax.dot_general(kd[i].astype(BF16), v_of(s, hd), (((0,), (0,)), ((), ())),
                          preferred_element_type=F32) for i, (s, hd) in enumerate(pairs)]

    for i, (s, hd) in enumerate(pairs):
        p, w_inter, m_t, m_new, decay, carry = gate_vals[i]
        n_row = n_in[s, hd:hd + 1, :]
        c_new = carry * c_in[s, hd] + kv[i]
        n_new = carry * n_row + jnp.sum(kd[i], axis=0, keepdims=True)
        num = sv[i] + qc[i] * w_inter
        qn = jnp.sum(q_of(s, hd).astype(F32) * n_row, axis=1, keepdims=True)
        den = jnp.sum(scores[i], axis=1, keepdims=True) + w_inter * qn
        h = num / jnp.maximum(jnp.abs(den), jnp.exp(-m_t))
        c_ref[s, hd] = c_new
        n_ref[s, hd:hd + 1, :] = n_new
        m_ref[s, hd:hd + 1, :] = jnp.broadcast_to(m_new, (1, GATE_LANES))
        hn = _rmsnorm(h, hg_ref[:, hd * DV:(hd + 1) * DV])
        o_pre = o_ref[rows_of(s), hd * DV:(hd + 1) * DV].astype(F32)
        gate = 0.5 * jnp.tanh(0.5 * o_pre) + 0.5
        out_ref[rows_of(s), hd * DV:(hd + 1) * DV] = (hn * gate).astype(out_ref.dtype)


def _mlstm(proj, gcol, grow, head_g, c0, n0, m0, *, row0, n_blocks, seqs, length, n_chunks):
    assert seqs == 1 or n_chunks == 1
    blk = seqs * length
    assert row0 % blk == 0
    b0 = row0 // blk
    n_seq = n_blocks * seqs

    def rows_map(col):
        return lambda b, c: (b0 + b * n_chunks + c, col)

    kernel = functools.partial(_mlstm_kernel, seqs=seqs, length=length, single_chunk=n_chunks == 1)
    return pl.pallas_call(
        kernel,
        out_shape=(jax.ShapeDtypeStruct((n_blocks * n_chunks * blk, HEADS * DV), BF16),
                   jax.ShapeDtypeStruct((n_seq, HEADS, DQK, DV), F32),
                   jax.ShapeDtypeStruct((n_seq, HEADS, DQK), F32),
                   jax.ShapeDtypeStruct((n_seq, HEADS, GATE_LANES), F32)),
        grid=(n_blocks, n_chunks),
        in_specs=[pl.BlockSpec((blk, 2 * HEADS * DQK), rows_map(0)),
                  pl.BlockSpec((blk, HEADS * DV), rows_map(1)),
                  pl.BlockSpec((blk, HEADS * DV), rows_map(2)),
                  pl.BlockSpec((blk, GATE_LANES), rows_map(0)),
                  pl.BlockSpec((1, seqs, 2 * HEADS, length), lambda b, c: (b * n_chunks + c, 0, 0, 0)),
                  pl.BlockSpec((1, HEADS * DV), lambda b, c: (0, 0)),
                  pl.BlockSpec((seqs, HEADS, DQK, DV), lambda b, c: (b, 0, 0, 0)),
                  pl.BlockSpec((seqs, HEADS, DQK), lambda b, c: (b, 0, 0)),
                  pl.BlockSpec((seqs, HEADS, GATE_LANES), lambda b, c: (b, 0, 0))],
        out_specs=(pl.BlockSpec((blk, HEADS * DV), lambda b, c: (b * n_chunks + c, 0)),
                   pl.BlockSpec((seqs, HEADS, DQK, DV), lambda b, c: (b, 0, 0, 0)),
                   pl.BlockSpec((seqs, HEADS, DQK), lambda b, c: (b, 0, 0)),
                   pl.BlockSpec((seqs, HEADS, GATE_LANES), lambda b, c: (b, 0, 0))),
        compiler_params=_params(("parallel", "arbitrary")),
        name=f"mlstm_l{length}",
    )(proj, proj, proj, gcol, grow, head_g, c0, n0, m0)


def _pool_mix(ext, first, n_rows, pos, wg_ref, scale_ref):
    outs = []
    for g, w in enumerate(POOL_WINDOWS):
        cols = slice(g * POOL_GROUP_DIM, (g + 1) * POOL_GROUP_DIM)
        e = ext[:, cols]
        total, span = e, 1
        while span < w:
            total = total + pltpu.roll(total, span, 0)
            span *= 2
        cnt = jnp.minimum(float(w), pos + 1.0)
        pooled = total[first:first + n_rows] / cnt - e[first:first + n_rows]
        outs.append(_dot(pooled.astype(BF16), wg_ref[g].astype(BF16)))
    mixed = jnp.concatenate(outs, axis=-1) * scale_ref[...]
    return mixed.astype(BF16)


def _pool_rows_kernel(u_ref, prev_ref, first_ref, wg_ref, scale_ref, o_ref, *, tiles_per_seq, pos0):
    tr = u_ref.shape[0]
    tile = pl.program_id(0) % tiles_per_seq
    halo = jnp.where(tile == 0, first_ref[...], prev_ref[...])
    ext = jnp.concatenate([halo, u_ref[...]], axis=0)
    pos = (lax.broadcasted_iota(jnp.int32, (tr, 1), 0) + (tile * tr + pos0)).astype(F32)
    o_ref[...] = _pool_mix(ext, N_META, tr, pos, wg_ref, scale_ref)


def _pool_rows(u, first_src, w_group, scale, *, row0, n_tiles, tr, tiles_per_seq, pos0, first_block):
    hb = tr // N_META
    t0 = row0 // tr
    kernel = functools.partial(_pool_rows_kernel, tiles_per_seq=tiles_per_seq, pos0=pos0)
    return pl.pallas_call(
        kernel,
        out_shape=jax.ShapeDtypeStruct((n_tiles * tr, D), BF16),
        grid=(n_tiles,),
        in_specs=[pl.BlockSpec((tr, D), lambda i: (t0 + i, 0)),
                  pl.BlockSpec((N_META, D), lambda i: (jnp.maximum((t0 + i) * hb - 1, 0), 0)),
                  pl.BlockSpec((N_META, D), lambda i: (first_block, 0)),
                  pl.BlockSpec((None, len(POOL_WINDOWS), POOL_GROUP_DIM, POOL_GROUP_DIM),
                               lambda i: (0, 0, 0, 0)),
                  pl.BlockSpec((1, D), lambda i: (0, 0))],
        out_specs=pl.BlockSpec((tr, D), lambda i: (i, 0)),
        compiler_params=_params(("parallel",)),
        name=f"pool_rows_{tr}",
    )(u, u, first_src, w_group, scale)


def _pool_seqs_kernel(prefix_ref, u_ref, wg_ref, scale_ref, o_ref, buf_ref, *, pos0):
    g = prefix_ref.shape[0]
    n_new = u_ref.shape[0] // g
    r = 1 + POOL_BUF + n_new
    prefix = prefix_ref[...]
    u3 = u_ref[...].reshape(g, n_new, D)
    ext = jnp.concatenate([jnp.zeros((g, 1, D), F32), prefix, u3], axis=1).reshape(g * r, D)
    buf_ref[...] = jnp.concatenate([prefix, u3], axis=1)[:, n_new:, :]
    outs = []
    pos = (lax.broadcasted_iota(jnp.int32, (g, n_new, 1), 1) + pos0).astype(F32)
    for gi, w in enumerate(POOL_WINDOWS):
        cols = slice(gi * POOL_GROUP_DIM, (gi + 1) * POOL_GROUP_DIM)
        e = ext[:, cols]
        total, span = e, 1
        while span < w:
            total = total + pltpu.roll(total, span, 0)
            span *= 2
        cnt = jnp.minimum(float(w), pos + 1.0)
        tot3 = total.reshape(g, r, POOL_GROUP_DIM)[:, 1 + POOL_BUF:, :]
        e3 = e.reshape(g, r, POOL_GROUP_DIM)[:, 1 + POOL_BUF:, :]
        pooled = (tot3 / cnt - e3).reshape(g * n_new, POOL_GROUP_DIM)
        outs.append(_dot(pooled.astype(BF16), wg_ref[gi].astype(BF16)))
    mixed = jnp.concatenate(outs, axis=-1) * scale_ref[...]
    o_ref[...] = mixed.astype(BF16)


def _pool_seqs(prefix, u, w_group, scale, *, n_new, seqs_per_block, pos0):
    n_seq = prefix.shape[1]
    kernel = functools.partial(_pool_seqs_kernel, pos0=pos0)
    return pl.pallas_call(
        kernel,
        out_shape=(jax.ShapeDtypeStruct((n_seq * n_new, D), BF16),
                   jax.ShapeDtypeStruct((n_seq, POOL_BUF, D), F32)),
        grid=(n_seq // seqs_per_block,),
        in_specs=[pl.BlockSpec((None, seqs_per_block, POOL_BUF, D), lambda i: (0, i, 0, 0)),
                  pl.BlockSpec((seqs_per_block * n_new, D), lambda i: (i, 0)),
                  pl.BlockSpec((None, len(POOL_WINDOWS), POOL_GROUP_DIM, POOL_GROUP_DIM),
                               lambda i: (0, 0, 0, 0)),
                  pl.BlockSpec((1, D), lambda i: (0, 0))],
        out_specs=(pl.BlockSpec((seqs_per_block * n_new, D), lambda i: (i, 0)),
                   pl.BlockSpec((seqs_per_block, POOL_BUF, D), lambda i: (i, 0, 0))),
        compiler_params=_params(("parallel",)),
        name="pool_seqs",
    )(prefix, u, w_group, scale)


TM_PROJ = 1024
TN_PROJ = 1024
TM = 512
TM_AUX = 544
TM_FFN = 512
TF = 1024
TF_AUX = 512
CHUNK = 512
SAMPLE_SEQS = 4
POOL_TR = 512
POOL_SEQS = 16


def kernel(x_prompt, x_sample, state_mlstm_c, state_mlstm_n, state_mlstm_m, state_pool, meta_tokens,
           norm_mix_pre, norm_mix_post, norm_ffn_pre, norm_ffn_post, mlstm_w_in, mlstm_b_i, mlstm_b_f,
           mlstm_head_norm, mlstm_w_out, pool_w_in, pool_w_group, pool_scale, pool_w_out,
           ffn_w_up, ffn_w_down):
    B, S, _ = x_prompt.shape
    DB, DS, _ = x_sample.shape
    n_p, n_s = B * S, DB * DS
    rows_a = 2 * TM_AUX
    row_m = n_s
    assert n_s + N_META <= rows_a and n_p % TM_PROJ == 0 and n_p % TM == 0
    assert S % CHUNK == 0 and S % POOL_TR == 0 and row_m % N_META == 0

    h_p = x_prompt.reshape(n_p, D)
    h_a = jnp.concatenate([x_sample.reshape(n_s, D), meta_tokens.astype(F32),
                           jnp.zeros((rows_a - n_s - N_META, D), F32)], axis=0)
    pad_a = jnp.zeros((rows_a - n_s - N_META, D), BF16)

    def gain(a):
        return a.reshape(1, D).astype(F32)


    def matmul_norm_res(a_p, a_a, w, g):
        return _matmul_norm_res(a_p, w, g, h_p, tm=TM), _matmul_norm_res(a_a, w, g, h_a, tm=TM_AUX)

    def ffn(layer):
        g_pre, g_post = gain(norm_ffn_pre[layer]), gain(norm_ffn_post[layer])
        out_a, w_up, w_down = _ffn_cast(h_a, g_pre, ffn_w_up.astype(F32), ffn_w_down.astype(F32), g_post,
                                        layer=layer, tm=TM_AUX, tf=TF_AUX)
        return _ffn(h_p, g_pre, w_up, w_down, g_post, tm=TM_FFN, tf=TF), out_a

    w_in_t = jnp.swapaxes(mlstm_w_in[:1].astype(F32), 1, 2)
    w_gate_t = jnp.pad(w_in_t[0, PROJ_MAIN:], ((0, GATE_LANES - 2 * HEADS), (0, 0))).astype(BF16)
    b_gate = jnp.pad(jnp.concatenate([mlstm_b_i[0], mlstm_b_f[0]]).astype(F32),
                     (0, GATE_LANES - 2 * HEADS)).reshape(1, GATE_LANES)
    g_mix = gain(norm_mix_pre[0])
    proj_p, gcol_p = _norm_matmul_gates(h_p, g_mix, w_in_t, w_gate_t, b_gate, n=PROJ_MAIN, tm=TM_PROJ, tn=TN_PROJ)
    proj_a, gcol_a = _norm_matmul_gates(h_a, g_mix, w_in_t, w_gate_t, b_gate, n=PROJ_MAIN, tm=TM_AUX, tn=TN_PROJ)

    def gate_rows(gcol, r0, n_blk, seqs, length):
        g8 = gcol[r0:r0 + n_blk * seqs * length, :2 * HEADS]
        return g8.reshape(n_blk, seqs, length, 2 * HEADS).transpose(0, 1, 3, 2)

    head_g = mlstm_head_norm[0].reshape(1, HEADS * DV).astype(F32)
    zc = jnp.zeros((1, HEADS, DQK, DV), F32)
    zn = jnp.zeros((1, HEADS, DQK), F32)
    zm = jnp.zeros((1, HEADS, GATE_LANES), F32)
    mix_m, c_m, n_m, m_m = _mlstm(proj_a, gcol_a, gate_rows(gcol_a, row_m, 1, 1, N_META), head_g, zc, zn, zm,
                                  row0=row_m, n_blocks=1, seqs=1, length=N_META, n_chunks=1)
    mix_p, c_p, n_p_, m_p = _mlstm(proj_p, gcol_p, gate_rows(gcol_p, 0, B * (S // CHUNK), 1, CHUNK), head_g,
                                   jnp.broadcast_to(c_m, (B,) + c_m.shape[1:]),
                                   jnp.broadcast_to(n_m, (B,) + n_m.shape[1:]),
                                   jnp.broadcast_to(m_m, (B,) + m_m.shape[1:]),
                                   row0=0, n_blocks=B, seqs=1, length=CHUNK, n_chunks=S // CHUNK)
    m0_s = jnp.broadcast_to(state_mlstm_m[0].astype(F32)[:, :, None], (DB, HEADS, GATE_LANES))
    mix_s, c_s, n_s_, m_s = _mlstm(proj_a, gcol_a, gate_rows(gcol_a, 0, DB // SAMPLE_SEQS, SAMPLE_SEQS, DS),
                                   head_g, state_mlstm_c[0].astype(F32), state_mlstm_n[0].astype(F32), m0_s,
                                   row0=0, n_blocks=DB // SAMPLE_SEQS, seqs=SAMPLE_SEQS, length=DS, n_chunks=1)
    mix_a = jnp.concatenate([mix_s, mix_m, pad_a], axis=0)
    h_p, h_a = matmul_norm_res(mix_p, mix_a, mlstm_w_out[:1].astype(F32), gain(norm_mix_post[0]))
    h_p, h_a = ffn(0)

    w_pool_in, g_mix = pool_w_in[:1].astype(F32), gain(norm_mix_pre[1])
    u_p = _norm_matmul(h_p, g_mix, w_pool_in, tm=TM, out_dtype=F32)
    u_a = _norm_matmul(h_a, g_mix, w_pool_in, tm=TM_AUX, out_dtype=F32)
    w_group = pool_w_group[:1].astype(F32)
    p_scale = pool_scale[0].reshape(1, D).astype(F32)
    pmix_m = _pool_rows(u_a, jnp.zeros((N_META, D), F32), w_group, p_scale, row0=row_m, n_tiles=1,
                        tr=N_META, tiles_per_seq=1, pos0=0, first_block=0)
    pmix_p = _pool_rows(u_p, u_a, w_group, p_scale, row0=0, n_tiles=n_p // POOL_TR, tr=POOL_TR,
                        tiles_per_seq=S // POOL_TR, pos0=N_META, first_block=row_m // N_META)
    pmix_s, pool_s = _pool_seqs(state_pool[:1].astype(F32), u_a, w_group, p_scale, n_new=DS,
                                seqs_per_block=POOL_SEQS, pos0=PAST_LEN)
    pmix_a = jnp.concatenate([pmix_s, pmix_m, pad_a], axis=0)
    h_p, h_a = matmul_norm_res(pmix_p, pmix_a, pool_w_out[:1].astype(F32), gain(norm_mix_post[1]))
    h_p, h_a = ffn(1)

    y_prompt = h_p.reshape(B, S, D)
    y_sample = h_a[:n_s].reshape(DB, DS, D)
    dt_c, dt_n, dt_m, dt_pool = state_mlstm_c.dtype, state_mlstm_n.dtype, state_mlstm_m.dtype, state_pool.dtype
    u_prompt = u_p.reshape(B, S, D)
    return (y_prompt, y_sample,
            c_p[None].astype(dt_c), n_p_[None].astype(dt_n), m_p[None, :, :, 0].astype(dt_m),
            u_prompt[None, :, S - POOL_BUF:].astype(dt_pool),
            c_s[None].astype(dt_c), n_s_[None].astype(dt_n), m_s[None, :, :, 0].astype(dt_m),
            pool_s[None].astype(dt_pool))
```

```python
import functools

import jax
import jax.numpy as jnp
from jax import lax
from jax.experimental import pallas as pl
from jax.experimental.pallas import tpu as pltpu

D = 2048
N_META = 16
HEADS = 4
DQK = 256
DV = 512
D_FF = 4 * D
POOL_WINDOWS = (2, 4, 8, 16)
POOL_GROUP_DIM = D // len(POOL_WINDOWS)
POOL_BUF = 15
PAST_LEN = 16384
EPS = 1e-6
K_SCALE = DQK ** -0.5
PROJ_MAIN = 2 * HEADS * DQK + 2 * HEADS * DV
GATE_LANES = 128

VMEM_LIMIT = 56 * 1024 * 1024

F32 = jnp.float32
BF16 = jnp.bfloat16


def _params(semantics):
    return pltpu.CompilerParams(dimension_semantics=semantics, vmem_limit_bytes=VMEM_LIMIT)


def _rmsnorm(x, g):
    return x * lax.rsqrt(jnp.mean(x * x, axis=-1, keepdims=True) + EPS) * g


def _dot(a, b):
    return jnp.dot(a, b, preferred_element_type=F32)


SUB_ROWS = 256


def _norm_matmul_kernel(x_ref, g_ref, w_ref, o_ref, wb_ref):
    @pl.when(pl.program_id(0) == 0)
    def _():
        wb_ref[...] = w_ref[...].astype(BF16)

    for r in range(0, x_ref.shape[0], SUB_ROWS):
        rows = slice(r, min(r + SUB_ROWS, x_ref.shape[0]))
        xn = _rmsnorm(x_ref[rows, :], g_ref[...]).astype(BF16)
        o_ref[rows, :] = _dot(xn, wb_ref[...]).astype(o_ref.dtype)


def _dot_t(a, b):
    return lax.dot_general(a, b, (((1,), (1,)), ((), ())), preferred_element_type=F32)


def _in_proj_first_block(x_ref, g_ref, w_t, wgt_ref, bg_ref, o_ref, gate_ref, xn_ref):
    tm = x_ref.shape[0]
    for r in range(0, tm, SUB_ROWS):
        rows = slice(r, min(r + SUB_ROWS, tm))
        xn = _rmsnorm(x_ref[rows, :], g_ref[...]).astype(BF16)
        xn_ref[rows, :] = xn
        z = _dot_t(xn, wgt_ref[...]) + bg_ref[...]
        lane = lax.broadcasted_iota(jnp.int32, z.shape, 1)
        log_sig = jnp.minimum(z, 0.0) - jnp.log1p(jnp.exp(-jnp.abs(z)))
        gate_ref[rows, :] = jnp.where(lane >= HEADS, log_sig, z)
        o_ref[rows, :] = _dot_t(xn, w_t).astype(o_ref.dtype)


def _norm_matmul_gates_kernel(x_ref, g_ref, wt_ref, wgt_ref, bg_ref, o_ref, gate_ref, xn_ref):
    @pl.when(pl.program_id(1) == 0)
    def _():
        _in_proj_first_block(x_ref, g_ref, wt_ref[...], wgt_ref, bg_ref, o_ref, gate_ref, xn_ref)

    @pl.when(pl.program_id(1) > 0)
    def _():
        o_ref[...] = _dot_t(xn_ref[...], wt_ref[...]).astype(o_ref.dtype)


def _norm_matmul_gates_cast_kernel(x_ref, g_ref, wt_ref, wgt_ref, bg_ref, o_ref, gate_ref, wtb_ref, xn_ref,
                                   w_scr):
    w_scr[...] = wt_ref[...].astype(BF16)

    @pl.when(pl.program_id(0) == 0)
    def _():
        wtb_ref[...] = w_scr[...]

    @pl.when(pl.program_id(1) == 0)
    def _():
        _in_proj_first_block(x_ref, g_ref, w_scr[...], wgt_ref, bg_ref, o_ref, gate_ref, xn_ref)

    @pl.when(pl.program_id(1) > 0)
    def _():
        o_ref[...] = _dot_t(xn_ref[...], w_scr[...]).astype(o_ref.dtype)


def _norm_matmul(x, g, w, *, tm, out_dtype):
    rows = x.shape[0]
    return pl.pallas_call(
        _norm_matmul_kernel,
        out_shape=jax.ShapeDtypeStruct((rows, D), out_dtype),
        grid=(rows // tm,),
        in_specs=[pl.BlockSpec((tm, D), lambda i: (i, 0)),
                  pl.BlockSpec((1, D), lambda i: (0, 0)),
                  pl.BlockSpec((None, D, D), lambda i: (0, 0, 0), pipeline_mode=pl.Buffered(1))],
        out_specs=pl.BlockSpec((tm, D), lambda i: (i, 0)),
        scratch_shapes=[pltpu.VMEM((D, D), BF16)],
        compiler_params=_params(("arbitrary",)),
        name="norm_matmul",
    )(x, g, w)


def _norm_matmul_gates(x, g, w_t, wg_t, bg, *, tm, tn):
    rows, n = x.shape[0], w_t.shape[0]
    return pl.pallas_call(
        _norm_matmul_gates_kernel,
        out_shape=(jax.ShapeDtypeStruct((rows, n), BF16),
                   jax.ShapeDtypeStruct((rows, GATE_LANES), F32)),
        grid=(rows // tm, n // tn),
        in_specs=[pl.BlockSpec((tm, D), lambda i, j: (i, 0)),
                  pl.BlockSpec((1, D), lambda i, j: (0, 0)),
                  pl.BlockSpec((tn, D), lambda i, j: (j, 0)),
                  pl.BlockSpec((GATE_LANES, D), lambda i, j: (0, 0)),
                  pl.BlockSpec((1, GATE_LANES), lambda i, j: (0, 0))],
        out_specs=(pl.BlockSpec((tm, tn), lambda i, j: (i, j)),
                   pl.BlockSpec((tm, GATE_LANES), lambda i, j: (i, 0))),
        scratch_shapes=[pltpu.VMEM((tm, D), BF16)],
        compiler_params=_params(("parallel", "arbitrary")),
        name="norm_matmul_gates",
    )(x, g, w_t, wg_t, bg)


def _norm_matmul_gates_cast(x, g, w_t, wg_t, bg, *, n, tm, tn):
    rows = x.shape[0]
    n_j = n // tn

    def copy_block(i, j):
        return jnp.where(i == 0, j, n_j - 1)

    return pl.pallas_call(
        _norm_matmul_gates_cast_kernel,
        out_shape=(jax.ShapeDtypeStruct((rows, n), BF16),
                   jax.ShapeDtypeStruct((rows, GATE_LANES), F32),
                   jax.ShapeDtypeStruct((n, D), BF16)),
        grid=(rows // tm, n_j),
        in_specs=[pl.BlockSpec((tm, D), lambda i, j: (i, 0)),
                  pl.BlockSpec((1, D), lambda i, j: (0, 0)),
                  pl.BlockSpec((None, tn, D), lambda i, j: (0, j, 0)),
                  pl.BlockSpec((GATE_LANES, D), lambda i, j: (0, 0)),
                  pl.BlockSpec((1, GATE_LANES), lambda i, j: (0, 0))],
        out_specs=(pl.BlockSpec((tm, tn), lambda i, j: (i, j)),
                   pl.BlockSpec((tm, GATE_LANES), lambda i, j: (i, 0)),
                   pl.BlockSpec((tn, D), lambda i, j: (copy_block(i, j), 0))),
        scratch_shapes=[pltpu.VMEM((tm, D), BF16), pltpu.VMEM((tn, D), BF16)],
        compiler_params=_params(("arbitrary", "arbitrary")),
        name="norm_matmul_gates_cast",
    )(x, g, w_t, wg_t, bg)


def _matmul_norm_res_kernel(a_ref, w_ref, g_ref, h_ref, o_ref, wb_ref):
    @pl.when(pl.program_id(0) == 0)
    def _():
        wb_ref[...] = w_ref[...].astype(BF16)

    y = _dot(a_ref[...], wb_ref[...])
    o_ref[...] = h_ref[...] + _rmsnorm(y, g_ref[...])


def _matmul_norm_res(a, w, g, h, *, tm):
    rows = a.shape[0]
    return pl.pallas_call(
        _matmul_norm_res_kernel,
        out_shape=jax.ShapeDtypeStruct((rows, D), F32),
        grid=(rows // tm,),
        in_specs=[pl.BlockSpec((tm, D), lambda i: (i, 0)),
                  pl.BlockSpec((None, D, D), lambda i: (0, 0, 0), pipeline_mode=pl.Buffered(1)),
                  pl.BlockSpec((1, D), lambda i: (0, 0)),
                  pl.BlockSpec((tm, D), lambda i: (i, 0))],
        out_specs=pl.BlockSpec((tm, D), lambda i: (i, 0)),
        scratch_shapes=[pltpu.VMEM((D, D), BF16)],
        compiler_params=_params(("arbitrary",)),
        name="matmul_norm_res",
    )(a, w, g, h)


def _ffn_begin(h_ref, gpre_ref, o_ref, xn_ref):
    @pl.when(pl.program_id(1) == 0)
    def _():
        xn_ref[...] = _rmsnorm(h_ref[...], gpre_ref[...]).astype(BF16)
        o_ref[...] = jnp.zeros_like(o_ref)


def _ffn_end(h_ref, gpost_ref, o_ref):
    @pl.when(pl.program_id(1) == pl.num_programs(1) - 1)
    def _():
        o_ref[...] = h_ref[...] + _rmsnorm(o_ref[...], gpost_ref[...])


def _ffn_kernel(h_ref, gpre_ref, wup_ref, wdown_ref, gpost_ref, o_ref, xn_ref):
    f = pl.program_id(1)
    last = pl.num_programs(1) - 1
    tm = h_ref.shape[0]
    subs = [slice(r, min(r + SUB_ROWS, tm)) for r in range(0, tm, SUB_ROWS)]

    def mlp_chunk(xn):
        a = jnp.maximum(_dot(xn, wup_ref[...]), 0.0)
        return _dot((a * a).astype(BF16), wdown_ref[...])

    @pl.when(f == 0)
    def _():
        for rows in subs:
            xn = _rmsnorm(h_ref[rows, :], gpre_ref[...]).astype(BF16)
            xn_ref[rows, :] = xn
            o_ref[rows, :] = mlp_chunk(xn)

    @pl.when(jnp.logical_and(f > 0, f < last))
    def _():
        o_ref[...] += mlp_chunk(xn_ref[...])

    @pl.when(f == last)
    def _():
        for rows in subs:
            y = o_ref[rows, :] + mlp_chunk(xn_ref[rows, :])
            o_ref[rows, :] = h_ref[rows, :] + _rmsnorm(y, gpost_ref[...])


def _ffn_cast_kernel(h_ref, gpre_ref, wup_ref, wdown_ref, gpost_ref, o_ref, wupb_ref, wdownb_ref, xn_ref,
                     wup_scr, wdown_scr):
    _ffn_begin(h_ref, gpre_ref, o_ref, xn_ref)
    wup_scr[...] = wup_ref[...].astype(BF16)
    wdown_scr[...] = wdown_ref[...].astype(BF16)
    a = jnp.maximum(_dot(xn_ref[...], wup_scr[...]), 0.0)
    o_ref[...] += _dot((a * a).astype(BF16), wdown_scr[...])

    @pl.when(pl.program_id(0) == 0)
    def _():
        wupb_ref[...] = wup_scr[...]
        wdownb_ref[...] = wdown_scr[...]

    _ffn_end(h_ref, gpost_ref, o_ref)


def _ffn(h, g_pre, w_up, w_down, g_post, *, tm, tf, first_tile=0):
    n_tiles = h.shape[0] // tm - first_tile
    return pl.pallas_call(
        _ffn_kernel,
        out_shape=jax.ShapeDtypeStruct((n_tiles * tm, D), F32),
        grid=(n_tiles, D_FF // tf),
        in_specs=[pl.BlockSpec((tm, D), lambda i, f: (first_tile + i, 0)),
                  pl.BlockSpec((1, D), lambda i, f: (0, 0)),
                  pl.BlockSpec((D, tf), lambda i, f: (0, f)),
                  pl.BlockSpec((tf, D), lambda i, f: (f, 0)),
                  pl.BlockSpec((1, D), lambda i, f: (0, 0))],
        out_specs=pl.BlockSpec((tm, D), lambda i, f: (i, 0)),
        scratch_shapes=[pltpu.VMEM((tm, D), BF16)],
        compiler_params=_params(("parallel", "arbitrary")),
        name="ffn",
    )(h, g_pre, w_up, w_down, g_post)


def _ffn_cast(h, g_pre, w_up, w_down, g_post, *, layer, tm, tf, n_tiles):
    n_f = D_FF // tf

    def copy_chunk(i, f):
        return jnp.where(i == 0, f, n_f - 1)

    return pl.pallas_call(
        _ffn_cast_kernel,
        out_shape=(jax.ShapeDtypeStruct((n_tiles * tm, D), F32),
                   jax.ShapeDtypeStruct((D, D_FF), BF16),
                   jax.ShapeDtypeStruct((D_FF, D), BF16)),
        grid=(n_tiles, n_f),
        in_specs=[pl.BlockSpec((tm, D), lambda i, f: (i, 0)),
                  pl.BlockSpec((1, D), lambda i, f: (0, 0)),
                  pl.BlockSpec((None, D, tf), lambda i, f: (layer, 0, f)),
                  pl.BlockSpec((None, tf, D), lambda i, f: (layer, f, 0)),
                  pl.BlockSpec((1, D), lambda i, f: (0, 0))],
        out_specs=(pl.BlockSpec((tm, D), lambda i, f: (i, 0)),
                   pl.BlockSpec((D, tf), lambda i, f: (0, copy_chunk(i, f))),
                   pl.BlockSpec((tf, D), lambda i, f: (copy_chunk(i, f), 0))),
        scratch_shapes=[pltpu.VMEM((tm, D), BF16), pltpu.VMEM((D, tf), BF16), pltpu.VMEM((tf, D), BF16)],
        compiler_params=_params(("arbitrary", "arbitrary")),
        name="ffn_cast",
    )(h, g_pre, w_up, w_down, g_post)


def _mlstm_kernel(qk_ref, v_ref, o_ref, gcol_ref, grow_ref, hg_ref, c0_ref, n0_ref, m0_ref,
                  out_ref, c_ref, n_ref, m_ref, *, seqs, length, single_chunk):
    if single_chunk:
        c_in, n_in, m_in = c0_ref, n0_ref, m0_ref
    else:
        c_in, n_in, m_in = c_ref, n_ref, m_ref

        @pl.when(pl.program_id(1) == 0)
        def _():
            c_ref[...] = c0_ref[...]
            n_ref[...] = n0_ref[...]
            m_ref[...] = m0_ref[...]

    pairs = [(s, hd) for s in range(seqs) for hd in range(HEADS)]
    L = length
    t_idx = lax.broadcasted_iota(jnp.int32, (L, L), 0)
    s_idx = lax.broadcasted_iota(jnp.int32, (L, L), 1)
    causal = s_idx <= t_idx

    def rows_of(s):
        return slice(s * L, (s + 1) * L)

    def q_of(s, hd):
        return qk_ref[rows_of(s), hd * DQK:(hd + 1) * DQK]

    def k_of(s, hd):
        return qk_ref[rows_of(s), (HEADS + hd) * DQK:(HEADS + hd + 1) * DQK]

    def v_of(s, hd):
        return v_ref[rows_of(s), hd * DV:(hd + 1) * DV]

    qk = [_dot_t(q_of(s, hd), k_of(s, hd)) for s, hd in pairs]
    qc = [_dot(q_of(s, hd), c_in[s, hd].astype(BF16)) for s, hd in pairs]

    gate_vals = []
    for s, hd in pairs:
        gcol = gcol_ref[rows_of(s), :]
        grow = grow_ref[0, s]
        li_row, lf_row = grow[hd:hd + 1, :], grow[HEADS + hd:HEADS + hd + 1, :]
        li_col, lf_col = gcol[:, hd:hd + 1], gcol[:, HEADS + hd:HEADS + hd + 1]
        m = m_in[s, hd:hd + 1, 0:1]
        b_col = jnp.sum(jnp.where(causal, lf_row, 0.0), axis=1, keepdims=True)
        b_row = jnp.sum(jnp.where(t_idx <= s_idx, lf_col, 0.0), axis=0, keepdims=True)
        b_tot = b_col[L - 1:L, :]
        dmat = jnp.where(causal, b_col - b_row + li_row, -jnp.inf)
        m_inter = b_col + m
        m_t = jnp.maximum(m_inter, jnp.max(dmat, axis=1, keepdims=True))
        p = jnp.exp(dmat - m_t)
        w_inter = jnp.exp(m_inter - m_t)
        m_new = m_t[L - 1:L, :]
        decay = jnp.exp(b_tot - b_col + li_col - m_new)
        carry = jnp.exp(b_tot + m - m_new)
        gate_vals.append((p, w_inter, m_t, m_new, decay, carry))

    scores = [qk[i] * (gate_vals[i][0] * K_SCALE) for i in range(len(pairs))]
    kd = [k_of(s, hd).astype(F32) * (gate_vals[i][4] * K_SCALE) for i, (s, hd) in enumerate(pairs)]
    sv = [_dot(scores[i].astype(BF16), v_of(s, hd)) for i, (s, hd) in enumerate(pairs)]
    kv = [lax.dot_general(kd[i].astype(BF16), v_of(s, hd), (((0,), (0,)), ((), ())),
                          preferred_element_type=F32) for i, (s, hd) in enumerate(pairs)]

    for i, (s, hd) in enumerate(pairs):
        p, w_inter, m_t, m_new, decay, carry = gate_vals[i]
        n_row = n_in[s, hd:hd + 1, :]
        c_new = carry * c_in[s, hd] + kv[i]
        n_new = carry * n_row + jnp.sum(kd[i], axis=0, keepdims=True)
        num = sv[i] + qc[i] * w_inter
        qn = jnp.sum(q_of(s, hd).astype(F32) * n_row, axis=1, keepdims=True)
        den = jnp.sum(scores[i], axis=1, keepdims=True) + w_inter * qn
        h = num / jnp.maximum(jnp.abs(den), jnp.exp(-m_t))
        c_ref[s, hd] = c_new
        n_ref[s, hd:hd + 1, :] = n_new
        m_ref[s, hd:hd + 1, :] = jnp.broadcast_to(m_new, (1, GATE_LANES))
        hn = _rmsnorm(h, hg_ref[:, hd * DV:(hd + 1) * DV])
        o_pre = o_ref[rows_of(s), hd * DV:(hd + 1) * DV].astype(F32)
        gate = 0.5 * jnp.tanh(0.5 * o_pre) + 0.5
        out_ref[rows_of(s), hd * DV:(hd + 1) * DV] = (hn * gate).astype(out_ref.dtype)


def _mlstm(proj, gcol, grow, head_g, c0, n0, m0, *, row0, n_blocks, seqs, length, n_chunks):
    assert seqs == 1 or n_chunks == 1
    blk = seqs * length
    assert row0 % blk == 0
    b0 = row0 // blk
    n_seq = n_blocks * seqs

    def rows_map(col):
        return lambda b, c: (b0 + b * n_chunks + c, col)

    kernel = functools.partial(_mlstm_kernel, seqs=seqs, length=length, single_chunk=n_chunks == 1)
    return pl.pallas_call(
        kernel,
        out_shape=(jax.ShapeDtypeStruct((n_blocks * n_chunks * blk, HEADS * DV), BF16),
                   jax.ShapeDtypeStruct((n_seq, HEADS, DQK, DV), F32),
                   jax.ShapeDtypeStruct((n_seq, HEADS, DQK), F32),
                   jax.ShapeDtypeStruct((n_seq, HEADS, GATE_LANES), F32)),
        grid=(n_blocks, n_chunks),
        in_specs=[pl.BlockSpec((blk, 2 * HEADS * DQK), rows_map(0)),
                  pl.BlockSpec((blk, HEADS * DV), rows_map(1)),
                  pl.BlockSpec((blk, HEADS * DV), rows_map(2)),
                  pl.BlockSpec((blk, GATE_LANES), rows_map(0)),
                  pl.BlockSpec((1, seqs, 2 * HEADS, length), lambda b, c: (b * n_chunks + c, 0, 0, 0)),
                  pl.BlockSpec((1, HEADS * DV), lambda b, c: (0, 0)),
                  pl.BlockSpec((seqs, HEADS, DQK, DV), lambda b, c: (b, 0, 0, 0)),
                  pl.BlockSpec((seqs, HEADS, DQK), lambda b, c: (b, 0, 0)),
                  pl.BlockSpec((seqs, HEADS, GATE_LANES), lambda b, c: (b, 0, 0))],
        out_specs=(pl.BlockSpec((blk, HEADS * DV), lambda b, c: (b * n_chunks + c, 0)),
                   pl.BlockSpec((seqs, HEADS, DQK, DV), lambda b, c: (b, 0, 0, 0)),
                   pl.BlockSpec((seqs, HEADS, DQK), lambda b, c: (b, 0, 0)),
                   pl.BlockSpec((seqs, HEADS, GATE_LANES), lambda b, c: (b, 0, 0))),
        compiler_params=_params(("parallel", "arbitrary")),
        name=f"mlstm_l{length}",
    )(proj, proj, proj, gcol, grow, head_g, c0, n0, m0)


def _pool_mix(ext, first, n_rows, pos, wg_ref, scale_ref):
    outs = []
    for g, w in enumerate(POOL_WINDOWS):
        cols = slice(g * POOL_GROUP_DIM, (g + 1) * POOL_GROUP_DIM)
        e = ext[:, cols]
        total, span = e, 1
        while span < w:
            total = total + pltpu.roll(total, span, 0)
            span *= 2
        cnt = jnp.minimum(float(w), pos + 1.0)
        pooled = total[first:first + n_rows] / cnt - e[first:first + n_rows]
        outs.append(_dot(pooled.astype(BF16), wg_ref[g].astype(BF16)))
    mixed = jnp.concatenate(outs, axis=-1) * scale_ref[...]
    return mixed.astype(BF16)


def _pool_rows_kernel(u_ref, prev_ref, first_ref, wg_ref, scale_ref, o_ref, *, tiles_per_seq, pos0):
    tr = u_ref.shape[0]
    tile = pl.program_id(0) % tiles_per_seq
    halo = jnp.where(tile == 0, first_ref[...], prev_ref[...])
    ext = jnp.concatenate([halo, u_ref[...]], axis=0)
    pos = (lax.broadcasted_iota(jnp.int32, (tr, 1), 0) + (tile * tr + pos0)).astype(F32)
    o_ref[...] = _pool_mix(ext, N_META, tr, pos, wg_ref, scale_ref)


def _pool_rows(u, first_src, w_group, scale, *, row0, n_tiles, tr, tiles_per_seq, pos0, first_block):
    hb = tr // N_META
    t0 = row0 // tr
    kernel = functools.partial(_pool_rows_kernel, tiles_per_seq=tiles_per_seq, pos0=pos0)
    return pl.pallas_call(
        kernel,
        out_shape=jax.ShapeDtypeStruct((n_tiles * tr, D), BF16),
        grid=(n_tiles,),
        in_specs=[pl.BlockSpec((tr, D), lambda i: (t0 + i, 0)),
                  pl.BlockSpec((N_META, D), lambda i: (jnp.maximum((t0 + i) * hb - 1, 0), 0)),
                  pl.BlockSpec((N_META, D), lambda i: (first_block, 0)),
                  pl.BlockSpec((None, len(POOL_WINDOWS), POOL_GROUP_DIM, POOL_GROUP_DIM),
                               lambda i: (0, 0, 0, 0)),
                  pl.BlockSpec((1, D), lambda i: (0, 0))],
        out_specs=pl.BlockSpec((tr, D), lambda i: (i, 0)),
        compiler_params=_params(("parallel",)),
        name=f"pool_rows_{tr}",
    )(u, u, first_src, w_group, scale)


def _pool_seqs_kernel(prefix_ref, u_ref, wg_ref, scale_ref, o_ref, buf_ref, *, pos0):
    g = prefix_ref.shape[0]
    n_new = u_ref.shape[0] // g
    r = 1 + POOL_BUF + n_new
    prefix = prefix_ref[...]
    u3 = u_ref[...].reshape(g, n_new, D)
    ext = jnp.concatenate([jnp.zeros((g, 1, D), F32), prefix, u3], axis=1).reshape(g * r, D)
    buf_ref[...] = jnp.concatenate([prefix, u3], axis=1)[:, n_new:, :]
    outs = []
    pos = (lax.broadcasted_iota(jnp.int32, (g, n_new, 1), 1) + pos0).astype(F32)
    for gi, w in enumerate(POOL_WINDOWS):
        cols = slice(gi * POOL_GROUP_DIM, (gi + 1) * POOL_GROUP_DIM)
        e = ext[:, cols]
        total, span = e, 1
        while span < w:
            total = total + pltpu.roll(total, span, 0)
            span *= 2
        cnt = jnp.minimum(float(w), pos + 1.0)
        tot3 = total.reshape(g, r, POOL_GROUP_DIM)[:, 1 + POOL_BUF:, :]
        e3 = e.reshape(g, r, POOL_GROUP_DIM)[:, 1 + POOL_BUF:, :]
        pooled = (tot3 / cnt - e3).reshape(g * n_new, POOL_GROUP_DIM)
        outs.append(_dot(pooled.astype(BF16), wg_ref[gi].astype(BF16)))
    mixed = jnp.concatenate(outs, axis=-1) * scale_ref[...]
    o_ref[...] = mixed.astype(BF16)


def _pool_seqs(prefix, u, w_group, scale, *, n_new, seqs_per_block, pos0):
    n_seq = prefix.shape[1]
    kernel = functools.partial(_pool_seqs_kernel, pos0=pos0)
    return pl.pallas_call(
        kernel,
        out_shape=(jax.ShapeDtypeStruct((n_seq * n_new, D), BF16),
                   jax.ShapeDtypeStruct((n_seq, POOL_BUF, D), F32)),
        grid=(n_seq // seqs_per_block,),
        in_specs=[pl.BlockSpec((None, seqs_per_block, POOL_BUF, D), lambda i: (0, i, 0, 0)),
                  pl.BlockSpec((seqs_per_block * n_new, D), lambda i: (i, 0)),
                  pl.BlockSpec((None, len(POOL_WINDOWS), POOL_GROUP_DIM, POOL_GROUP_DIM),
                               lambda i: (0, 0, 0, 0)),
                  pl.BlockSpec((1, D), lambda i: (0, 0))],
        out_specs=(pl.BlockSpec((seqs_per_block * n_new, D), lambda i: (i, 0)),
                   pl.BlockSpec((seqs_per_block, POOL_BUF, D), lambda i: (i, 0, 0))),
        compiler_params=_params(("parallel",)),
        name="pool_seqs",
    )(prefix, u, w_group, scale)


TM_PROJ = 1024
TN_PROJ = 1024
TM = 512
TM_AUX = 544
TM_FFN = 512
TF = 1024
TF_AUX = 512
CHUNK = 512
SAMPLE_SEQS = 4
POOL_TR = 512
POOL_SEQS = 16


def kernel(x_prompt, x_sample, state_mlstm_c, state_mlstm_n, state_mlstm_m, state_pool, meta_tokens,
           norm_mix_pre, norm_mix_post, norm_ffn_pre, norm_ffn_post, mlstm_w_in, mlstm_b_i, mlstm_b_f,
           mlstm_head_norm, mlstm_w_out, pool_w_in, pool_w_group, pool_scale, pool_w_out,
           ffn_w_up, ffn_w_down):
    B, S, _ = x_prompt.shape
    DB, DS, _ = x_sample.shape
    n_p, n_s = B * S, DB * DS
    rows_a = 2 * TM_AUX
    row_m = n_s
    assert n_s + N_META <= rows_a and n_p % TM_PROJ == 0 and n_p % TM == 0
    assert S % CHUNK == 0 and S % POOL_TR == 0 and row_m % N_META == 0

    h_p = x_prompt.reshape(n_p, D)
    h_a = jnp.concatenate([x_sample.reshape(n_s, D), meta_tokens.astype(F32),
                           jnp.zeros((rows_a - n_s - N_META, D), F32)], axis=0)
    pad_a = jnp.zeros((rows_a - n_s - N_META, D), BF16)

    def gain(a):
        return a.reshape(1, D).astype(F32)


    def matmul_norm_res(a_p, a_a, w, g):
        return _matmul_norm_res(a_p, w, g, h_p, tm=TM), _matmul_norm_res(a_a, w, g, h_a, tm=TM_AUX)

    def ffn(layer):
        g_pre, g_post = gain(norm_ffn_pre[layer]), gain(norm_ffn_post[layer])
        out_a, w_up, w_down = _ffn_cast(h_a, g_pre, ffn_w_up.astype(F32), ffn_w_down.astype(F32), g_post,
                                        layer=layer, tm=TM_AUX, tf=TF_AUX, n_tiles=1)
        out_a2 = _ffn(h_a, g_pre, w_up, w_down, g_post, tm=TM_AUX, tf=TF, first_tile=1)
        out_p = _ffn(h_p, g_pre, w_up, w_down, g_post, tm=TM_FFN, tf=TF)
        return out_p, jnp.concatenate([out_a, out_a2], axis=0)

    w_in_t = jnp.swapaxes(mlstm_w_in[:1].astype(F32), 1, 2)
    w_gate_t = jnp.pad(w_in_t[0, PROJ_MAIN:], ((0, GATE_LANES - 2 * HEADS), (0, 0))).astype(BF16)
    b_gate = jnp.pad(jnp.concatenate([mlstm_b_i[0], mlstm_b_f[0]]).astype(F32),
                     (0, GATE_LANES - 2 * HEADS)).reshape(1, GATE_LANES)
    g_mix = gain(norm_mix_pre[0])
    proj_a, gcol_a, w_in_tb = _norm_matmul_gates_cast(h_a, g_mix, w_in_t, w_gate_t, b_gate, n=PROJ_MAIN,
                                                      tm=TM_AUX, tn=TN_PROJ)
    proj_p, gcol_p = _norm_matmul_gates(h_p, g_mix, w_in_tb, w_gate_t, b_gate, tm=TM_PROJ, tn=TN_PROJ)

    def gate_rows(gcol, r0, n_blk, seqs, length):
        g8 = gcol[r0:r0 + n_blk * seqs * length, :2 * HEADS]
        return g8.reshape(n_blk, seqs, length, 2 * HEADS).transpose(0, 1, 3, 2)

    head_g = mlstm_head_norm[0].reshape(1, HEADS * DV).astype(F32)
    zc = jnp.zeros((1, HEADS, DQK, DV), F32)
    zn = jnp.zeros((1, HEADS, DQK), F32)
    zm = jnp.zeros((1, HEADS, GATE_LANES), F32)
    mix_m, c_m, n_m, m_m = _mlstm(proj_a, gcol_a, gate_rows(gcol_a, row_m, 1, 1, N_META), head_g, zc, zn, zm,
                                  row0=row_m, n_blocks=1, seqs=1, length=N_META, n_chunks=1)
    mix_p, c_p, n_p_, m_p = _mlstm(proj_p, gcol_p, gate_rows(gcol_p, 0, B * (S // CHUNK), 1, CHUNK), head_g,
                                   jnp.broadcast_to(c_m, (B,) + c_m.shape[1:]),
                                   jnp.broadcast_to(n_m, (B,) + n_m.shape[1:]),
                                   jnp.broadcast_to(m_m, (B,) + m_m.shape[1:]),
                                   row0=0, n_blocks=B, seqs=1, length=CHUNK, n_chunks=S // CHUNK)
    m0_s = jnp.broadcast_to(state_mlstm_m[0].astype(F32)[:, :, None], (DB, HEADS, GATE_LANES))
    mix_s, c_s, n_s_, m_s = _mlstm(proj_a, gcol_a, gate_rows(gcol_a, 0, DB // SAMPLE_SEQS, SAMPLE_SEQS, DS),
                                   head_g, state_mlstm_c[0].astype(F32), state_mlstm_n[0].astype(F32), m0_s,
                                   row0=0, n_blocks=DB // SAMPLE_SEQS, seqs=SAMPLE_SEQS, length=DS, n_chunks=1)
    mix_a = jnp.concatenate([mix_s, mix_m, pad_a], axis=0)
    h_p, h_a = matmul_norm_res(mix_p, mix_a, mlstm_w_out[:1].astype(F32), gain(norm_mix_post[0]))
    h_p, h_a = ffn(0)

    w_pool_in, g_mix = pool_w_in[:1].astype(F32), gain(norm_mix_pre[1])
    u_p = _norm_matmul(h_p, g_mix, w_pool_in, tm=TM, out_dtype=F32)
    u_a = _norm_matmul(h_a, g_mix, w_pool_in, tm=TM_AUX, out_dtype=F32)
    w_group = pool_w_group[:1].astype(F32)
    p_scale = pool_scale[0].reshape(1, D).astype(F32)
    pmix_m = _pool_rows(u_a, jnp.zeros((N_META, D), F32), w_group, p_scale, row0=row_m, n_tiles=1,
                        tr=N_META, tiles_per_seq=1, pos0=0, first_block=0)
    pmix_p = _pool_rows(u_p, u_a, w_group, p_scale, row0=0, n_tiles=n_p // POOL_TR, tr=POOL_TR,
                        tiles_per_seq=S // POOL_TR, pos0=N_META, first_block=row_m // N_META)
    pmix_s, pool_s = _pool_seqs(state_pool[:1].astype(F32), u_a, w_group, p_scale, n_new=DS,
                                seqs_per_block=POOL_SEQS, pos0=PAST_LEN)
    pmix_a = jnp.concatenate([pmix_s, pmix_m, pad_a], axis=0)
    h_p, h_a = matmul_norm_res(pmix_p, pmix_a, pool_w_out[:1].astype(F32), gain(norm_mix_post[1]))
    h_p, h_a = ffn(1)

    y_prompt = h_p.reshape(B, S, D)
    y_sample = h_a[:n_s].reshape(DB, DS, D)
    dt_c, dt_n, dt_m, dt_pool = state_mlstm_c.dtype, state_mlstm_n.dtype, state_mlstm_m.dtype, state_pool.dtype
    u_prompt = u_p.reshape(B, S, D)
    return (y_prompt, y_sample,
            c_p[None].astype(dt_c), n_p_[None].astype(dt_n), m_p[None, :, :, 0].astype(dt_m),
            u_prompt[None, :, S - POOL_BUF:].astype(dt_pool),
            c_s[None].astype(dt_c), n_s_[None].astype(dt_n), m_s[None, :, :, 0].astype(dt_m),
            pool_s[None].astype(dt_pool))
```

```python
import functools

import jax
import jax.numpy as jnp
from jax import lax
from jax.experimental import pallas as pl
from jax.experimental.pallas import tpu as pltpu

D = 2048
N_META = 16
HEADS = 4
DQK = 256
DV = 512
D_FF = 4 * D
POOL_WINDOWS = (2, 4, 8, 16)
POOL_GROUP_DIM = D // len(POOL_WINDOWS)
POOL_BUF = 15
PAST_LEN = 16384
EPS = 1e-6
K_SCALE = DQK ** -0.5
PROJ_MAIN = 2 * HEADS * DQK + 2 * HEADS * DV
GATE_LANES = 128

VMEM_LIMIT = 56 * 1024 * 1024

F32 = jnp.float32
BF16 = jnp.bfloat16


def _params(semantics):
    return pltpu.CompilerParams(dimension_semantics=semantics, vmem_limit_bytes=VMEM_LIMIT)


def _rmsnorm(x, g):
    return x * lax.rsqrt(jnp.mean(x * x, axis=-1, keepdims=True) + EPS) * g


def _dot(a, b):
    return jnp.dot(a, b, preferred_element_type=F32)


SUB_ROWS = 256


def _norm_matmul_kernel(x_ref, g_ref, w_ref, o_ref, wb_ref):
    @pl.when(pl.program_id(0) == 0)
    def _():
        wb_ref[...] = w_ref[...].astype(BF16)

    for r in range(0, x_ref.shape[0], SUB_ROWS):
        rows = slice(r, min(r + SUB_ROWS, x_ref.shape[0]))
        xn = _rmsnorm(x_ref[rows, :], g_ref[...]).astype(BF16)
        o_ref[rows, :] = _dot(xn, wb_ref[...]).astype(o_ref.dtype)


def _dot_t(a, b):
    return lax.dot_general(a, b, (((1,), (1,)), ((), ())), preferred_element_type=F32)


def _in_proj_first_block(x_ref, g_ref, w_t, wgt_ref, bg_ref, o_ref, gate_ref, xn_ref):
    tm = x_ref.shape[0]
    for r in range(0, tm, SUB_ROWS):
        rows = slice(r, min(r + SUB_ROWS, tm))
        xn = _rmsnorm(x_ref[rows, :], g_ref[...]).astype(BF16)
        xn_ref[rows, :] = xn
        z = _dot_t(xn, wgt_ref[...]) + bg_ref[...]
        lane = lax.broadcasted_iota(jnp.int32, z.shape, 1)
        log_sig = jnp.minimum(z, 0.0) - jnp.log1p(jnp.exp(-jnp.abs(z)))
        gate_ref[rows, :] = jnp.where(lane >= HEADS, log_sig, z)
        o_ref[rows, :] = _dot_t(xn, w_t).astype(o_ref.dtype)


def _norm_matmul_gates_kernel(x_ref, g_ref, wt_ref, wgt_ref, bg_ref, o_ref, gate_ref, xn_ref):
    @pl.when(pl.program_id(1) == 0)
    def _():
        _in_proj_first_block(x_ref, g_ref, wt_ref[...], wgt_ref, bg_ref, o_ref, gate_ref, xn_ref)

    @pl.when(pl.program_id(1) > 0)
    def _():
        o_ref[...] = _dot_t(xn_ref[...], wt_ref[...]).astype(o_ref.dtype)


def _norm_matmul_gates_cast_kernel(x_ref, g_ref, wt_ref, wgt_ref, bg_ref, o_ref, gate_ref, wtb_ref, xn_ref):
    wtb_ref[...] = wt_ref[...].astype(BF16)

    @pl.when(pl.program_id(1) == 0)
    def _():
        _in_proj_first_block(x_ref, g_ref, wtb_ref[...], wgt_ref, bg_ref, o_ref, gate_ref, xn_ref)

    @pl.when(pl.program_id(1) > 0)
    def _():
        o_ref[...] = _dot_t(xn_ref[...], wtb_ref[...]).astype(o_ref.dtype)


def _norm_matmul(x, g, w, *, tm, out_dtype):
    rows = x.shape[0]
    return pl.pallas_call(
        _norm_matmul_kernel,
        out_shape=jax.ShapeDtypeStruct((rows, D), out_dtype),
        grid=(rows // tm,),
        in_specs=[pl.BlockSpec((tm, D), lambda i: (i, 0)),
                  pl.BlockSpec((1, D), lambda i: (0, 0)),
                  pl.BlockSpec((None, D, D), lambda i: (0, 0, 0), pipeline_mode=pl.Buffered(1))],
        out_specs=pl.BlockSpec((tm, D), lambda i: (i, 0)),
        scratch_shapes=[pltpu.VMEM((D, D), BF16)],
        compiler_params=_params(("arbitrary",)),
        name="norm_matmul",
    )(x, g, w)


def _norm_matmul_gates(x, g, w_t, wg_t, bg, *, tm, tn):
    rows, n = x.shape[0], w_t.shape[0]
    return pl.pallas_call(
        _norm_matmul_gates_kernel,
        out_shape=(jax.ShapeDtypeStruct((rows, n), BF16),
                   jax.ShapeDtypeStruct((rows, GATE_LANES), F32)),
        grid=(rows // tm, n // tn),
        in_specs=[pl.BlockSpec((tm, D), lambda i, j: (i, 0)),
                  pl.BlockSpec((1, D), lambda i, j: (0, 0)),
                  pl.BlockSpec((tn, D), lambda i, j: (j, 0)),
                  pl.BlockSpec((GATE_LANES, D), lambda i, j: (0, 0)),
                  pl.BlockSpec((1, GATE_LANES), lambda i, j: (0, 0))],
        out_specs=(pl.BlockSpec((tm, tn), lambda i, j: (i, j)),
                   pl.BlockSpec((tm, GATE_LANES), lambda i, j: (i, 0))),
        scratch_shapes=[pltpu.VMEM((tm, D), BF16)],
        compiler_params=_params(("parallel", "arbitrary")),
        name="norm_matmul_gates",
    )(x, g, w_t, wg_t, bg)


def _norm_matmul_gates_cast(x, g, w_t, wg_t, bg, *, n, tn):
    rows = x.shape[0]
    return pl.pallas_call(
        _norm_matmul_gates_cast_kernel,
        out_shape=(jax.ShapeDtypeStruct((rows, n), BF16),
                   jax.ShapeDtypeStruct((rows, GATE_LANES), F32),
                   jax.ShapeDtypeStruct((n, D), BF16)),
        grid=(1, n // tn),
        in_specs=[pl.BlockSpec((rows, D), lambda i, j: (0, 0), pipeline_mode=pl.Buffered(1)),
                  pl.BlockSpec((1, D), lambda i, j: (0, 0)),
                  pl.BlockSpec((None, tn, D), lambda i, j: (0, j, 0)),
                  pl.BlockSpec((GATE_LANES, D), lambda i, j: (0, 0)),
                  pl.BlockSpec((1, GATE_LANES), lambda i, j: (0, 0))],
        out_specs=(pl.BlockSpec((rows, tn), lambda i, j: (0, j)),
                   pl.BlockSpec((rows, GATE_LANES), lambda i, j: (0, 0)),
                   pl.BlockSpec((tn, D), lambda i, j: (j, 0))),
        scratch_shapes=[pltpu.VMEM((rows, D), BF16)],
        compiler_params=_params(("arbitrary", "arbitrary")),
        name="norm_matmul_gates_cast",
    )(x, g, w_t, wg_t, bg)


def _matmul_norm_res_kernel(a_ref, w_ref, g_ref, h_ref, o_ref, wb_ref):
    @pl.when(pl.program_id(0) == 0)
    def _():
        wb_ref[...] = w_ref[...].astype(BF16)

    y = _dot(a_ref[...], wb_ref[...])
    o_ref[...] = h_ref[...] + _rmsnorm(y, g_ref[...])


def _matmul_norm_res(a, w, g, h, *, tm):
    rows = a.shape[0]
    return pl.pallas_call(
        _matmul_norm_res_kernel,
        out_shape=jax.ShapeDtypeStruct((rows, D), F32),
        grid=(rows // tm,),
        in_specs=[pl.BlockSpec((tm, D), lambda i: (i, 0)),
                  pl.BlockSpec((None, D, D), lambda i: (0, 0, 0), pipeline_mode=pl.Buffered(1)),
                  pl.BlockSpec((1, D), lambda i: (0, 0)),
                  pl.BlockSpec((tm, D), lambda i: (i, 0))],
        out_specs=pl.BlockSpec((tm, D), lambda i: (i, 0)),
        scratch_shapes=[pltpu.VMEM((D, D), BF16)],
        compiler_params=_params(("arbitrary",)),
        name="matmul_norm_res",
    )(a, w, g, h)


def _ffn_begin(h_ref, gpre_ref, o_ref, xn_ref):
    @pl.when(pl.program_id(1) == 0)
    def _():
        xn_ref[...] = _rmsnorm(h_ref[...], gpre_ref[...]).astype(BF16)
        o_ref[...] = jnp.zeros_like(o_ref)


def _ffn_end(h_ref, gpost_ref, o_ref):
    @pl.when(pl.program_id(1) == pl.num_programs(1) - 1)
    def _():
        o_ref[...] = h_ref[...] + _rmsnorm(o_ref[...], gpost_ref[...])


def _ffn_kernel(h_ref, gpre_ref, wup_ref, wdown_ref, gpost_ref, o_ref, xn_ref):
    f = pl.program_id(1)
    last = pl.num_programs(1) - 1
    tm = h_ref.shape[0]
    subs = [slice(r, min(r + SUB_ROWS, tm)) for r in range(0, tm, SUB_ROWS)]

    def mlp_chunk(xn):
        a = jnp.maximum(_dot(xn, wup_ref[...]), 0.0)
        return _dot((a * a).astype(BF16), wdown_ref[...])

    @pl.when(f == 0)
    def _():
        for rows in subs:
            xn = _rmsnorm(h_ref[rows, :], gpre_ref[...]).astype(BF16)
            xn_ref[rows, :] = xn
            o_ref[rows, :] = mlp_chunk(xn)

    @pl.when(jnp.logical_and(f > 0, f < last))
    def _():
        o_ref[...] += mlp_chunk(xn_ref[...])

    @pl.when(f == last)
    def _():
        for rows in subs:
            y = o_ref[rows, :] + mlp_chunk(xn_ref[rows, :])
            o_ref[rows, :] = h_ref[rows, :] + _rmsnorm(y, gpost_ref[...])


def _ffn_cast_kernel(h_ref, gpre_ref, wup_ref, wdown_ref, gpost_ref, o_ref, wupb_ref, wdownb_ref, xn_ref,
                     wup_scr, wdown_scr):
    _ffn_begin(h_ref, gpre_ref, o_ref, xn_ref)
    wup_scr[...] = wup_ref[...].astype(BF16)
    wdown_scr[...] = wdown_ref[...].astype(BF16)
    a = jnp.maximum(_dot(xn_ref[...], wup_scr[...]), 0.0)
    o_ref[...] += _dot((a * a).astype(BF16), wdown_scr[...])

    @pl.when(pl.program_id(0) == 0)
    def _():
        wupb_ref[...] = wup_scr[...]
        wdownb_ref[...] = wdown_scr[...]

    _ffn_end(h_ref, gpost_ref, o_ref)


def _ffn(h, g_pre, w_up, w_down, g_post, *, tm, tf):
    n_tiles = h.shape[0] // tm
    return pl.pallas_call(
        _ffn_kernel,
        out_shape=jax.ShapeDtypeStruct((n_tiles * tm, D), F32),
        grid=(n_tiles, D_FF // tf),
        in_specs=[pl.BlockSpec((tm, D), lambda i, f: (i, 0)),
                  pl.BlockSpec((1, D), lambda i, f: (0, 0)),
                  pl.BlockSpec((D, tf), lambda i, f: (0, f)),
                  pl.BlockSpec((tf, D), lambda i, f: (f, 0)),
                  pl.BlockSpec((1, D), lambda i, f: (0, 0))],
        out_specs=pl.BlockSpec((tm, D), lambda i, f: (i, 0)),
        scratch_shapes=[pltpu.VMEM((tm, D), BF16)],
        compiler_params=_params(("parallel", "arbitrary")),
        name="ffn",
    )(h, g_pre, w_up, w_down, g_post)


def _ffn_cast(h, g_pre, w_up, w_down, g_post, *, layer, tm, tf, n_tiles):
    n_f = D_FF // tf

    def copy_chunk(i, f):
        return jnp.where(i == 0, f, n_f - 1)

    return pl.pallas_call(
        _ffn_cast_kernel,
        out_shape=(jax.ShapeDtypeStruct((n_tiles * tm, D), F32),
                   jax.ShapeDtypeStruct((D, D_FF), BF16),
                   jax.ShapeDtypeStruct((D_FF, D), BF16)),
        grid=(n_tiles, n_f),
        in_specs=[pl.BlockSpec((tm, D), lambda i, f: (i, 0)),
                  pl.BlockSpec((1, D), lambda i, f: (0, 0)),
                  pl.BlockSpec((None, D, tf), lambda i, f: (layer, 0, f)),
                  pl.BlockSpec((None, tf, D), lambda i, f: (layer, f, 0)),
                  pl.BlockSpec((1, D), lambda i, f: (0, 0))],
        out_specs=(pl.BlockSpec((tm, D), lambda i, f: (i, 0)),
                   pl.BlockSpec((D, tf), lambda i, f: (0, copy_chunk(i, f))),
                   pl.BlockSpec((tf, D), lambda i, f: (copy_chunk(i, f), 0))),
        scratch_shapes=[pltpu.VMEM((tm, D), BF16), pltpu.VMEM((D, tf), BF16), pltpu.VMEM((tf, D), BF16)],
        compiler_params=_params(("arbitrary", "arbitrary")),
        name="ffn_cast",
    )(h, g_pre, w_up, w_down, g_post)


def _mlstm_kernel(qk_ref, v_ref, o_ref, gcol_ref, grow_ref, hg_ref, c0_ref, n0_ref, m0_ref,
                  out_ref, c_ref, n_ref, m_ref, *, seqs, length, single_chunk):
    if single_chunk:
        c_in, n_in, m_in = c0_ref, n0_ref, m0_ref
    else:
        c_in, n_in, m_in = c_ref, n_ref, m_ref

        @pl.when(pl.program_id(1) == 0)
        def _():
            c_ref[...] = c0_ref[...]
            n_ref[...] = n0_ref[...]
            m_ref[...] = m0_ref[...]

    pairs = [(s, hd) for s in range(seqs) for hd in range(HEADS)]
    L = length
    t_idx = lax.broadcasted_iota(jnp.int32, (L, L), 0)
    s_idx = lax.broadcasted_iota(jnp.int32, (L, L), 1)
    causal = s_idx <= t_idx

    def rows_of(s):
        return slice(s * L, (s + 1) * L)

    def q_of(s, hd):
        return qk_ref[rows_of(s), hd * DQK:(hd + 1) * DQK]

    def k_of(s, hd):
        return qk_ref[rows_of(s), (HEADS + hd) * DQK:(HEADS + hd + 1) * DQK]

    def v_of(s, hd):
        return v_ref[rows_of(s), hd * DV:(hd + 1) * DV]

    qk = [_dot_t(q_of(s, hd), k_of(s, hd)) for s, hd in pairs]
    qc = [_dot(q_of(s, hd), c_in[s, hd].astype(BF16)) for s, hd in pairs]

    gate_vals = []
    for s, hd in pairs:
        gcol = gcol_ref[rows_of(s), :]
        grow = grow_ref[0, s]
        li_row, lf_row = grow[hd:hd + 1, :], grow[HEADS + hd:HEADS + hd + 1, :]
        li_col, lf_col = gcol[:, hd:hd + 1], gcol[:, HEADS + hd:HEADS + hd + 1]
        m = m_in[s, hd:hd + 1, 0:1]
        b_col = jnp.sum(jnp.where(causal, lf_row, 0.0), axis=1, keepdims=True)
        b_row = jnp.sum(jnp.where(t_idx <= s_idx, lf_col, 0.0), axis=0, keepdims=True)
        b_tot = b_col[L - 1:L, :]
        dmat = jnp.where(causal, b_col - b_row + li_row, -jnp.inf)
        m_inter = b_col + m
        m_t = jnp.maximum(m_inter, jnp.max(dmat, axis=1, keepdims=True))
        p = jnp.exp(dmat - m_t)
        w_inter = jnp.exp(m_inter - m_t)
        m_new = m_t[L - 1:L, :]
        decay = jnp.exp(b_tot - b_col + li_col - m_new)
        carry = jnp.exp(b_tot + m - m_new)
        gate_vals.append((p, w_inter, m_t, m_new, decay, carry))

    scores = [qk[i] * (gate_vals[i][0] * K_SCALE) for i in range(len(pairs))]
    kd = [k_of(s, hd).astype(F32) * (gate_vals[i][4] * K_SCALE) for i, (s, hd) in enumerate(pairs)]
    sv = [_dot(scores[i].astype(BF16), v_of(s, hd)) for i, (s, hd) in enumerate(pairs)]
    kv = [lax.dot_general(kd[i].astype(BF16), v_of(s, hd), (((0,), (0,)), ((), ())),
                          preferred_element_type=F32) for i, (s, hd) in enumerate(pairs)]

    for i, (s, hd) in enumerate(pairs):
        p, w_inter, m_t, m_new, decay, carry = gate_vals[i]
        n_row = n_in[s, hd:hd + 1, :]
        c_new = carry * c_in[s, hd] + kv[i]
        n_new = carry * n_row + jnp.sum(kd[i], axis=0, keepdims=True)
        num = sv[i] + qc[i] * w_inter
        qn = jnp.sum(q_of(s, hd).astype(F32) * n_row, axis=1, keepdims=True)
        den = jnp.sum(scores[i], axis=1, keepdims=True) + w_inter * qn
        h = num / jnp.maximum(jnp.abs(den), jnp.exp(-m_t))
        c_ref[s, hd] = c_new
        n_ref[s, hd:hd + 1, :] = n_new
        m_ref[s, hd:hd + 1, :] = jnp.broadcast_to(m_new, (1, GATE_LANES))
        hn = _rmsnorm(h, hg_ref[:, hd * DV:(hd + 1) * DV])
        o_pre = o_ref[rows_of(s), hd * DV:(hd + 1) * DV].astype(F32)
        gate = 0.5 * jnp.tanh(0.5 * o_pre) + 0.5
        out_ref[rows_of(s), hd * DV:(hd + 1) * DV] = (hn * gate).astype(out_ref.dtype)


def _mlstm(proj, gcol, grow, head_g, c0, n0, m0, *, row0, n_blocks, seqs, length, n_chunks):
    assert seqs == 1 or n_chunks == 1
    blk = seqs * length
    assert row0 % blk == 0
    b0 = row0 // blk
    n_seq = n_blocks * seqs

    def rows_map(col):
        return lambda b, c: (b0 + b * n_chunks + c, col)

    kernel = functools.partial(_mlstm_kernel, seqs=seqs, length=length, single_chunk=n_chunks == 1)
    return pl.pallas_call(
        kernel,
        out_shape=(jax.ShapeDtypeStruct((n_blocks * n_chunks * blk, HEADS * DV), BF16),
                   jax.ShapeDtypeStruct((n_seq, HEADS, DQK, DV), F32),
                   jax.ShapeDtypeStruct((n_seq, HEADS, DQK), F32),
                   jax.ShapeDtypeStruct((n_seq, HEADS, GATE_LANES), F32)),
        grid=(n_blocks, n_chunks),
        in_specs=[pl.BlockSpec((blk, 2 * HEADS * DQK), rows_map(0)),
                  pl.BlockSpec((blk, HEADS * DV), rows_map(1)),
                  pl.BlockSpec((blk, HEADS * DV), rows_map(2)),
                  pl.BlockSpec((blk, GATE_LANES), rows_map(0)),
                  pl.BlockSpec((1, seqs, 2 * HEADS, length), lambda b, c: (b * n_chunks + c, 0, 0, 0)),
                  pl.BlockSpec((1, HEADS * DV), lambda b, c: (0, 0)),
                  pl.BlockSpec((seqs, HEADS, DQK, DV), lambda b, c: (b, 0, 0, 0)),
                  pl.BlockSpec((seqs, HEADS, DQK), lambda b, c: (b, 0, 0)),
                  pl.BlockSpec((seqs, HEADS, GATE_LANES), lambda b, c: (b, 0, 0))],
        out_specs=(pl.BlockSpec((blk, HEADS * DV), lambda b, c: (b * n_chunks + c, 0)),
                   pl.BlockSpec((seqs, HEADS, DQK, DV), lambda b, c: (b, 0, 0, 0)),
                   pl.BlockSpec((seqs, HEADS, DQK), lambda b, c: (b, 0, 0)),
                   pl.BlockSpec((seqs, HEADS, GATE_LANES), lambda b, c: (b, 0, 0))),
        compiler_params=_params(("parallel", "arbitrary")),
        name=f"mlstm_l{length}",
    )(proj, proj, proj, gcol, grow, head_g, c0, n0, m0)


def _pool_mix(ext, first, n_rows, pos, wg_ref, scale_ref):
    outs = []
    for g, w in enumerate(POOL_WINDOWS):
        cols = slice(g * POOL_GROUP_DIM, (g + 1) * POOL_GROUP_DIM)
        e = ext[:, cols]
        total, span = e, 1
        while span < w:
            total = total + pltpu.roll(total, span, 0)
            span *= 2
        cnt = jnp.minimum(float(w), pos + 1.0)
        pooled = total[first:first + n_rows] / cnt - e[first:first + n_rows]
        outs.append(_dot(pooled.astype(BF16), wg_ref[g].astype(BF16)))
    mixed = jnp.concatenate(outs, axis=-1) * scale_ref[...]
    return mixed.astype(BF16)


def _pool_rows_kernel(u_ref, prev_ref, first_ref, wg_ref, scale_ref, o_ref, *, tiles_per_seq, pos0):
    tr = u_ref.shape[0]
    tile = pl.program_id(0) % tiles_per_seq
    halo = jnp.where(tile == 0, first_ref[...], prev_ref[...])
    ext = jnp.concatenate([halo, u_ref[...]], axis=0)
    pos = (lax.broadcasted_iota(jnp.int32, (tr, 1), 0) + (tile * tr + pos0)).astype(F32)
    o_ref[...] = _pool_mix(ext, N_META, tr, pos, wg_ref, scale_ref)


def _pool_rows(u, first_src, w_group, scale, *, row0, n_tiles, tr, tiles_per_seq, pos0, first_block):
    hb = tr // N_META
    t0 = row0 // tr
    kernel = functools.partial(_pool_rows_kernel, tiles_per_seq=tiles_per_seq, pos0=pos0)
    return pl.pallas_call(
        kernel,
        out_shape=jax.ShapeDtypeStruct((n_tiles * tr, D), BF16),
        grid=(n_tiles,),
        in_specs=[pl.BlockSpec((tr, D), lambda i: (t0 + i, 0)),
                  pl.BlockSpec((N_META, D), lambda i: (jnp.maximum((t0 + i) * hb - 1, 0), 0)),
                  pl.BlockSpec((N_META, D), lambda i: (first_block, 0)),
                  pl.BlockSpec((None, len(POOL_WINDOWS), POOL_GROUP_DIM, POOL_GROUP_DIM),
                               lambda i: (0, 0, 0, 0)),
                  pl.BlockSpec((1, D), lambda i: (0, 0))],
        out_specs=pl.BlockSpec((tr, D), lambda i: (i, 0)),
        compiler_params=_params(("parallel",)),
        name=f"pool_rows_{tr}",
    )(u, u, first_src, w_group, scale)


def _pool_seqs_kernel(prefix_ref, u_ref, wg_ref, scale_ref, o_ref, buf_ref, *, pos0):
    g = prefix_ref.shape[0]
    n_new = u_ref.shape[0] // g
    r = 1 + POOL_BUF + n_new
    prefix = prefix_ref[...]
    u3 = u_ref[...].reshape(g, n_new, D)
    ext = jnp.concatenate([jnp.zeros((g, 1, D), F32), prefix, u3], axis=1).reshape(g * r, D)
    buf_ref[...] = jnp.concatenate([prefix, u3], axis=1)[:, n_new:, :]
    outs = []
    pos = (lax.broadcasted_iota(jnp.int32, (g, n_new, 1), 1) + pos0).astype(F32)
    for gi, w in enumerate(POOL_WINDOWS):
        cols = slice(gi * POOL_GROUP_DIM, (gi + 1) * POOL_GROUP_DIM)
        e = ext[:, cols]
        total, span = e, 1
        while span < w:
            total = total + pltpu.roll(total, span, 0)
            span *= 2
        cnt = jnp.minimum(float(w), pos + 1.0)
        tot3 = total.reshape(g, r, POOL_GROUP_DIM)[:, 1 + POOL_BUF:, :]
        e3 = e.reshape(g, r, POOL_GROUP_DIM)[:, 1 + POOL_BUF:, :]
        pooled = (tot3 / cnt - e3).reshape(g * n_new, POOL_GROUP_DIM)
        outs.append(_dot(pooled.astype(BF16), wg_ref[gi].astype(BF16)))
    mixed = jnp.concatenate(outs, axis=-1) * scale_ref[...]
    o_ref[...] = mixed.astype(BF16)


def _pool_seqs(prefix, u, w_group, scale, *, n_new, seqs_per_block, pos0):
    n_seq = prefix.shape[1]
    kernel = functools.partial(_pool_seqs_kernel, pos0=pos0)
    return pl.pallas_call(
        kernel,
        out_shape=(jax.ShapeDtypeStruct((n_seq * n_new, D), BF16),
                   jax.ShapeDtypeStruct((n_seq, POOL_BUF, D), F32)),
        grid=(n_seq // seqs_per_block,),
        in_specs=[pl.BlockSpec((None, seqs_per_block, POOL_BUF, D), lambda i: (0, i, 0, 0)),
                  pl.BlockSpec((seqs_per_block * n_new, D), lambda i: (i, 0)),
                  pl.BlockSpec((None, len(POOL_WINDOWS), POOL_GROUP_DIM, POOL_GROUP_DIM),
                               lambda i: (0, 0, 0, 0)),
                  pl.BlockSpec((1, D), lambda i: (0, 0))],
        out_specs=(pl.BlockSpec((seqs_per_block * n_new, D), lambda i: (i, 0)),
                   pl.BlockSpec((seqs_per_block, POOL_BUF, D), lambda i: (i, 0, 0))),
        compiler_params=_params(("parallel",)),
        name="pool_seqs",
    )(prefix, u, w_group, scale)


TM_PROJ = 1024
TN_PROJ = 1024
TM = 512
TM_AUX = 544
TM_FFN = 512
TF = 1024
TF_AUX = 512
CHUNK = 512
SAMPLE_SEQS = 4
POOL_TR = 512
POOL_SEQS = 16


def kernel(x_prompt, x_sample, state_mlstm_c, state_mlstm_n, state_mlstm_m, state_pool, meta_tokens,
           norm_mix_pre, norm_mix_post, norm_ffn_pre, norm_ffn_post, mlstm_w_in, mlstm_b_i, mlstm_b_f,
           mlstm_head_norm, mlstm_w_out, pool_w_in, pool_w_group, pool_scale, pool_w_out,
           ffn_w_up, ffn_w_down):
    B, S, _ = x_prompt.shape
    DB, DS, _ = x_sample.shape
    n_p, n_s = B * S, DB * DS
    rows_a = 2 * TM_AUX
    row_m = n_s
    assert n_s + N_META <= rows_a and n_p % TM_PROJ == 0 and n_p % TM == 0
    assert S % CHUNK == 0 and S % POOL_TR == 0 and row_m % N_META == 0

    h_p = x_prompt.reshape(n_p, D)
    h_a = jnp.concatenate([x_sample.reshape(n_s, D), meta_tokens.astype(F32),
                           jnp.zeros((rows_a - n_s - N_META, D), F32)], axis=0)
    pad_a = jnp.zeros((rows_a - n_s - N_META, D), BF16)

    def gain(a):
        return a.reshape(1, D).astype(F32)


    def matmul_norm_res(a_p, a_a, w, g):
        return _matmul_norm_res(a_p, w, g, h_p, tm=TM), _matmul_norm_res(a_a, w, g, h_a, tm=TM_AUX)

    def ffn(layer):
        g_pre, g_post = gain(norm_ffn_pre[layer]), gain(norm_ffn_post[layer])
        out_a, w_up, w_down = _ffn_cast(h_a, g_pre, ffn_w_up.astype(F32), ffn_w_down.astype(F32), g_post,
                                        layer=layer, tm=TM_AUX, tf=TF_AUX, n_tiles=rows_a // TM_AUX)
        return _ffn(h_p, g_pre, w_up, w_down, g_post, tm=TM_FFN, tf=TF), out_a

    w_in_t = jnp.swapaxes(mlstm_w_in[:1].astype(F32), 1, 2)
    w_gate_t = jnp.pad(w_in_t[0, PROJ_MAIN:], ((0, GATE_LANES - 2 * HEADS), (0, 0))).astype(BF16)
    b_gate = jnp.pad(jnp.concatenate([mlstm_b_i[0], mlstm_b_f[0]]).astype(F32),
                     (0, GATE_LANES - 2 * HEADS)).reshape(1, GATE_LANES)
    g_mix = gain(norm_mix_pre[0])
    proj_a, gcol_a, w_in_tb = _norm_matmul_gates_cast(h_a, g_mix, w_in_t, w_gate_t, b_gate, n=PROJ_MAIN,
                                                      tn=TN_PROJ)
    proj_p, gcol_p = _norm_matmul_gates(h_p, g_mix, w_in_tb, w_gate_t, b_gate, tm=TM_PROJ, tn=TN_PROJ)

    def gate_rows(gcol, r0, n_blk, seqs, length):
        g8 = gcol[r0:r0 + n_blk * seqs * length, :2 * HEADS]
        return g8.reshape(n_blk, seqs, length, 2 * HEADS).transpose(0, 1, 3, 2)

    head_g = mlstm_head_norm[0].reshape(1, HEADS * DV).astype(F32)
    zc = jnp.zeros((1, HEADS, DQK, DV), F32)
    zn = jnp.zeros((1, HEADS, DQK), F32)
    zm = jnp.zeros((1, HEADS, GATE_LANES), F32)
    mix_m, c_m, n_m, m_m = _mlstm(proj_a, gcol_a, gate_rows(gcol_a, row_m, 1, 1, N_META), head_g, zc, zn, zm,
                                  row0=row_m, n_blocks=1, seqs=1, length=N_META, n_chunks=1)
    mix_p, c_p, n_p_, m_p = _mlstm(proj_p, gcol_p, gate_rows(gcol_p, 0, B * (S // CHUNK), 1, CHUNK), head_g,
                                   jnp.broadcast_to(c_m, (B,) + c_m.shape[1:]),
                                   jnp.broadcast_to(n_m, (B,) + n_m.shape[1:]),
                                   jnp.broadcast_to(m_m, (B,) + m_m.shape[1:]),
                                   row0=0, n_blocks=B, seqs=1, length=CHUNK, n_chunks=S // CHUNK)
    m0_s = jnp.broadcast_to(state_mlstm_m[0].astype(F32)[:, :, None], (DB, HEADS, GATE_LANES))
    mix_s, c_s, n_s_, m_s = _mlstm(proj_a, gcol_a, gate_rows(gcol_a, 0, DB // SAMPLE_SEQS, SAMPLE_SEQS, DS),
                                   head_g, state_mlstm_c[0].astype(F32), state_mlstm_n[0].astype(F32), m0_s,
                                   row0=0, n_blocks=DB // SAMPLE_SEQS, seqs=SAMPLE_SEQS, length=DS, n_chunks=1)
    mix_a = jnp.concatenate([mix_s, mix_m, pad_a], axis=0)
    h_p, h_a = matmul_norm_res(mix_p, mix_a, mlstm_w_out[:1].astype(F32), gain(norm_mix_post[0]))
    h_p, h_a = ffn(0)

    w_pool_in, g_mix = pool_w_in[:1].astype(F32), gain(norm_mix_pre[1])
    u_p = _norm_matmul(h_p, g_mix, w_pool_in, tm=TM, out_dtype=F32)
    u_a = _norm_matmul(h_a, g_mix, w_pool_in, tm=TM_AUX, out_dtype=F32)
    w_group = pool_w_group[:1].astype(F32)
    p_scale = pool_scale[0].reshape(1, D).astype(F32)
    pmix_m = _pool_rows(u_a, jnp.zeros((N_META, D), F32), w_group, p_scale, row0=row_m, n_tiles=1,
                        tr=N_META, tiles_per_seq=1, pos0=0, first_block=0)
    pmix_p = _pool_rows(u_p, u_a, w_group, p_scale, row0=0, n_tiles=n_p // POOL_TR, tr=POOL_TR,
                        tiles_per_seq=S // POOL_TR, pos0=N_META, first_block=row_m // N_META)
    pmix_s, pool_s = _pool_seqs(state_pool[:1].astype(F32), u_a, w_group, p_scale, n_new=DS,
                                seqs_per_block=POOL_SEQS, pos0=PAST_LEN)
    pmix_a = jnp.concatenate([pmix_s, pmix_m, pad_a], axis=0)
    h_p, h_a = matmul_norm_res(pmix_p, pmix_a, pool_w_out[:1].astype(F32), gain(norm_mix_post[1]))
    h_p, h_a = ffn(1)

    y_prompt = h_p.reshape(B, S, D)
    y_sample = h_a[:n_s].reshape(DB, DS, D)
    dt_c, dt_n, dt_m, dt_pool = state_mlstm_c.dtype, state_mlstm_n.dtype, state_mlstm_m.dtype, state_pool.dtype
    u_prompt = u_p.reshape(B, S, D)
    return (y_prompt, y_sample,
            c_p[None].astype(dt_c), n_p_[None].astype(dt_n), m_p[None, :, :, 0].astype(dt_m),
            u_prompt[None, :, S - POOL_BUF:].astype(dt_pool),
            c_s[None].astype(dt_c), n_s_[None].astype(dt_n), m_s[None, :, :, 0].astype(dt_m),
            pool_s[None].astype(dt_pool))
```

```python
import functools

import jax
import jax.numpy as jnp
from jax import lax
from jax.experimental import pallas as pl
from jax.experimental.pallas import tpu as pltpu

D = 2048
N_META = 16
HEADS = 4
DQK = 256
DV = 512
D_FF = 4 * D
POOL_WINDOWS = (2, 4, 8, 16)
POOL_GROUP_DIM = D // len(POOL_WINDOWS)
POOL_BUF = 15
PAST_LEN = 16384
EPS = 1e-6
K_SCALE = DQK ** -0.5
PROJ_MAIN = 2 * HEADS * DQK + 2 * HEADS * DV
GATE_LANES = 128

VMEM_LIMIT = 56 * 1024 * 1024

F32 = jnp.float32
BF16 = jnp.bfloat16


def _params(semantics):
    return pltpu.CompilerParams(dimension_semantics=semantics, vmem_limit_bytes=VMEM_LIMIT)


def _rmsnorm(x, g):
    return x * lax.rsqrt(jnp.mean(x * x, axis=-1, keepdims=True) + EPS) * g


def _dot(a, b):
    return jnp.dot(a, b, preferred_element_type=F32)


SUB_ROWS = 256
CAST_SUB_ROWS = 272


def _norm_matmul_kernel(x_ref, g_ref, w_ref, o_ref, wb_ref):
    @pl.when(pl.program_id(0) == 0)
    def _():
        wb_ref[...] = w_ref[...].astype(BF16)

    for r in range(0, x_ref.shape[0], SUB_ROWS):
        rows = slice(r, min(r + SUB_ROWS, x_ref.shape[0]))
        xn = _rmsnorm(x_ref[rows, :], g_ref[...]).astype(BF16)
        o_ref[rows, :] = _dot(xn, wb_ref[...]).astype(o_ref.dtype)


def _dot_t(a, b):
    return lax.dot_general(a, b, (((1,), (1,)), ((), ())), preferred_element_type=F32)


def _in_proj_first_block(x_ref, g_ref, w_t, wgt_ref, bg_ref, o_ref, gate_ref, xn_ref):
    tm = x_ref.shape[0]
    for r in range(0, tm, SUB_ROWS):
        rows = slice(r, min(r + SUB_ROWS, tm))
        xn = _rmsnorm(x_ref[rows, :], g_ref[...]).astype(BF16)
        xn_ref[rows, :] = xn
        z = _dot_t(xn, wgt_ref[...]) + bg_ref[...]
        lane = lax.broadcasted_iota(jnp.int32, z.shape, 1)
        log_sig = jnp.minimum(z, 0.0) - jnp.log1p(jnp.exp(-jnp.abs(z)))
        gate_ref[rows, :] = jnp.where(lane >= HEADS, log_sig, z)
        o_ref[rows, :] = _dot_t(xn, w_t).astype(o_ref.dtype)


def _norm_matmul_gates_kernel(x_ref, g_ref, wt_ref, wgt_ref, bg_ref, o_ref, gate_ref, xn_ref):
    @pl.when(pl.program_id(1) == 0)
    def _():
        _in_proj_first_block(x_ref, g_ref, wt_ref[...], wgt_ref, bg_ref, o_ref, gate_ref, xn_ref)

    @pl.when(pl.program_id(1) > 0)
    def _():
        o_ref[...] = _dot_t(xn_ref[...], wt_ref[...]).astype(o_ref.dtype)


def _norm_matmul_gates_cast_kernel(x_ref, g_ref, wt_ref, wgt_ref, bg_ref, o_ref, gate_ref, wtb_ref, xn_ref):
    wtb_ref[...] = wt_ref[...].astype(BF16)

    @pl.when(pl.program_id(1) == 0)
    def _():
        _in_proj_first_block(x_ref, g_ref, wtb_ref[...], wgt_ref, bg_ref, o_ref, gate_ref, xn_ref)

    @pl.when(pl.program_id(1) > 0)
    def _():
        o_ref[...] = _dot_t(xn_ref[...], wtb_ref[...]).astype(o_ref.dtype)


def _norm_matmul(x, g, w, *, tm, out_dtype):
    rows = x.shape[0]
    return pl.pallas_call(
        _norm_matmul_kernel,
        out_shape=jax.ShapeDtypeStruct((rows, D), out_dtype),
        grid=(rows // tm,),
        in_specs=[pl.BlockSpec((tm, D), lambda i: (i, 0)),
                  pl.BlockSpec((1, D), lambda i: (0, 0)),
                  pl.BlockSpec((None, D, D), lambda i: (0, 0, 0), pipeline_mode=pl.Buffered(1))],
        out_specs=pl.BlockSpec((tm, D), lambda i: (i, 0)),
        scratch_shapes=[pltpu.VMEM((D, D), BF16)],
        compiler_params=_params(("arbitrary",)),
        name="norm_matmul",
    )(x, g, w)


def _norm_matmul_gates(x, g, w_t, wg_t, bg, *, tm, tn):
    rows, n = x.shape[0], w_t.shape[0]
    return pl.pallas_call(
        _norm_matmul_gates_kernel,
        out_shape=(jax.ShapeDtypeStruct((rows, n), BF16),
                   jax.ShapeDtypeStruct((rows, GATE_LANES), F32)),
        grid=(rows // tm, n // tn),
        in_specs=[pl.BlockSpec((tm, D), lambda i, j: (i, 0)),
                  pl.BlockSpec((1, D), lambda i, j: (0, 0)),
                  pl.BlockSpec((tn, D), lambda i, j: (j, 0)),
                  pl.BlockSpec((GATE_LANES, D), lambda i, j: (0, 0)),
                  pl.BlockSpec((1, GATE_LANES), lambda i, j: (0, 0))],
        out_specs=(pl.BlockSpec((tm, tn), lambda i, j: (i, j)),
                   pl.BlockSpec((tm, GATE_LANES), lambda i, j: (i, 0))),
        scratch_shapes=[pltpu.VMEM((tm, D), BF16)],
        compiler_params=_params(("parallel", "arbitrary")),
        name="norm_matmul_gates",
    )(x, g, w_t, wg_t, bg)


def _norm_matmul_gates_cast(x, g, w_t, wg_t, bg, *, n, tn):
    rows = x.shape[0]
    return pl.pallas_call(
        _norm_matmul_gates_cast_kernel,
        out_shape=(jax.ShapeDtypeStruct((rows, n), BF16),
                   jax.ShapeDtypeStruct((rows, GATE_LANES), F32),
                   jax.ShapeDtypeStruct((n, D), BF16)),
        grid=(1, n // tn),
        in_specs=[pl.BlockSpec((rows, D), lambda i, j: (0, 0), pipeline_mode=pl.Buffered(1)),
                  pl.BlockSpec((1, D), lambda i, j: (0, 0)),
                  pl.BlockSpec((None, tn, D), lambda i, j: (0, j, 0)),
                  pl.BlockSpec((GATE_LANES, D), lambda i, j: (0, 0)),
                  pl.BlockSpec((1, GATE_LANES), lambda i, j: (0, 0))],
        out_specs=(pl.BlockSpec((rows, tn), lambda i, j: (0, j)),
                   pl.BlockSpec((rows, GATE_LANES), lambda i, j: (0, 0)),
                   pl.BlockSpec((tn, D), lambda i, j: (j, 0))),
        scratch_shapes=[pltpu.VMEM((rows, D), BF16)],
        compiler_params=_params(("arbitrary", "arbitrary")),
        name="norm_matmul_gates_cast",
    )(x, g, w_t, wg_t, bg)


def _matmul_norm_res_kernel(a_ref, w_ref, g_ref, h_ref, o_ref, wb_ref):
    @pl.when(pl.program_id(0) == 0)
    def _():
        wb_ref[...] = w_ref[...].astype(BF16)

    y = _dot(a_ref[...], wb_ref[...])
    o_ref[...] = h_ref[...] + _rmsnorm(y, g_ref[...])


def _matmul_norm_res(a, w, g, h, *, tm):
    rows = a.shape[0]
    return pl.pallas_call(
        _matmul_norm_res_kernel,
        out_shape=jax.ShapeDtypeStruct((rows, D), F32),
        grid=(rows // tm,),
        in_specs=[pl.BlockSpec((tm, D), lambda i: (i, 0)),
                  pl.BlockSpec((None, D, D), lambda i: (0, 0, 0), pipeline_mode=pl.Buffered(1)),
                  pl.BlockSpec((1, D), lambda i: (0, 0)),
                  pl.BlockSpec((tm, D), lambda i: (i, 0))],
        out_specs=pl.BlockSpec((tm, D), lambda i: (i, 0)),
        scratch_shapes=[pltpu.VMEM((D, D), BF16)],
        compiler_params=_params(("arbitrary",)),
        name="matmul_norm_res",
    )(a, w, g, h)


def _ffn_chunk(f, last, h_ref, gpre_ref, wup_ref, wdown_ref, gpost_ref, o_ref, xn_ref, *, sub, sub_middle):
    tm = h_ref.shape[0]
    subs = [slice(r, min(r + sub, tm)) for r in range(0, tm, sub)]

    def mlp_chunk(xn):
        a = jnp.maximum(_dot(xn, wup_ref[...]), 0.0)
        return _dot((a * a).astype(BF16), wdown_ref[...])

    @pl.when(f == 0)
    def _():
        for rows in subs:
            xn = _rmsnorm(h_ref[rows, :], gpre_ref[...]).astype(BF16)
            xn_ref[rows, :] = xn
            o_ref[rows, :] = mlp_chunk(xn)

    @pl.when(jnp.logical_and(f > 0, f < last))
    def _():
        for rows in (subs if sub_middle else [slice(None)]):
            o_ref[rows, :] += mlp_chunk(xn_ref[rows, :])

    @pl.when(f == last)
    def _():
        for rows in subs:
            y = o_ref[rows, :] + mlp_chunk(xn_ref[rows, :])
            o_ref[rows, :] = h_ref[rows, :] + _rmsnorm(y, gpost_ref[...])


def _ffn_kernel(h_ref, gpre_ref, wup_ref, wdown_ref, gpost_ref, o_ref, xn_ref):
    _ffn_chunk(pl.program_id(1), pl.num_programs(1) - 1, h_ref, gpre_ref, wup_ref, wdown_ref, gpost_ref,
               o_ref, xn_ref, sub=SUB_ROWS, sub_middle=False)


def _ffn_cast_kernel(h_ref, gpre_ref, wup_ref, wdown_ref, gpost_ref, o_ref, wupb_ref, wdownb_ref, xn_ref):
    wupb_ref[...] = wup_ref[...].astype(BF16)
    wdownb_ref[...] = wdown_ref[...].astype(BF16)
    _ffn_chunk(pl.program_id(0), pl.num_programs(0) - 1, h_ref, gpre_ref, wupb_ref, wdownb_ref, gpost_ref,
               o_ref, xn_ref, sub=CAST_SUB_ROWS, sub_middle=True)


def _ffn(h, g_pre, w_up, w_down, g_post, *, tm, tf):
    n_tiles = h.shape[0] // tm
    return pl.pallas_call(
        _ffn_kernel,
        out_shape=jax.ShapeDtypeStruct((n_tiles * tm, D), F32),
        grid=(n_tiles, D_FF // tf),
        in_specs=[pl.BlockSpec((tm, D), lambda i, f: (i, 0)),
                  pl.BlockSpec((1, D), lambda i, f: (0, 0)),
                  pl.BlockSpec((D, tf), lambda i, f: (0, f)),
                  pl.BlockSpec((tf, D), lambda i, f: (f, 0)),
                  pl.BlockSpec((1, D), lambda i, f: (0, 0))],
        out_specs=pl.BlockSpec((tm, D), lambda i, f: (i, 0)),
        scratch_shapes=[pltpu.VMEM((tm, D), BF16)],
        compiler_params=_params(("parallel", "arbitrary")),
        name="ffn",
    )(h, g_pre, w_up, w_down, g_post)


def _ffn_cast(h, g_pre, w_up, w_down, g_post, *, layer, tf):
    rows = h.shape[0]
    single = pl.Buffered(1)
    return pl.pallas_call(
        _ffn_cast_kernel,
        out_shape=(jax.ShapeDtypeStruct((rows, D), F32),
                   jax.ShapeDtypeStruct((D, D_FF), BF16),
                   jax.ShapeDtypeStruct((D_FF, D), BF16)),
        grid=(D_FF // tf,),
        in_specs=[pl.BlockSpec((rows, D), lambda f: (0, 0), pipeline_mode=single),
                  pl.BlockSpec((1, D), lambda f: (0, 0)),
                  pl.BlockSpec((None, D, tf), lambda f: (layer, 0, f)),
                  pl.BlockSpec((None, tf, D), lambda f: (layer, f, 0)),
                  pl.BlockSpec((1, D), lambda f: (0, 0))],
        out_specs=(pl.BlockSpec((rows, D), lambda f: (0, 0), pipeline_mode=single),
                   pl.BlockSpec((D, tf), lambda f: (0, f)),
                   pl.BlockSpec((tf, D), lambda f: (f, 0))),
        scratch_shapes=[pltpu.VMEM((rows, D), BF16)],
        compiler_params=_params(("arbitrary",)),
        name="ffn_cast",
    )(h, g_pre, w_up, w_down, g_post)


def _mlstm_kernel(qk_ref, v_ref, o_ref, gcol_ref, grow_ref, hg_ref, c0_ref, n0_ref, m0_ref,
                  out_ref, c_ref, n_ref, m_ref, *, seqs, length, single_chunk):
    if single_chunk:
        c_in, n_in, m_in = c0_ref, n0_ref, m0_ref
    else:
        c_in, n_in, m_in = c_ref, n_ref, m_ref

        @pl.when(pl.program_id(1) == 0)
        def _():
            c_ref[...] = c0_ref[...]
            n_ref[...] = n0_ref[...]
            m_ref[...] = m0_ref[...]

    pairs = [(s, hd) for s in range(seqs) for hd in range(HEADS)]
    L = length
    t_idx = lax.broadcasted_iota(jnp.int32, (L, L), 0)
    s_idx = lax.broadcasted_iota(jnp.int32, (L, L), 1)
    causal = s_idx <= t_idx

    def rows_of(s):
        return slice(s * L, (s + 1) * L)

    def q_of(s, hd):
        return qk_ref[rows_of(s), hd * DQK:(hd + 1) * DQK]

    def k_of(s, hd):
        return qk_ref[rows_of(s), (HEADS + hd) * DQK:(HEADS + hd + 1) * DQK]

    def v_of(s, hd):
        return v_ref[rows_of(s), hd * DV:(hd + 1) * DV]

    qk = [_dot_t(q_of(s, hd), k_of(s, hd)) for s, hd in pairs]
    qc = [_dot(q_of(s, hd), c_in[s, hd].astype(BF16)) for s, hd in pairs]

    gate_vals = []
    for s, hd in pairs:
        gcol = gcol_ref[rows_of(s), :]
        grow = grow_ref[0, s]
        li_row, lf_row = grow[hd:hd + 1, :], grow[HEADS + hd:HEADS + hd + 1, :]
        li_col, lf_col = gcol[:, hd:hd + 1], gcol[:, HEADS + hd:HEADS + hd + 1]
        m = m_in[s, hd:hd + 1, 0:1]
        b_col = jnp.sum(jnp.where(causal, lf_row, 0.0), axis=1, keepdims=True)
        b_row = jnp.sum(jnp.where(t_idx <= s_idx, lf_col, 0.0), axis=0, keepdims=True)
        b_tot = b_col[L - 1:L, :]
        dmat = jnp.where(causal, b_col - b_row + li_row, -jnp.inf)
        m_inter = b_col + m
        m_t = jnp.maximum(m_inter, jnp.max(dmat, axis=1, keepdims=True))
        p = jnp.exp(dmat - m_t)
        w_inter = jnp.exp(m_inter - m_t)
        m_new = m_t[L - 1:L, :]
        decay = jnp.exp(b_tot - b_col + li_col - m_new)
        carry = jnp.exp(b_tot + m - m_new)
        gate_vals.append((p, w_inter, m_t, m_new, decay, carry))

    scores = [qk[i] * (gate_vals[i][0] * K_SCALE) for i in range(len(pairs))]
    kd = [k_of(s, hd).astype(F32) * (gate_vals[i][4] * K_SCALE) for i, (s, hd) in enumerate(pairs)]
    sv = [_dot(scores[i].astype(BF16), v_of(s, hd)) for i, (s, hd) in enumerate(pairs)]
    kv = [lax.dot_general(kd[i].astype(BF16), v_of(s, hd), (((0,), (0,)), ((), ())),
                          preferred_element_type=F32) for i, (s, hd) in enumerate(pairs)]

    for i, (s, hd) in enumerate(pairs):
        p, w_inter, m_t, m_new, decay, carry = gate_vals[i]
        n_row = n_in[s, hd:hd + 1, :]
        c_new = carry * c_in[s, hd] + kv[i]
        n_new = carry * n_row + jnp.sum(kd[i], axis=0, keepdims=True)
        num = sv[i] + qc[i] * w_inter
        qn = jnp.sum(q_of(s, hd).astype(F32) * n_row, axis=1, keepdims=True)
        den = jnp.sum(scores[i], axis=1, keepdims=True) + w_inter * qn
        h = num / jnp.maximum(jnp.abs(den), jnp.exp(-m_t))
        c_ref[s, hd] = c_new
        n_ref[s, hd:hd + 1, :] = n_new
        m_ref[s, hd:hd + 1, :] = jnp.broadcast_to(m_new, (1, GATE_LANES))
        hn = _rmsnorm(h, hg_ref[:, hd * DV:(hd + 1) * DV])
        o_pre = o_ref[rows_of(s), hd * DV:(hd + 1) * DV].astype(F32)
        gate = 0.5 * jnp.tanh(0.5 * o_pre) + 0.5
        out_ref[rows_of(s), hd * DV:(hd + 1) * DV] = (hn * gate).astype(out_ref.dtype)


def _mlstm(proj, gcol, grow, head_g, c0, n0, m0, *, row0, n_blocks, seqs, length, n_chunks):
    assert seqs == 1 or n_chunks == 1
    blk = seqs * length
    assert row0 % blk == 0
    b0 = row0 // blk
    n_seq = n_blocks * seqs

    def rows_map(col):
        return lambda b, c: (b0 + b * n_chunks + c, col)

    kernel = functools.partial(_mlstm_kernel, seqs=seqs, length=length, single_chunk=n_chunks == 1)
    return pl.pallas_call(
        kernel,
        out_shape=(jax.ShapeDtypeStruct((n_blocks * n_chunks * blk, HEADS * DV), BF16),
                   jax.ShapeDtypeStruct((n_seq, HEADS, DQK, DV), F32),
                   jax.ShapeDtypeStruct((n_seq, HEADS, DQK), F32),
                   jax.ShapeDtypeStruct((n_seq, HEADS, GATE_LANES), F32)),
        grid=(n_blocks, n_chunks),
        in_specs=[pl.BlockSpec((blk, 2 * HEADS * DQK), rows_map(0)),
                  pl.BlockSpec((blk, HEADS * DV), rows_map(1)),
                  pl.BlockSpec((blk, HEADS * DV), rows_map(2)),
                  pl.BlockSpec((blk, GATE_LANES), rows_map(0)),
                  pl.BlockSpec((1, seqs, 2 * HEADS, length), lambda b, c: (b * n_chunks + c, 0, 0, 0)),
                  pl.BlockSpec((1, HEADS * DV), lambda b, c: (0, 0)),
                  pl.BlockSpec((seqs, HEADS, DQK, DV), lambda b, c: (b, 0, 0, 0)),
                  pl.BlockSpec((seqs, HEADS, DQK), lambda b, c: (b, 0, 0)),
                  pl.BlockSpec((seqs, HEADS, GATE_LANES), lambda b, c: (b, 0, 0))],
        out_specs=(pl.BlockSpec((blk, HEADS * DV), lambda b, c: (b * n_chunks + c, 0)),
                   pl.BlockSpec((seqs, HEADS, DQK, DV), lambda b, c: (b, 0, 0, 0)),
                   pl.BlockSpec((seqs, HEADS, DQK), lambda b, c: (b, 0, 0)),
                   pl.BlockSpec((seqs, HEADS, GATE_LANES), lambda b, c: (b, 0, 0))),
        compiler_params=_params(("parallel", "arbitrary")),
        name=f"mlstm_l{length}",
    )(proj, proj, proj, gcol, grow, head_g, c0, n0, m0)


def _pool_mix(ext, first, n_rows, pos, wg_ref, scale_ref):
    outs = []
    for g, w in enumerate(POOL_WINDOWS):
        cols = slice(g * POOL_GROUP_DIM, (g + 1) * POOL_GROUP_DIM)
        e = ext[:, cols]
        total, span = e, 1
        while span < w:
            total = total + pltpu.roll(total, span, 0)
            span *= 2
        cnt = jnp.minimum(float(w), pos + 1.0)
        pooled = total[first:first + n_rows] / cnt - e[first:first + n_rows]
        outs.append(_dot(pooled.astype(BF16), wg_ref[g].astype(BF16)))
    mixed = jnp.concatenate(outs, axis=-1) * scale_ref[...]
    return mixed.astype(BF16)


def _pool_rows_kernel(u_ref, prev_ref, first_ref, wg_ref, scale_ref, o_ref, *, tiles_per_seq, pos0):
    tr = u_ref.shape[0]
    tile = pl.program_id(0) % tiles_per_seq
    halo = jnp.where(tile == 0, first_ref[...], prev_ref[...])
    ext = jnp.concatenate([halo, u_ref[...]], axis=0)
    pos = (lax.broadcasted_iota(jnp.int32, (tr, 1), 0) + (tile * tr + pos0)).astype(F32)
    o_ref[...] = _pool_mix(ext, N_META, tr, pos, wg_ref, scale_ref)


def _pool_rows(u, first_src, w_group, scale, *, row0, n_tiles, tr, tiles_per_seq, pos0, first_block):
    hb = tr // N_META
    t0 = row0 // tr
    kernel = functools.partial(_pool_rows_kernel, tiles_per_seq=tiles_per_seq, pos0=pos0)
    return pl.pallas_call(
        kernel,
        out_shape=jax.ShapeDtypeStruct((n_tiles * tr, D), BF16),
        grid=(n_tiles,),
        in_specs=[pl.BlockSpec((tr, D), lambda i: (t0 + i, 0)),
                  pl.BlockSpec((N_META, D), lambda i: (jnp.maximum((t0 + i) * hb - 1, 0), 0)),
                  pl.BlockSpec((N_META, D), lambda i: (first_block, 0)),
                  pl.BlockSpec((None, len(POOL_WINDOWS), POOL_GROUP_DIM, POOL_GROUP_DIM),
                               lambda i: (0, 0, 0, 0)),
                  pl.BlockSpec((1, D), lambda i: (0, 0))],
        out_specs=pl.BlockSpec((tr, D), lambda i: (i, 0)),
        compiler_params=_params(("parallel",)),
        name=f"pool_rows_{tr}",
    )(u, u, first_src, w_group, scale)


def _pool_seqs_kernel(prefix_ref, u_ref, wg_ref, scale_ref, o_ref, buf_ref, *, pos0):
    g = prefix_ref.shape[0]
    n_new = u_ref.shape[0] // g
    r = 1 + POOL_BUF + n_new
    prefix = prefix_ref[...]
    u3 = u_ref[...].reshape(g, n_new, D)
    ext = jnp.concatenate([jnp.zeros((g, 1, D), F32), prefix, u3], axis=1).reshape(g * r, D)
    buf_ref[...] = jnp.concatenate([prefix, u3], axis=1)[:, n_new:, :]
    outs = []
    pos = (lax.broadcasted_iota(jnp.int32, (g, n_new, 1), 1) + pos0).astype(F32)
    for gi, w in enumerate(POOL_WINDOWS):
        cols = slice(gi * POOL_GROUP_DIM, (gi + 1) * POOL_GROUP_DIM)
        e = ext[:, cols]
        total, span = e, 1
        while span < w:
            total = total + pltpu.roll(total, span, 0)
            span *= 2
        cnt = jnp.minimum(float(w), pos + 1.0)
        tot3 = total.reshape(g, r, POOL_GROUP_DIM)[:, 1 + POOL_BUF:, :]
        e3 = e.reshape(g, r, POOL_GROUP_DIM)[:, 1 + POOL_BUF:, :]
        pooled = (tot3 / cnt - e3).reshape(g * n_new, POOL_GROUP_DIM)
        outs.append(_dot(pooled.astype(BF16), wg_ref[gi].astype(BF16)))
    mixed = jnp.concatenate(outs, axis=-1) * scale_ref[...]
    o_ref[...] = mixed.astype(BF16)


def _pool_seqs(prefix, u, w_group, scale, *, n_new, seqs_per_block, pos0):
    n_seq = prefix.shape[1]
    kernel = functools.partial(_pool_seqs_kernel, pos0=pos0)
    return pl.pallas_call(
        kernel,
        out_shape=(jax.ShapeDtypeStruct((n_seq * n_new, D), BF16),
                   jax.ShapeDtypeStruct((n_seq, POOL_BUF, D), F32)),
        grid=(n_seq // seqs_per_block,),
        in_specs=[pl.BlockSpec((None, seqs_per_block, POOL_BUF, D), lambda i: (0, i, 0, 0)),
                  pl.BlockSpec((seqs_per_block * n_new, D), lambda i: (i, 0)),
                  pl.BlockSpec((None, len(POOL_WINDOWS), POOL_GROUP_DIM, POOL_GROUP_DIM),
                               lambda i: (0, 0, 0, 0)),
                  pl.BlockSpec((1, D), lambda i: (0, 0))],
        out_specs=(pl.BlockSpec((seqs_per_block * n_new, D), lambda i: (i, 0)),
                   pl.BlockSpec((seqs_per_block, POOL_BUF, D), lambda i: (i, 0, 0))),
        compiler_params=_params(("parallel",)),
        name="pool_seqs",
    )(prefix, u, w_group, scale)


TM_PROJ = 1024
TN_PROJ = 1024
TM = 512
TM_AUX = 544
TM_FFN = 512
TF = 1024
TF_AUX = 512
CHUNK = 512
SAMPLE_SEQS = 4
POOL_TR = 512
POOL_SEQS = 16


def kernel(x_prompt, x_sample, state_mlstm_c, state_mlstm_n, state_mlstm_m, state_pool, meta_tokens,
           norm_mix_pre, norm_mix_post, norm_ffn_pre, norm_ffn_post, mlstm_w_in, mlstm_b_i, mlstm_b_f,
           mlstm_head_norm, mlstm_w_out, pool_w_in, pool_w_group, pool_scale, pool_w_out,
           ffn_w_up, ffn_w_down):
    B, S, _ = x_prompt.shape
    DB, DS, _ = x_sample.shape
    n_p, n_s = B * S, DB * DS
    rows_a = 2 * TM_AUX
    row_m = n_s
    assert n_s + N_META <= rows_a and n_p % TM_PROJ == 0 and n_p % TM == 0
    assert S % CHUNK == 0 and S % POOL_TR == 0 and row_m % N_META == 0

    h_p = x_prompt.reshape(n_p, D)
    h_a = jnp.concatenate([x_sample.reshape(n_s, D), meta_tokens.astype(F32),
                           jnp.zeros((rows_a - n_s - N_META, D), F32)], axis=0)
    pad_a = jnp.zeros((rows_a - n_s - N_META, D), BF16)

    def gain(a):
        return a.reshape(1, D).astype(F32)


    def matmul_norm_res(a_p, a_a, w, g):
        return _matmul_norm_res(a_p, w, g, h_p, tm=TM), _matmul_norm_res(a_a, w, g, h_a, tm=TM_AUX)

    def ffn(layer):
        g_pre, g_post = gain(norm_ffn_pre[layer]), gain(norm_ffn_post[layer])
        out_a, w_up, w_down = _ffn_cast(h_a, g_pre, ffn_w_up.astype(F32), ffn_w_down.astype(F32), g_post,
                                        layer=layer, tf=TF_AUX)
        return _ffn(h_p, g_pre, w_up, w_down, g_post, tm=TM_FFN, tf=TF), out_a

    w_in_t = jnp.swapaxes(mlstm_w_in[:1].astype(F32), 1, 2)
    w_gate_t = jnp.pad(w_in_t[0, PROJ_MAIN:], ((0, GATE_LANES - 2 * HEADS), (0, 0))).astype(BF16)
    b_gate = jnp.pad(jnp.concatenate([mlstm_b_i[0], mlstm_b_f[0]]).astype(F32),
                     (0, GATE_LANES - 2 * HEADS)).reshape(1, GATE_LANES)
    g_mix = gain(norm_mix_pre[0])
    proj_a, gcol_a, w_in_tb = _norm_matmul_gates_cast(h_a, g_mix, w_in_t, w_gate_t, b_gate, n=PROJ_MAIN,
                                                      tn=TN_PROJ)
    proj_p, gcol_p = _norm_matmul_gates(h_p, g_mix, w_in_tb, w_gate_t, b_gate, tm=TM_PROJ, tn=TN_PROJ)

    def gate_rows(gcol, r0, n_blk, seqs, length):
        g8 = gcol[r0:r0 + n_blk * seqs * length, :2 * HEADS]
        return g8.reshape(n_blk, seqs, length, 2 * HEADS).transpose(0, 1, 3, 2)

    head_g = mlstm_head_norm[0].reshape(1, HEADS * DV).astype(F32)
    zc = jnp.zeros((1, HEADS, DQK, DV), F32)
    zn = jnp.zeros((1, HEADS, DQK), F32)
    zm = jnp.zeros((1, HEADS, GATE_LANES), F32)
    mix_m, c_m, n_m, m_m = _mlstm(proj_a, gcol_a, gate_rows(gcol_a, row_m, 1, 1, N_META), head_g, zc, zn, zm,
                                  row0=row_m, n_blocks=1, seqs=1, length=N_META, n_chunks=1)
    mix_p, c_p, n_p_, m_p = _mlstm(proj_p, gcol_p, gate_rows(gcol_p, 0, B * (S // CHUNK), 1, CHUNK), head_g,
                                   jnp.broadcast_to(c_m, (B,) + c_m.shape[1:]),
                                   jnp.broadcast_to(n_m, (B,) + n_m.shape[1:]),
                                   jnp.broadcast_to(m_m, (B,) + m_m.shape[1:]),
                                   row0=0, n_blocks=B, seqs=1, length=CHUNK, n_chunks=S // CHUNK)
    m0_s = jnp.broadcast_to(state_mlstm_m[0].astype(F32)[:, :, None], (DB, HEADS, GATE_LANES))
    mix_s, c_s, n_s_, m_s = _mlstm(proj_a, gcol_a, gate_rows(gcol_a, 0, DB // SAMPLE_SEQS, SAMPLE_SEQS, DS),
                                   head_g, state_mlstm_c[0].astype(F32), state_mlstm_n[0].astype(F32), m0_s,
                                   row0=0, n_blocks=DB // SAMPLE_SEQS, seqs=SAMPLE_SEQS, length=DS, n_chunks=1)
    mix_a = jnp.concatenate([mix_s, mix_m, pad_a], axis=0)
    h_p, h_a = matmul_norm_res(mix_p, mix_a, mlstm_w_out[:1].astype(F32), gain(norm_mix_post[0]))
    h_p, h_a = ffn(0)

    w_pool_in, g_mix = pool_w_in[:1].astype(F32), gain(norm_mix_pre[1])
    u_p = _norm_matmul(h_p, g_mix, w_pool_in, tm=TM, out_dtype=F32)
    u_a = _norm_matmul(h_a, g_mix, w_pool_in, tm=TM_AUX, out_dtype=F32)
    w_group = pool_w_group[:1].astype(F32)
    p_scale = pool_scale[0].reshape(1, D).astype(F32)
    pmix_m = _pool_rows(u_a, jnp.zeros((N_META, D), F32), w_group, p_scale, row0=row_m, n_tiles=1,
                        tr=N_META, tiles_per_seq=1, pos0=0, first_block=0)
    pmix_p = _pool_rows(u_p, u_a, w_group, p_scale, row0=0, n_tiles=n_p // POOL_TR, tr=POOL_TR,
                        tiles_per_seq=S // POOL_TR, pos0=N_META, first_block=row_m // N_META)
    pmix_s, pool_s = _pool_seqs(state_pool[:1].astype(F32), u_a, w_group, p_scale, n_new=DS,
                                seqs_per_block=POOL_SEQS, pos0=PAST_LEN)
    pmix_a = jnp.concatenate([pmix_s, pmix_m, pad_a], axis=0)
    h_p, h_a = matmul_norm_res(pmix_p, pmix_a, pool_w_out[:1].astype(F32), gain(norm_mix_post[1]))
    h_p, h_a = ffn(1)

    y_prompt = h_p.reshape(B, S, D)
    y_sample = h_a[:n_s].reshape(DB, DS, D)
    dt_c, dt_n, dt_m, dt_pool = state_mlstm_c.dtype, state_mlstm_n.dtype, state_mlstm_m.dtype, state_pool.dtype
    u_prompt = u_p.reshape(B, S, D)
    return (y_prompt, y_sample,
            c_p[None].astype(dt_c), n_p_[None].astype(dt_n), m_p[None, :, :, 0].astype(dt_m),
            u_prompt[None, :, S - POOL_BUF:].astype(dt_pool),
            c_s[None].astype(dt_c), n_s_[None].astype(dt_n), m_s[None, :, :, 0].astype(dt_m),
            pool_s[None].astype(dt_pool))
```

```python
import functools

import jax
import jax.numpy as jnp
from jax import lax
from jax.experimental import pallas as pl
from jax.experimental.pallas import tpu as pltpu

D = 2048
N_META = 16
HEADS = 4
DQK = 256
DV = 512
D_FF = 4 * D
POOL_WINDOWS = (2, 4, 8, 16)
POOL_GROUP_DIM = D // len(POOL_WINDOWS)
POOL_BUF = 15
PAST_LEN = 16384
EPS = 1e-6
K_SCALE = DQK ** -0.5
PROJ_MAIN = 2 * HEADS * DQK + 2 * HEADS * DV
GATE_LANES = 128

VMEM_LIMIT = 56 * 1024 * 1024

F32 = jnp.float32
BF16 = jnp.bfloat16


def _params(semantics):
    return pltpu.CompilerParams(dimension_semantics=semantics, vmem_limit_bytes=VMEM_LIMIT)


def _rmsnorm(x, g):
    return x * lax.rsqrt(jnp.mean(x * x, axis=-1, keepdims=True) + EPS) * g


def _dot(a, b):
    return jnp.dot(a, b, preferred_element_type=F32)


SUB_ROWS = 256
CAST_SUB_ROWS = 272


def _norm_matmul_kernel(x_ref, g_ref, w_ref, o_ref, wb_ref):
    @pl.when(pl.program_id(0) == 0)
    def _():
        wb_ref[...] = w_ref[...].astype(BF16)

    for r in range(0, x_ref.shape[0], SUB_ROWS):
        rows = slice(r, min(r + SUB_ROWS, x_ref.shape[0]))
        xn = _rmsnorm(x_ref[rows, :], g_ref[...]).astype(BF16)
        o_ref[rows, :] = _dot(xn, wb_ref[...]).astype(o_ref.dtype)


def _dot_t(a, b):
    return lax.dot_general(a, b, (((1,), (1,)), ((), ())), preferred_element_type=F32)


def _in_proj_first_block(x_ref, g_ref, w_t, wgt_ref, bg_ref, o_ref, gate_ref, xn_ref):
    tm = x_ref.shape[0]
    for r in range(0, tm, SUB_ROWS):
        rows = slice(r, min(r + SUB_ROWS, tm))
        xn = _rmsnorm(x_ref[rows, :], g_ref[...]).astype(BF16)
        xn_ref[rows, :] = xn
        z = _dot_t(xn, wgt_ref[...]) + bg_ref[...]
        lane = lax.broadcasted_iota(jnp.int32, z.shape, 1)
        log_sig = jnp.minimum(z, 0.0) - jnp.log1p(jnp.exp(-jnp.abs(z)))
        gate_ref[rows, :] = jnp.where(lane >= HEADS, log_sig, z)
        o_ref[rows, :] = _dot_t(xn, w_t).astype(o_ref.dtype)


def _norm_matmul_gates_kernel(x_ref, g_ref, wt_ref, wgt_ref, bg_ref, o_ref, gate_ref, xn_ref):
    @pl.when(pl.program_id(1) == 0)
    def _():
        _in_proj_first_block(x_ref, g_ref, wt_ref[...], wgt_ref, bg_ref, o_ref, gate_ref, xn_ref)

    @pl.when(pl.program_id(1) > 0)
    def _():
        o_ref[...] = _dot_t(xn_ref[...], wt_ref[...]).astype(o_ref.dtype)


def _norm_matmul_gates_cast_kernel(x_ref, g_ref, wt_ref, wgt_ref, bg_ref, o_ref, gate_ref, wtb_ref, xn_ref):
    wtb_ref[...] = wt_ref[...].astype(BF16)

    @pl.when(pl.program_id(1) == 0)
    def _():
        _in_proj_first_block(x_ref, g_ref, wtb_ref[...], wgt_ref, bg_ref, o_ref, gate_ref, xn_ref)

    @pl.when(pl.program_id(1) > 0)
    def _():
        o_ref[...] = _dot_t(xn_ref[...], wtb_ref[...]).astype(o_ref.dtype)


def _norm_matmul(x, g, w, *, tm, out_dtype):
    rows = x.shape[0]
    return pl.pallas_call(
        _norm_matmul_kernel,
        out_shape=jax.ShapeDtypeStruct((rows, D), out_dtype),
        grid=(rows // tm,),
        in_specs=[pl.BlockSpec((tm, D), lambda i: (i, 0)),
                  pl.BlockSpec((1, D), lambda i: (0, 0)),
                  pl.BlockSpec((None, D, D), lambda i: (0, 0, 0), pipeline_mode=pl.Buffered(1))],
        out_specs=pl.BlockSpec((tm, D), lambda i: (i, 0)),
        scratch_shapes=[pltpu.VMEM((D, D), BF16)],
        compiler_params=_params(("arbitrary",)),
        name="norm_matmul",
    )(x, g, w)


def _norm_matmul_gates(x, g, w_t, wg_t, bg, *, tm, tn):
    rows, n = x.shape[0], w_t.shape[0]
    return pl.pallas_call(
        _norm_matmul_gates_kernel,
        out_shape=(jax.ShapeDtypeStruct((rows, n), BF16),
                   jax.ShapeDtypeStruct((rows, GATE_LANES), F32)),
        grid=(rows // tm, n // tn),
        in_specs=[pl.BlockSpec((tm, D), lambda i, j: (i, 0)),
                  pl.BlockSpec((1, D), lambda i, j: (0, 0)),
                  pl.BlockSpec((tn, D), lambda i, j: (j, 0)),
                  pl.BlockSpec((GATE_LANES, D), lambda i, j: (0, 0)),
                  pl.BlockSpec((1, GATE_LANES), lambda i, j: (0, 0))],
        out_specs=(pl.BlockSpec((tm, tn), lambda i, j: (i, j)),
                   pl.BlockSpec((tm, GATE_LANES), lambda i, j: (i, 0))),
        scratch_shapes=[pltpu.VMEM((tm, D), BF16)],
        compiler_params=_params(("parallel", "arbitrary")),
        name="norm_matmul_gates",
    )(x, g, w_t, wg_t, bg)


def _norm_matmul_gates_cast(x, g, w_t, wg_t, bg, *, n, tn):
    rows = x.shape[0]
    return pl.pallas_call(
        _norm_matmul_gates_cast_kernel,
        out_shape=(jax.ShapeDtypeStruct((rows, n), BF16),
                   jax.ShapeDtypeStruct((rows, GATE_LANES), F32),
                   jax.ShapeDtypeStruct((n, D), BF16)),
        grid=(1, n // tn),
        in_specs=[pl.BlockSpec((rows, D), lambda i, j: (0, 0), pipeline_mode=pl.Buffered(1)),
                  pl.BlockSpec((1, D), lambda i, j: (0, 0)),
                  pl.BlockSpec((None, tn, D), lambda i, j: (0, j, 0)),
                  pl.BlockSpec((GATE_LANES, D), lambda i, j: (0, 0)),
                  pl.BlockSpec((1, GATE_LANES), lambda i, j: (0, 0))],
        out_specs=(pl.BlockSpec((rows, tn), lambda i, j: (0, j)),
                   pl.BlockSpec((rows, GATE_LANES), lambda i, j: (0, 0)),
                   pl.BlockSpec((tn, D), lambda i, j: (j, 0))),
        scratch_shapes=[pltpu.VMEM((rows, D), BF16)],
        compiler_params=_params(("arbitrary", "arbitrary")),
        name="norm_matmul_gates_cast",
    )(x, g, w_t, wg_t, bg)


def _matmul_norm_res_kernel(a_ref, w_ref, g_ref, h_ref, o_ref, wb_ref):
    @pl.when(pl.program_id(0) == 0)
    def _():
        wb_ref[...] = w_ref[...].astype(BF16)

    y = _dot(a_ref[...], wb_ref[...])
    o_ref[...] = h_ref[...] + _rmsnorm(y, g_ref[...])


def _matmul_norm_res(a, w, g, h, *, tm):
    rows = a.shape[0]
    return pl.pallas_call(
        _matmul_norm_res_kernel,
        out_shape=jax.ShapeDtypeStruct((rows, D), F32),
        grid=(rows // tm,),
        in_specs=[pl.BlockSpec((tm, D), lambda i: (i, 0)),
                  pl.BlockSpec((None, D, D), lambda i: (0, 0, 0), pipeline_mode=pl.Buffered(1)),
                  pl.BlockSpec((1, D), lambda i: (0, 0)),
                  pl.BlockSpec((tm, D), lambda i: (i, 0))],
        out_specs=pl.BlockSpec((tm, D), lambda i: (i, 0)),
        scratch_shapes=[pltpu.VMEM((D, D), BF16)],
        compiler_params=_params(("arbitrary",)),
        name="matmul_norm_res",
    )(a, w, g, h)


def _ffn_chunk(f, last, h_ref, gpre_ref, wup_ref, wdown_ref, gpost_ref, o_ref, xn_ref, *, sub, sub_middle):
    tm = h_ref.shape[0]
    subs = [slice(r, min(r + sub, tm)) for r in range(0, tm, sub)]

    def mlp_chunk(xn):
        a = jnp.maximum(_dot(xn, wup_ref[...]), 0.0)
        return _dot((a * a).astype(BF16), wdown_ref[...])

    @pl.when(f == 0)
    def _():
        for rows in subs:
            xn = _rmsnorm(h_ref[rows, :], gpre_ref[...]).astype(BF16)
            xn_ref[rows, :] = xn
            o_ref[rows, :] = mlp_chunk(xn)

    @pl.when(jnp.logical_and(f > 0, f < last))
    def _():
        for rows in (subs if sub_middle else [slice(None)]):
            o_ref[rows, :] += mlp_chunk(xn_ref[rows, :])

    @pl.when(f == last)
    def _():
        for rows in subs:
            y = o_ref[rows, :] + mlp_chunk(xn_ref[rows, :])
            o_ref[rows, :] = h_ref[rows, :] + _rmsnorm(y, gpost_ref[...])


def _ffn_kernel(h_ref, gpre_ref, wup_ref, wdown_ref, gpost_ref, o_ref, xn_ref):
    _ffn_chunk(pl.program_id(1), pl.num_programs(1) - 1, h_ref, gpre_ref, wup_ref, wdown_ref, gpost_ref,
               o_ref, xn_ref, sub=SUB_ROWS, sub_middle=False)


def _ffn_cast_kernel(h_ref, gpre_ref, wup_ref, wdown_ref, gpost_ref, o_ref, wupb_ref, wdownb_ref, xn_ref):
    wupb_ref[...] = wup_ref[...].astype(BF16)
    wdownb_ref[...] = wdown_ref[...].astype(BF16)
    _ffn_chunk(pl.program_id(0), pl.num_programs(0) - 1, h_ref, gpre_ref, wupb_ref, wdownb_ref, gpost_ref,
               o_ref, xn_ref, sub=CAST_SUB_ROWS, sub_middle=True)


def _ffn(h, g_pre, w_up, w_down, g_post, *, tm, tf):
    n_tiles = h.shape[0] // tm
    return pl.pallas_call(
        _ffn_kernel,
        out_shape=jax.ShapeDtypeStruct((n_tiles * tm, D), F32),
        grid=(n_tiles, D_FF // tf),
        in_specs=[pl.BlockSpec((tm, D), lambda i, f: (i, 0)),
                  pl.BlockSpec((1, D), lambda i, f: (0, 0)),
                  pl.BlockSpec((D, tf), lambda i, f: (0, f)),
                  pl.BlockSpec((tf, D), lambda i, f: (f, 0)),
                  pl.BlockSpec((1, D), lambda i, f: (0, 0))],
        out_specs=pl.BlockSpec((tm, D), lambda i, f: (i, 0)),
        scratch_shapes=[pltpu.VMEM((tm, D), BF16)],
        compiler_params=_params(("parallel", "arbitrary")),
        name="ffn",
    )(h, g_pre, w_up, w_down, g_post)


def _ffn_cast(h, g_pre, w_up, w_down, g_post, *, layer, tf):
    rows = h.shape[0]
    single = pl.Buffered(1)
    return pl.pallas_call(
        _ffn_cast_kernel,
        out_shape=(jax.ShapeDtypeStruct((rows, D), F32),
                   jax.ShapeDtypeStruct((D, D_FF), BF16),
                   jax.ShapeDtypeStruct((D_FF, D), BF16)),
        grid=(D_FF // tf,),
        in_specs=[pl.BlockSpec((rows, D), lambda f: (0, 0), pipeline_mode=single),
                  pl.BlockSpec((1, D), lambda f: (0, 0)),
                  pl.BlockSpec((None, D, tf), lambda f: (layer, 0, f)),
                  pl.BlockSpec((None, tf, D), lambda f: (layer, f, 0)),
                  pl.BlockSpec((1, D), lambda f: (0, 0))],
        out_specs=(pl.BlockSpec((rows, D), lambda f: (0, 0), pipeline_mode=single),
                   pl.BlockSpec((D, tf), lambda f: (0, f)),
                   pl.BlockSpec((tf, D), lambda f: (f, 0))),
        scratch_shapes=[pltpu.VMEM((rows, D), BF16)],
        compiler_params=_params(("arbitrary",)),
        name="ffn_cast",
    )(h, g_pre, w_up, w_down, g_post)


def _mlstm_kernel(qk_ref, v_ref, o_ref, gcol_ref, grow_ref, hg_ref, c0_ref, n0_ref, m0_ref,
                  out_ref, c_ref, n_ref, m_ref, *, seqs, length, single_chunk):
    if single_chunk:
        c_in, n_in, m_in = c0_ref, n0_ref, m0_ref
    else:
        c_in, n_in, m_in = c_ref, n_ref, m_ref

        @pl.when(pl.program_id(1) == 0)
        def _():
            c_ref[...] = c0_ref[...]
            n_ref[...] = n0_ref[...]
            m_ref[...] = m0_ref[...]

    pairs = [(s, hd) for s in range(seqs) for hd in range(HEADS)]
    L = length
    t_idx = lax.broadcasted_iota(jnp.int32, (L, L), 0)
    s_idx = lax.broadcasted_iota(jnp.int32, (L, L), 1)
    causal = s_idx <= t_idx

    def rows_of(s):
        return slice(s * L, (s + 1) * L)

    def q_of(s, hd):
        return qk_ref[rows_of(s), hd * DQK:(hd + 1) * DQK]

    def k_of(s, hd):
        return qk_ref[rows_of(s), (HEADS + hd) * DQK:(HEADS + hd + 1) * DQK]

    def v_of(s, hd):
        return v_ref[rows_of(s), hd * DV:(hd + 1) * DV]

    qk = [_dot_t(q_of(s, hd), k_of(s, hd)) for s, hd in pairs]
    qc = [_dot(q_of(s, hd), c_in[s, hd].astype(BF16)) for s, hd in pairs]

    gate_vals = []
    for s, hd in pairs:
        gcol = gcol_ref[rows_of(s), :]
        grow = grow_ref[0, s]
        li_row, lf_row = grow[hd:hd + 1, :], grow[HEADS + hd:HEADS + hd + 1, :]
        li_col, lf_col = gcol[:, hd:hd + 1], gcol[:, HEADS + hd:HEADS + hd + 1]
        m = m_in[s, hd:hd + 1, 0:1]
        b_col = jnp.sum(jnp.where(causal, lf_row, 0.0), axis=1, keepdims=True)
        b_row = jnp.sum(jnp.where(t_idx <= s_idx, lf_col, 0.0), axis=0, keepdims=True)
        b_tot = b_col[L - 1:L, :]
        dmat = jnp.where(causal, b_col - b_row + li_row, -jnp.inf)
        m_inter = b_col + m
        m_t = jnp.maximum(m_inter, jnp.max(dmat, axis=1, keepdims=True))
        p = jnp.exp(dmat - m_t)
        w_inter = jnp.exp(m_inter - m_t)
        m_new = m_t[L - 1:L, :]
        decay = jnp.exp(b_tot - b_col + li_col - m_new)
        carry = jnp.exp(b_tot + m - m_new)
        gate_vals.append((p, w_inter, m_t, m_new, decay, carry))

    scores = [qk[i] * (gate_vals[i][0] * K_SCALE) for i in range(len(pairs))]
    kd = [k_of(s, hd).astype(F32) * (gate_vals[i][4] * K_SCALE) for i, (s, hd) in enumerate(pairs)]
    sv = [_dot(scores[i].astype(BF16), v_of(s, hd)) for i, (s, hd) in enumerate(pairs)]
    kv = [lax.dot_general(kd[i].astype(BF16), v_of(s, hd), (((0,), (0,)), ((), ())),
                          preferred_element_type=F32) for i, (s, hd) in enumerate(pairs)]

    for i, (s, hd) in enumerate(pairs):
        p, w_inter, m_t, m_new, decay, carry = gate_vals[i]
        n_row = n_in[s, hd:hd + 1, :]
        c_new = carry * c_in[s, hd] + kv[i]
        n_new = carry * n_row + jnp.sum(kd[i], axis=0, keepdims=True)
        num = sv[i] + qc[i] * w_inter
        qn = jnp.sum(q_of(s, hd).astype(F32) * n_row, axis=1, keepdims=True)
        den = jnp.sum(scores[i], axis=1, keepdims=True) + w_inter * qn
        h = num / jnp.maximum(jnp.abs(den), jnp.exp(-m_t))
        c_ref[s, hd] = c_new
        n_ref[s, hd:hd + 1, :] = n_new
        m_ref[s, hd:hd + 1, :] = jnp.broadcast_to(m_new, (1, GATE_LANES))
        hn = _rmsnorm(h, hg_ref[:, hd * DV:(hd + 1) * DV])
        o_pre = o_ref[rows_of(s), hd * DV:(hd + 1) * DV].astype(F32)
        gate = 0.5 * jnp.tanh(0.5 * o_pre) + 0.5
        out_ref[rows_of(s), hd * DV:(hd + 1) * DV] = (hn * gate).astype(out_ref.dtype)


def _mlstm(proj, gcol, grow, head_g, c0, n0, m0, *, row0, n_blocks, seqs, length, n_chunks):
    assert seqs == 1 or n_chunks == 1
    blk = seqs * length
    assert row0 % blk == 0
    b0 = row0 // blk
    n_seq = n_blocks * seqs

    def rows_map(col):
        return lambda b, c: (b0 + b * n_chunks + c, col)

    kernel = functools.partial(_mlstm_kernel, seqs=seqs, length=length, single_chunk=n_chunks == 1)
    return pl.pallas_call(
        kernel,
        out_shape=(jax.ShapeDtypeStruct((n_blocks * n_chunks * blk, HEADS * DV), BF16),
                   jax.ShapeDtypeStruct((n_seq, HEADS, DQK, DV), F32),
                   jax.ShapeDtypeStruct((n_seq, HEADS, DQK), F32),
                   jax.ShapeDtypeStruct((n_seq, HEADS, GATE_LANES), F32)),
        grid=(n_blocks, n_chunks),
        in_specs=[pl.BlockSpec((blk, 2 * HEADS * DQK), rows_map(0)),
                  pl.BlockSpec((blk, HEADS * DV), rows_map(1)),
                  pl.BlockSpec((blk, HEADS * DV), rows_map(2)),
                  pl.BlockSpec((blk, GATE_LANES), rows_map(0)),
                  pl.BlockSpec((1, seqs, 2 * HEADS, length), lambda b, c: (b * n_chunks + c, 0, 0, 0)),
                  pl.BlockSpec((1, HEADS * DV), lambda b, c: (0, 0)),
                  pl.BlockSpec((seqs, HEADS, DQK, DV), lambda b, c: (b, 0, 0, 0)),
                  pl.BlockSpec((seqs, HEADS, DQK), lambda b, c: (b, 0, 0)),
                  pl.BlockSpec((seqs, HEADS, GATE_LANES), lambda b, c: (b, 0, 0))],
        out_specs=(pl.BlockSpec((blk, HEADS * DV), lambda b, c: (b * n_chunks + c, 0)),
                   pl.BlockSpec((seqs, HEADS, DQK, DV), lambda b, c: (b, 0, 0, 0)),
                   pl.BlockSpec((seqs, HEADS, DQK), lambda b, c: (b, 0, 0)),
                   pl.BlockSpec((seqs, HEADS, GATE_LANES), lambda b, c: (b, 0, 0))),
        compiler_params=_params(("parallel", "arbitrary")),
        name=f"mlstm_l{length}",
    )(proj, proj, proj, gcol, grow, head_g, c0, n0, m0)


def _pool_mix(ext, first, n_rows, pos, wg_ref, scale_ref):
    outs = []
    for g, w in enumerate(POOL_WINDOWS):
        cols = slice(g * POOL_GROUP_DIM, (g + 1) * POOL_GROUP_DIM)
        e = ext[:, cols]
        total, span = e, 1
        while span < w:
            total = total + pltpu.roll(total, span, 0)
            span *= 2
        cnt = jnp.minimum(float(w), pos + 1.0)
        pooled = total[first:first + n_rows] / cnt - e[first:first + n_rows]
        outs.append(_dot(pooled.astype(BF16), wg_ref[g].astype(BF16)))
    mixed = jnp.concatenate(outs, axis=-1) * scale_ref[...]
    return mixed.astype(BF16)


def _pool_rows_kernel(u_ref, prev_ref, first_ref, wg_ref, scale_ref, o_ref, *, tiles_per_seq, pos0):
    tr = u_ref.shape[0]
    tile = pl.program_id(0) % tiles_per_seq
    halo = jnp.where(tile == 0, first_ref[...], prev_ref[...])
    ext = jnp.concatenate([halo, u_ref[...]], axis=0)
    pos = (lax.broadcasted_iota(jnp.int32, (tr, 1), 0) + (tile * tr + pos0)).astype(F32)
    o_ref[...] = _pool_mix(ext, N_META, tr, pos, wg_ref, scale_ref)


def _pool_rows(u, first_src, w_group, scale, *, row0, n_tiles, tr, tiles_per_seq, pos0, first_block):
    hb = tr // N_META
    t0 = row0 // tr
    kernel = functools.partial(_pool_rows_kernel, tiles_per_seq=tiles_per_seq, pos0=pos0)
    return pl.pallas_call(
        kernel,
        out_shape=jax.ShapeDtypeStruct((n_tiles * tr, D), BF16),
        grid=(n_tiles,),
        in_specs=[pl.BlockSpec((tr, D), lambda i: (t0 + i, 0)),
                  pl.BlockSpec((N_META, D), lambda i: (jnp.maximum((t0 + i) * hb - 1, 0), 0)),
                  pl.BlockSpec((N_META, D), lambda i: (first_block, 0)),
                  pl.BlockSpec((None, len(POOL_WINDOWS), POOL_GROUP_DIM, POOL_GROUP_DIM),
                               lambda i: (0, 0, 0, 0)),
                  pl.BlockSpec((1, D), lambda i: (0, 0))],
        out_specs=pl.BlockSpec((tr, D), lambda i: (i, 0)),
        compiler_params=_params(("parallel",)),
        name=f"pool_rows_{tr}",
    )(u, u, first_src, w_group, scale)


def _pool_seqs_kernel(prefix_ref, u_ref, wg_ref, scale_ref, o_ref, buf_ref, *, pos0):
    g = prefix_ref.shape[0]
    n_new = u_ref.shape[0] // g
    r = 1 + POOL_BUF + n_new
    prefix = prefix_ref[...]
    u3 = u_ref[...].reshape(g, n_new, D)
    ext = jnp.concatenate([jnp.zeros((g, 1, D), F32), prefix, u3], axis=1).reshape(g * r, D)
    buf_ref[...] = jnp.concatenate([prefix, u3], axis=1)[:, n_new:, :]
    outs = []
    pos = (lax.broadcasted_iota(jnp.int32, (g, n_new, 1), 1) + pos0).astype(F32)
    for gi, w in enumerate(POOL_WINDOWS):
        cols = slice(gi * POOL_GROUP_DIM, (gi + 1) * POOL_GROUP_DIM)
        e = ext[:, cols]
        total, span = e, 1
        while span < w:
            total = total + pltpu.roll(total, span, 0)
            span *= 2
        cnt = jnp.minimum(float(w), pos + 1.0)
        tot3 = total.reshape(g, r, POOL_GROUP_DIM)[:, 1 + POOL_BUF:, :]
        e3 = e.reshape(g, r, POOL_GROUP_DIM)[:, 1 + POOL_BUF:, :]
        pooled = (tot3 / cnt - e3).reshape(g * n_new, POOL_GROUP_DIM)
        outs.append(_dot(pooled.astype(BF16), wg_ref[gi].astype(BF16)))
    mixed = jnp.concatenate(outs, axis=-1) * scale_ref[...]
    o_ref[...] = mixed.astype(BF16)


def _pool_seqs(prefix, u, w_group, scale, *, n_new, seqs_per_block, pos0):
    n_seq = prefix.shape[1]
    kernel = functools.partial(_pool_seqs_kernel, pos0=pos0)
    return pl.pallas_call(
        kernel,
        out_shape=(jax.ShapeDtypeStruct((n_seq * n_new, D), BF16),
                   jax.ShapeDtypeStruct((n_seq, POOL_BUF, D), F32)),
        grid=(n_seq // seqs_per_block,),
        in_specs=[pl.BlockSpec((None, seqs_per_block, POOL_BUF, D), lambda i: (0, i, 0, 0)),
                  pl.BlockSpec((seqs_per_block * n_new, D), lambda i: (i, 0)),
                  pl.BlockSpec((None, len(POOL_WINDOWS), POOL_GROUP_DIM, POOL_GROUP_DIM),
                               lambda i: (0, 0, 0, 0)),
                  pl.BlockSpec((1, D), lambda i: (0, 0))],
        out_specs=(pl.BlockSpec((seqs_per_block * n_new, D), lambda i: (i, 0)),
                   pl.BlockSpec((seqs_per_block, POOL_BUF, D), lambda i: (i, 0, 0))),
        compiler_params=_params(("parallel",)),
        name="pool_seqs",
    )(prefix, u, w_group, scale)


TM_PROJ = 1024
TN_PROJ = 1024
TM = 512
TM_AUX = 544
TM_FFN = 512
TF = 1024
TF_AUX = 512
CHUNK = 512
SAMPLE_SEQS = 4
POOL_TR = 512
POOL_SEQS = 16


def kernel(x_prompt, x_sample, state_mlstm_c, state_mlstm_n, state_mlstm_m, state_pool, meta_tokens,
           norm_mix_pre, norm_mix_post, norm_ffn_pre, norm_ffn_post, mlstm_w_in, mlstm_b_i, mlstm_b_f,
           mlstm_head_norm, mlstm_w_out, pool_w_in, pool_w_group, pool_scale, pool_w_out,
           ffn_w_up, ffn_w_down):
    B, S, _ = x_prompt.shape
    DB, DS, _ = x_sample.shape
    n_p, n_s = B * S, DB * DS
    rows_a = 2 * TM_AUX
    row_m = n_s
    assert n_s + N_META <= rows_a and n_p % TM_PROJ == 0 and n_p % TM == 0
    assert S % CHUNK == 0 and S % POOL_TR == 0 and row_m % N_META == 0

    h_p = x_prompt.reshape(n_p, D)
    h_a = jnp.concatenate([x_sample.reshape(n_s, D), meta_tokens.astype(F32),
                           jnp.zeros((rows_a - n_s - N_META, D), F32)], axis=0)
    pad_a = jnp.zeros((rows_a - n_s - N_META, D), BF16)

    def gain(a):
        return a.reshape(1, D).astype(F32)


    def matmul_norm_res(a_p, a_a, w, g):
        return _matmul_norm_res(a_p, w, g, h_p, tm=TM), _matmul_norm_res(a_a, w, g, h_a, tm=TM_AUX)

    def ffn(layer):
        g_pre, g_post = gain(norm_ffn_pre[layer]), gain(norm_ffn_post[layer])
        out_a, w_up, w_down = _ffn_cast(h_a, g_pre, ffn_w_up.astype(F32), ffn_w_down.astype(F32), g_post,
                                        layer=layer, tf=TF_AUX)
        return _ffn(h_p, g_pre, w_up, w_down, g_post, tm=TM_FFN, tf=TF), out_a

    w_in_t = jnp.swapaxes(mlstm_w_in[:1].astype(F32), 1, 2)
    w_gate_t = jnp.pad(w_in_t[0, PROJ_MAIN:], ((0, GATE_LANES - 2 * HEADS), (0, 0))).astype(BF16)
    b_gate = jnp.pad(jnp.concatenate([mlstm_b_i[0], mlstm_b_f[0]]).astype(F32),
                     (0, GATE_LANES - 2 * HEADS)).reshape(1, GATE_LANES)
    g_mix = gain(norm_mix_pre[0])
    proj_a, gcol_a, w_in_tb = _norm_matmul_gates_cast(h_a, g_mix, w_in_t, w_gate_t, b_gate, n=PROJ_MAIN,
                                                      tn=TN_PROJ)
    proj_p, gcol_p = _norm_matmul_gates(h_p, g_mix, w_in_tb, w_gate_t, b_gate, tm=TM_PROJ, tn=2 * TN_PROJ)

    def gate_rows(gcol, r0, n_blk, seqs, length):
        g8 = gcol[r0:r0 + n_blk * seqs * length, :2 * HEADS]
        return g8.reshape(n_blk, seqs, length, 2 * HEADS).transpose(0, 1, 3, 2)

    head_g = mlstm_head_norm[0].reshape(1, HEADS * DV).astype(F32)
    zc = jnp.zeros((1, HEADS, DQK, DV), F32)
    zn = jnp.zeros((1, HEADS, DQK), F32)
    zm = jnp.zeros((1, HEADS, GATE_LANES), F32)
    mix_m, c_m, n_m, m_m = _mlstm(proj_a, gcol_a, gate_rows(gcol_a, row_m, 1, 1, N_META), head_g, zc, zn, zm,
                                  row0=row_m, n_blocks=1, seqs=1, length=N_META, n_chunks=1)
    mix_p, c_p, n_p_, m_p = _mlstm(proj_p, gcol_p, gate_rows(gcol_p, 0, B * (S // CHUNK), 1, CHUNK), head_g,
                                   jnp.broadcast_to(c_m, (B,) + c_m.shape[1:]),
                                   jnp.broadcast_to(n_m, (B,) + n_m.shape[1:]),
                                   jnp.broadcast_to(m_m, (B,) + m_m.shape[1:]),
                                   row0=0, n_blocks=B, seqs=1, length=CHUNK, n_chunks=S // CHUNK)
    m0_s = jnp.broadcast_to(state_mlstm_m[0].astype(F32)[:, :, None], (DB, HEADS, GATE_LANES))
    mix_s, c_s, n_s_, m_s = _mlstm(proj_a, gcol_a, gate_rows(gcol_a, 0, DB // SAMPLE_SEQS, SAMPLE_SEQS, DS),
                                   head_g, state_mlstm_c[0].astype(F32), state_mlstm_n[0].astype(F32), m0_s,
                                   row0=0, n_blocks=DB // SAMPLE_SEQS, seqs=SAMPLE_SEQS, length=DS, n_chunks=1)
    mix_a = jnp.concatenate([mix_s, mix_m, pad_a], axis=0)
    h_p, h_a = matmul_norm_res(mix_p, mix_a, mlstm_w_out[:1].astype(F32), gain(norm_mix_post[0]))
    h_p, h_a = ffn(0)

    w_pool_in, g_mix = pool_w_in[:1].astype(F32), gain(norm_mix_pre[1])
    u_p = _norm_matmul(h_p, g_mix, w_pool_in, tm=TM, out_dtype=F32)
    u_a = _norm_matmul(h_a, g_mix, w_pool_in, tm=TM_AUX, out_dtype=F32)
    w_group = pool_w_group[:1].astype(F32)
    p_scale = pool_scale[0].reshape(1, D).astype(F32)
    pmix_m = _pool_rows(u_a, jnp.zeros((N_META, D), F32), w_group, p_scale, row0=row_m, n_tiles=1,
                        tr=N_META, tiles_per_seq=1, pos0=0, first_block=0)
    pmix_p = _pool_rows(u_p, u_a, w_group, p_scale, row0=0, n_tiles=n_p // POOL_TR, tr=POOL_TR,
                        tiles_per_seq=S // POOL_TR, pos0=N_META, first_block=row_m // N_META)
    pmix_s, pool_s = _pool_seqs(state_pool[:1].astype(F32), u_a, w_group, p_scale, n_new=DS,
                                seqs_per_block=POOL_SEQS, pos0=PAST_LEN)
    pmix_a = jnp.concatenate([pmix_s, pmix_m, pad_a], axis=0)
    h_p, h_a = matmul_norm_res(pmix_p, pmix_a, pool_w_out[:1].astype(F32), gain(norm_mix_post[1]))
    h_p, h_a = ffn(1)

    y_prompt = h_p.reshape(B, S, D)
    y_sample = h_a[:n_s].reshape(DB, DS, D)
    dt_c, dt_n, dt_m, dt_pool = state_mlstm_c.dtype, state_mlstm_n.dtype, state_mlstm_m.dtype, state_pool.dtype
    u_prompt = u_p.reshape(B, S, D)
    return (y_prompt, y_sample,
            c_p[None].astype(dt_c), n_p_[None].astype(dt_n), m_p[None, :, :, 0].astype(dt_m),
            u_prompt[None, :, S - POOL_BUF:].astype(dt_pool),
            c_s[None].astype(dt_c), n_s_[None].astype(dt_n), m_s[None, :, :, 0].astype(dt_m),
            pool_s[None].astype(dt_pool))
```

```python
import functools

import jax
import jax.numpy as jnp
from jax import lax
from jax.experimental import pallas as pl
from jax.experimental.pallas import tpu as pltpu

D = 2048
N_META = 16
HEADS = 4
DQK = 256
DV = 512
D_FF = 4 * D
POOL_WINDOWS = (2, 4, 8, 16)
POOL_GROUP_DIM = D // len(POOL_WINDOWS)
POOL_BUF = 15
PAST_LEN = 16384
EPS = 1e-6
K_SCALE = DQK ** -0.5
PROJ_MAIN = 2 * HEADS * DQK + 2 * HEADS * DV
GATE_LANES = 128

VMEM_LIMIT = 56 * 1024 * 1024

F32 = jnp.float32
BF16 = jnp.bfloat16

TM_PROJ = 1024
TN_PROJ = 1024
TM = 512
TM_AUX = 544
TM_FFN = 512
TF = 1024
TF_AUX = 512
SUB_ROWS = 256
CAST_SUB_ROWS = TM_AUX // 2
CHUNK = 512
SAMPLE_SEQS = 4
POOL_TR = 512
POOL_SEQS = 16


def _params(semantics):
    return pltpu.CompilerParams(dimension_semantics=semantics, vmem_limit_bytes=VMEM_LIMIT)


def _rmsnorm(x, g):
    return x * lax.rsqrt(jnp.mean(x * x, axis=-1, keepdims=True) + EPS) * g


def _dot(a, b):
    return jnp.dot(a, b, preferred_element_type=F32)


def _norm_matmul_kernel(x_ref, g_ref, w_ref, o_ref, wb_ref):
    @pl.when(pl.program_id(0) == 0)
    def _():
        wb_ref[...] = w_ref[...].astype(BF16)

    for r in range(0, x_ref.shape[0], SUB_ROWS):
        rows = slice(r, min(r + SUB_ROWS, x_ref.shape[0]))
        xn = _rmsnorm(x_ref[rows, :], g_ref[...]).astype(BF16)
        o_ref[rows, :] = _dot(xn, wb_ref[...]).astype(o_ref.dtype)


def _dot_t(a, b):
    return lax.dot_general(a, b, (((1,), (1,)), ((), ())), preferred_element_type=F32)


def _in_proj_first_block(x_ref, g_ref, w_t, wgt_ref, bg_ref, o_ref, gate_ref, xn_ref):
    tm = x_ref.shape[0]
    for r in range(0, tm, SUB_ROWS):
        rows = slice(r, min(r + SUB_ROWS, tm))
        xn = _rmsnorm(x_ref[rows, :], g_ref[...]).astype(BF16)
        xn_ref[rows, :] = xn
        z = _dot_t(xn, wgt_ref[...]) + bg_ref[...]
        lane = lax.broadcasted_iota(jnp.int32, z.shape, 1)
        log_sig = jnp.minimum(z, 0.0) - jnp.log1p(jnp.exp(-jnp.abs(z)))
        gate_ref[rows, :] = jnp.where(lane >= HEADS, log_sig, z)
        o_ref[rows, :] = _dot_t(xn, w_t).astype(o_ref.dtype)


def _norm_matmul_gates_kernel(x_ref, g_ref, wt_ref, wgt_ref, bg_ref, o_ref, gate_ref, xn_ref):
    @pl.when(pl.program_id(1) == 0)
    def _():
        _in_proj_first_block(x_ref, g_ref, wt_ref[...], wgt_ref, bg_ref, o_ref, gate_ref, xn_ref)

    @pl.when(pl.program_id(1) > 0)
    def _():
        o_ref[...] = _dot_t(xn_ref[...], wt_ref[...]).astype(o_ref.dtype)


def _norm_matmul_gates_cast_kernel(x_ref, g_ref, wt_ref, wgt_ref, bg_ref, o_ref, gate_ref, wtb_ref, xn_ref):
    wtb_ref[...] = wt_ref[...].astype(BF16)

    @pl.when(pl.program_id(1) == 0)
    def _():
        _in_proj_first_block(x_ref, g_ref, wtb_ref[...], wgt_ref, bg_ref, o_ref, gate_ref, xn_ref)

    @pl.when(pl.program_id(1) > 0)
    def _():
        o_ref[...] = _dot_t(xn_ref[...], wtb_ref[...]).astype(o_ref.dtype)


def _norm_matmul(x, g, w, *, tm, out_dtype):
    rows = x.shape[0]
    return pl.pallas_call(
        _norm_matmul_kernel,
        out_shape=jax.ShapeDtypeStruct((rows, D), out_dtype),
        grid=(rows // tm,),
        in_specs=[pl.BlockSpec((tm, D), lambda i: (i, 0)),
                  pl.BlockSpec((1, D), lambda i: (0, 0)),
                  pl.BlockSpec((None, D, D), lambda i: (0, 0, 0), pipeline_mode=pl.Buffered(1))],
        out_specs=pl.BlockSpec((tm, D), lambda i: (i, 0)),
        scratch_shapes=[pltpu.VMEM((D, D), BF16)],
        compiler_params=_params(("arbitrary",)),
        name="norm_matmul",
    )(x, g, w)


def _norm_matmul_gates(x, g, w_t, wg_t, bg, *, tm, tn):
    rows, n = x.shape[0], w_t.shape[0]
    return pl.pallas_call(
        _norm_matmul_gates_kernel,
        out_shape=(jax.ShapeDtypeStruct((rows, n), BF16),
                   jax.ShapeDtypeStruct((rows, GATE_LANES), F32)),
        grid=(rows // tm, n // tn),
        in_specs=[pl.BlockSpec((tm, D), lambda i, j: (i, 0)),
                  pl.BlockSpec((1, D), lambda i, j: (0, 0)),
                  pl.BlockSpec((tn, D), lambda i, j: (j, 0)),
                  pl.BlockSpec((GATE_LANES, D), lambda i, j: (0, 0)),
                  pl.BlockSpec((1, GATE_LANES), lambda i, j: (0, 0))],
        out_specs=(pl.BlockSpec((tm, tn), lambda i, j: (i, j)),
                   pl.BlockSpec((tm, GATE_LANES), lambda i, j: (i, 0))),
        scratch_shapes=[pltpu.VMEM((tm, D), BF16)],
        compiler_params=_params(("parallel", "arbitrary")),
        name="norm_matmul_gates",
    )(x, g, w_t, wg_t, bg)


def _norm_matmul_gates_cast(x, g, w_t, wg_t, bg, *, n, tn):
    rows = x.shape[0]
    return pl.pallas_call(
        _norm_matmul_gates_cast_kernel,
        out_shape=(jax.ShapeDtypeStruct((rows, n), BF16),
                   jax.ShapeDtypeStruct((rows, GATE_LANES), F32),
                   jax.ShapeDtypeStruct((n, D), BF16)),
        grid=(1, n // tn),
        in_specs=[pl.BlockSpec((rows, D), lambda i, j: (0, 0), pipeline_mode=pl.Buffered(1)),
                  pl.BlockSpec((1, D), lambda i, j: (0, 0)),
                  pl.BlockSpec((None, tn, D), lambda i, j: (0, j, 0)),
                  pl.BlockSpec((GATE_LANES, D), lambda i, j: (0, 0)),
                  pl.BlockSpec((1, GATE_LANES), lambda i, j: (0, 0))],
        out_specs=(pl.BlockSpec((rows, tn), lambda i, j: (0, j)),
                   pl.BlockSpec((rows, GATE_LANES), lambda i, j: (0, 0)),
                   pl.BlockSpec((tn, D), lambda i, j: (j, 0))),
        scratch_shapes=[pltpu.VMEM((rows, D), BF16)],
        compiler_params=_params(("arbitrary", "arbitrary")),
        name="norm_matmul_gates_cast",
    )(x, g, w_t, wg_t, bg)


def _matmul_norm_res_kernel(a_ref, w_ref, g_ref, h_ref, o_ref, wb_ref):
    @pl.when(pl.program_id(0) == 0)
    def _():
        wb_ref[...] = w_ref[...].astype(BF16)

    y = _dot(a_ref[...], wb_ref[...])
    o_ref[...] = h_ref[...] + _rmsnorm(y, g_ref[...])


def _matmul_norm_res(a, w, g, h, *, tm):
    rows = a.shape[0]
    return pl.pallas_call(
        _matmul_norm_res_kernel,
        out_shape=jax.ShapeDtypeStruct((rows, D), F32),
        grid=(rows // tm,),
        in_specs=[pl.BlockSpec((tm, D), lambda i: (i, 0)),
                  pl.BlockSpec((None, D, D), lambda i: (0, 0, 0), pipeline_mode=pl.Buffered(1)),
                  pl.BlockSpec((1, D), lambda i: (0, 0)),
                  pl.BlockSpec((tm, D), lambda i: (i, 0))],
        out_specs=pl.BlockSpec((tm, D), lambda i: (i, 0)),
        scratch_shapes=[pltpu.VMEM((D, D), BF16)],
        compiler_params=_params(("arbitrary",)),
        name="matmul_norm_res",
    )(a, w, g, h)


def _ffn_chunk(f, last, h_ref, gpre_ref, wup_ref, wdown_ref, gpost_ref, o_ref, xn_ref, *, sub, sub_middle):
    tm = h_ref.shape[0]
    subs = [slice(r, min(r + sub, tm)) for r in range(0, tm, sub)]

    def mlp_chunk(xn):
        a = jnp.maximum(_dot(xn, wup_ref[...]), 0.0)
        return _dot((a * a).astype(BF16), wdown_ref[...])

    @pl.when(f == 0)
    def _():
        for rows in subs:
            xn = _rmsnorm(h_ref[rows, :], gpre_ref[...]).astype(BF16)
            xn_ref[rows, :] = xn
            o_ref[rows, :] = mlp_chunk(xn)

    @pl.when(jnp.logical_and(f > 0, f < last))
    def _():
        for rows in (subs if sub_middle else [slice(None)]):
            o_ref[rows, :] += mlp_chunk(xn_ref[rows, :])

    @pl.when(f == last)
    def _():
        for rows in subs:
            y = o_ref[rows, :] + mlp_chunk(xn_ref[rows, :])
            o_ref[rows, :] = h_ref[rows, :] + _rmsnorm(y, gpost_ref[...])


def _ffn_kernel(h_ref, gpre_ref, wup_ref, wdown_ref, gpost_ref, o_ref, xn_ref):
    _ffn_chunk(pl.program_id(1), pl.num_programs(1) - 1, h_ref, gpre_ref, wup_ref, wdown_ref, gpost_ref,
               o_ref, xn_ref, sub=SUB_ROWS, sub_middle=False)


def _ffn_cast_kernel(h_ref, gpre_ref, wup_ref, wdown_ref, gpost_ref, o_ref, wupb_ref, wdownb_ref, xn_ref):
    wupb_ref[...] = wup_ref[...].astype(BF16)
    wdownb_ref[...] = wdown_ref[...].astype(BF16)
    _ffn_chunk(pl.program_id(0), pl.num_programs(0) - 1, h_ref, gpre_ref, wupb_ref, wdownb_ref, gpost_ref,
               o_ref, xn_ref, sub=CAST_SUB_ROWS, sub_middle=True)


def _ffn(h, g_pre, w_up, w_down, g_post, *, tm, tf):
    n_tiles = h.shape[0] // tm
    return pl.pallas_call(
        _ffn_kernel,
        out_shape=jax.ShapeDtypeStruct((n_tiles * tm, D), F32),
        grid=(n_tiles, D_FF // tf),
        in_specs=[pl.BlockSpec((tm, D), lambda i, f: (i, 0)),
                  pl.BlockSpec((1, D), lambda i, f: (0, 0)),
                  pl.BlockSpec((D, tf), lambda i, f: (0, f)),
                  pl.BlockSpec((tf, D), lambda i, f: (f, 0)),
                  pl.BlockSpec((1, D), lambda i, f: (0, 0))],
        out_specs=pl.BlockSpec((tm, D), lambda i, f: (i, 0)),
        scratch_shapes=[pltpu.VMEM((tm, D), BF16)],
        compiler_params=_params(("parallel", "arbitrary")),
        name="ffn",
    )(h, g_pre, w_up, w_down, g_post)


def _ffn_cast(h, g_pre, w_up, w_down, g_post, *, layer, tf):
    rows = h.shape[0]
    single = pl.Buffered(1)
    return pl.pallas_call(
        _ffn_cast_kernel,
        out_shape=(jax.ShapeDtypeStruct((rows, D), F32),
                   jax.ShapeDtypeStruct((D, D_FF), BF16),
                   jax.ShapeDtypeStruct((D_FF, D), BF16)),
        grid=(D_FF // tf,),
        in_specs=[pl.BlockSpec((rows, D), lambda f: (0, 0), pipeline_mode=single),
                  pl.BlockSpec((1, D), lambda f: (0, 0)),
                  pl.BlockSpec((None, D, tf), lambda f: (layer, 0, f)),
                  pl.BlockSpec((None, tf, D), lambda f: (layer, f, 0)),
                  pl.BlockSpec((1, D), lambda f: (0, 0))],
        out_specs=(pl.BlockSpec((rows, D), lambda f: (0, 0), pipeline_mode=single),
                   pl.BlockSpec((D, tf), lambda f: (0, f)),
                   pl.BlockSpec((tf, D), lambda f: (f, 0))),
        scratch_shapes=[pltpu.VMEM((rows, D), BF16)],
        compiler_params=_params(("arbitrary",)),
        name="ffn_cast",
    )(h, g_pre, w_up, w_down, g_post)


def _mlstm_kernel(qk_ref, v_ref, o_ref, gcol_ref, grow_ref, hg_ref, c0_ref, n0_ref, m0_ref,
                  out_ref, c_ref, n_ref, m_ref, *, seqs, length, single_chunk):
    if single_chunk:
        c_in, n_in, m_in = c0_ref, n0_ref, m0_ref
    else:
        c_in, n_in, m_in = c_ref, n_ref, m_ref

        @pl.when(pl.program_id(1) == 0)
        def _():
            c_ref[...] = c0_ref[...]
            n_ref[...] = n0_ref[...]
            m_ref[...] = m0_ref[...]

    pairs = [(s, hd) for s in range(seqs) for hd in range(HEADS)]
    L = length
    t_idx = lax.broadcasted_iota(jnp.int32, (L, L), 0)
    s_idx = lax.broadcasted_iota(jnp.int32, (L, L), 1)
    causal = s_idx <= t_idx

    def rows_of(s):
        return slice(s * L, (s + 1) * L)

    def q_of(s, hd):
        return qk_ref[rows_of(s), hd * DQK:(hd + 1) * DQK]

    def k_of(s, hd):
        return qk_ref[rows_of(s), (HEADS + hd) * DQK:(HEADS + hd + 1) * DQK]

    def v_of(s, hd):
        return v_ref[rows_of(s), hd * DV:(hd + 1) * DV]

    qk = [_dot_t(q_of(s, hd), k_of(s, hd)) for s, hd in pairs]
    qc = [_dot(q_of(s, hd), c_in[s, hd].astype(BF16)) for s, hd in pairs]

    gate_vals = []
    for s, hd in pairs:
        gcol = gcol_ref[rows_of(s), :]
        grow = grow_ref[0, s]
        li_row, lf_row = grow[hd:hd + 1, :], grow[HEADS + hd:HEADS + hd + 1, :]
        li_col, lf_col = gcol[:, hd:hd + 1], gcol[:, HEADS + hd:HEADS + hd + 1]
        m = m_in[s, hd:hd + 1, 0:1]
        b_col = jnp.sum(jnp.where(causal, lf_row, 0.0), axis=1, keepdims=True)
        b_row = jnp.sum(jnp.where(t_idx <= s_idx, lf_col, 0.0), axis=0, keepdims=True)
        b_tot = b_col[L - 1:L, :]
        dmat = jnp.where(causal, b_col - b_row + li_row, -jnp.inf)
        m_inter = b_col + m
        m_t = jnp.maximum(m_inter, jnp.max(dmat, axis=1, keepdims=True))
        p = jnp.exp(dmat - m_t)
        w_inter = jnp.exp(m_inter - m_t)
        m_new = m_t[L - 1:L, :]
        decay = jnp.exp(b_tot - b_col + li_col - m_new)
        carry = jnp.exp(b_tot + m - m_new)
        gate_vals.append((p, w_inter, m_t, m_new, decay, carry))

    scores = [qk[i] * (gate_vals[i][0] * K_SCALE) for i in range(len(pairs))]
    kd = [k_of(s, hd).astype(F32) * (gate_vals[i][4] * K_SCALE) for i, (s, hd) in enumerate(pairs)]
    sv = [_dot(scores[i].astype(BF16), v_of(s, hd)) for i, (s, hd) in enumerate(pairs)]
    kv = [lax.dot_general(kd[i].astype(BF16), v_of(s, hd), (((0,), (0,)), ((), ())),
                          preferred_element_type=F32) for i, (s, hd) in enumerate(pairs)]

    for i, (s, hd) in enumerate(pairs):
        p, w_inter, m_t, m_new, decay, carry = gate_vals[i]
        n_row = n_in[s, hd:hd + 1, :]
        c_new = carry * c_in[s, hd] + kv[i]
        n_new = carry * n_row + jnp.sum(kd[i], axis=0, keepdims=True)
        num = sv[i] + qc[i] * w_inter
        qn = jnp.sum(q_of(s, hd).astype(F32) * n_row, axis=1, keepdims=True)
        den = jnp.sum(scores[i], axis=1, keepdims=True) + w_inter * qn
        h = num / jnp.maximum(jnp.abs(den), jnp.exp(-m_t))
        c_ref[s, hd] = c_new
        n_ref[s, hd:hd + 1, :] = n_new
        m_ref[s, hd:hd + 1, :] = jnp.broadcast_to(m_new, (1, GATE_LANES))
        hn = _rmsnorm(h, hg_ref[:, hd * DV:(hd + 1) * DV])
        o_pre = o_ref[rows_of(s), hd * DV:(hd + 1) * DV].astype(F32)
        gate = 0.5 * jnp.tanh(0.5 * o_pre) + 0.5
        out_ref[rows_of(s), hd * DV:(hd + 1) * DV] = (hn * gate).astype(out_ref.dtype)


def _mlstm(proj, gcol, grow, head_g, c0, n0, m0, *, row0, n_blocks, seqs, length, n_chunks):
    assert seqs == 1 or n_chunks == 1
    blk = seqs * length
    assert row0 % blk == 0
    b0 = row0 // blk
    n_seq = n_blocks * seqs

    def rows_map(col):
        return lambda b, c: (b0 + b * n_chunks + c, col)

    kernel = functools.partial(_mlstm_kernel, seqs=seqs, length=length, single_chunk=n_chunks == 1)
    return pl.pallas_call(
        kernel,
        out_shape=(jax.ShapeDtypeStruct((n_blocks * n_chunks * blk, HEADS * DV), BF16),
                   jax.ShapeDtypeStruct((n_seq, HEADS, DQK, DV), F32),
                   jax.ShapeDtypeStruct((n_seq, HEADS, DQK), F32),
                   jax.ShapeDtypeStruct((n_seq, HEADS, GATE_LANES), F32)),
        grid=(n_blocks, n_chunks),
        in_specs=[pl.BlockSpec((blk, 2 * HEADS * DQK), rows_map(0)),
                  pl.BlockSpec((blk, HEADS * DV), rows_map(1)),
                  pl.BlockSpec((blk, HEADS * DV), rows_map(2)),
                  pl.BlockSpec((blk, GATE_LANES), rows_map(0)),
                  pl.BlockSpec((1, seqs, 2 * HEADS, length), lambda b, c: (b * n_chunks + c, 0, 0, 0)),
                  pl.BlockSpec((1, HEADS * DV), lambda b, c: (0, 0)),
                  pl.BlockSpec((seqs, HEADS, DQK, DV), lambda b, c: (b, 0, 0, 0)),
                  pl.BlockSpec((seqs, HEADS, DQK), lambda b, c: (b, 0, 0)),
                  pl.BlockSpec((seqs, HEADS, GATE_LANES), lambda b, c: (b, 0, 0))],
        out_specs=(pl.BlockSpec((blk, HEADS * DV), lambda b, c: (b * n_chunks + c, 0)),
                   pl.BlockSpec((seqs, HEADS, DQK, DV), lambda b, c: (b, 0, 0, 0)),
                   pl.BlockSpec((seqs, HEADS, DQK), lambda b, c: (b, 0, 0)),
                   pl.BlockSpec((seqs, HEADS, GATE_LANES), lambda b, c: (b, 0, 0))),
        compiler_params=_params(("parallel", "arbitrary")),
        name=f"mlstm_l{length}",
    )(proj, proj, proj, gcol, grow, head_g, c0, n0, m0)


def _pool_mix(ext, first, n_rows, pos, wg_ref, scale_ref):
    outs = []
    for g, w in enumerate(POOL_WINDOWS):
        cols = slice(g * POOL_GROUP_DIM, (g + 1) * POOL_GROUP_DIM)
        e = ext[:, cols]
        total, span = e, 1
        while span < w:
            total = total + pltpu.roll(total, span, 0)
            span *= 2
        cnt = jnp.minimum(float(w), pos + 1.0)
        pooled = total[first:first + n_rows] / cnt - e[first:first + n_rows]
        outs.append(_dot(pooled.astype(BF16), wg_ref[g].astype(BF16)))
    mixed = jnp.concatenate(outs, axis=-1) * scale_ref[...]
    return mixed.astype(BF16)


def _pool_rows_kernel(u_ref, prev_ref, first_ref, wg_ref, scale_ref, o_ref, *, tiles_per_seq, pos0):
    tr = u_ref.shape[0]
    tile = pl.program_id(0) % tiles_per_seq
    halo = jnp.where(tile == 0, first_ref[...], prev_ref[...])
    ext = jnp.concatenate([halo, u_ref[...]], axis=0)
    pos = (lax.broadcasted_iota(jnp.int32, (tr, 1), 0) + (tile * tr + pos0)).astype(F32)
    o_ref[...] = _pool_mix(ext, N_META, tr, pos, wg_ref, scale_ref)


def _pool_rows(u, first_src, w_group, scale, *, row0, n_tiles, tr, tiles_per_seq, pos0, first_block):
    hb = tr // N_META
    t0 = row0 // tr
    kernel = functools.partial(_pool_rows_kernel, tiles_per_seq=tiles_per_seq, pos0=pos0)
    return pl.pallas_call(
        kernel,
        out_shape=jax.ShapeDtypeStruct((n_tiles * tr, D), BF16),
        grid=(n_tiles,),
        in_specs=[pl.BlockSpec((tr, D), lambda i: (t0 + i, 0)),
                  pl.BlockSpec((N_META, D), lambda i: (jnp.maximum((t0 + i) * hb - 1, 0), 0)),
                  pl.BlockSpec((N_META, D), lambda i: (first_block, 0)),
                  pl.BlockSpec((None, len(POOL_WINDOWS), POOL_GROUP_DIM, POOL_GROUP_DIM),
                               lambda i: (0, 0, 0, 0)),
                  pl.BlockSpec((1, D), lambda i: (0, 0))],
        out_specs=pl.BlockSpec((tr, D), lambda i: (i, 0)),
        compiler_params=_params(("parallel",)),
        name=f"pool_rows_{tr}",
    )(u, u, first_src, w_group, scale)


def _pool_seqs_kernel(prefix_ref, u_ref, wg_ref, scale_ref, o_ref, buf_ref, *, pos0):
    g = prefix_ref.shape[0]
    n_new = u_ref.shape[0] // g
    r = 1 + POOL_BUF + n_new
    prefix = prefix_ref[...]
    u3 = u_ref[...].reshape(g, n_new, D)
    ext = jnp.concatenate([jnp.zeros((g, 1, D), F32), prefix, u3], axis=1).reshape(g * r, D)
    buf_ref[...] = jnp.concatenate([prefix, u3], axis=1)[:, n_new:, :]
    outs = []
    pos = (lax.broadcasted_iota(jnp.int32, (g, n_new, 1), 1) + pos0).astype(F32)
    for gi, w in enumerate(POOL_WINDOWS):
        cols = slice(gi * POOL_GROUP_DIM, (gi + 1) * POOL_GROUP_DIM)
        e = ext[:, cols]
        total, span = e, 1
        while span < w:
            total = total + pltpu.roll(total, span, 0)
            span *= 2
        cnt = jnp.minimum(float(w), pos + 1.0)
        tot3 = total.reshape(g, r, POOL_GROUP_DIM)[:, 1 + POOL_BUF:, :]
        e3 = e.reshape(g, r, POOL_GROUP_DIM)[:, 1 + POOL_BUF:, :]
        pooled = (tot3 / cnt - e3).reshape(g * n_new, POOL_GROUP_DIM)
        outs.append(_dot(pooled.astype(BF16), wg_ref[gi].astype(BF16)))
    mixed = jnp.concatenate(outs, axis=-1) * scale_ref[...]
    o_ref[...] = mixed.astype(BF16)


def _pool_seqs(prefix, u, w_group, scale, *, n_new, seqs_per_block, pos0):
    n_seq = prefix.shape[1]
    kernel = functools.partial(_pool_seqs_kernel, pos0=pos0)
    return pl.pallas_call(
        kernel,
        out_shape=(jax.ShapeDtypeStruct((n_seq * n_new, D), BF16),
                   jax.ShapeDtypeStruct((n_seq, POOL_BUF, D), F32)),
        grid=(n_seq // seqs_per_block,),
        in_specs=[pl.BlockSpec((None, seqs_per_block, POOL_BUF, D), lambda i: (0, i, 0, 0)),
                  pl.BlockSpec((seqs_per_block * n_new, D), lambda i: (i, 0)),
                  pl.BlockSpec((None, len(POOL_WINDOWS), POOL_GROUP_DIM, POOL_GROUP_DIM),
                               lambda i: (0, 0, 0, 0)),
                  pl.BlockSpec((1, D), lambda i: (0, 0))],
        out_specs=(pl.BlockSpec((seqs_per_block * n_new, D), lambda i: (i, 0)),
                   pl.BlockSpec((seqs_per_block, POOL_BUF, D), lambda i: (i, 0, 0))),
        compiler_params=_params(("parallel",)),
        name="pool_seqs",
    )(prefix, u, w_group, scale)


def kernel(x_prompt, x_sample, state_mlstm_c, state_mlstm_n, state_mlstm_m, state_pool, meta_tokens,
           norm_mix_pre, norm_mix_post, norm_ffn_pre, norm_ffn_post, mlstm_w_in, mlstm_b_i, mlstm_b_f,
           mlstm_head_norm, mlstm_w_out, pool_w_in, pool_w_group, pool_scale, pool_w_out,
           ffn_w_up, ffn_w_down):
    B, S, _ = x_prompt.shape
    DB, DS, _ = x_sample.shape
    n_p, n_s = B * S, DB * DS
    rows_a = 2 * TM_AUX
    row_m = n_s
    assert n_s + N_META <= rows_a and n_p % TM_PROJ == 0 and n_p % TM == 0
    assert S % CHUNK == 0 and S % POOL_TR == 0 and row_m % N_META == 0

    h_p = x_prompt.reshape(n_p, D)
    h_a = jnp.concatenate([x_sample.reshape(n_s, D), meta_tokens.astype(F32),
                           jnp.zeros((rows_a - n_s - N_META, D), F32)], axis=0)
    pad_a = jnp.zeros((rows_a - n_s - N_META, D), BF16)

    def gain(a):
        return a.reshape(1, D).astype(F32)


    def matmul_norm_res(a_p, a_a, w, g):
        return _matmul_norm_res(a_p, w, g, h_p, tm=TM), _matmul_norm_res(a_a, w, g, h_a, tm=TM_AUX)

    def ffn(layer):
        g_pre, g_post = gain(norm_ffn_pre[layer]), gain(norm_ffn_post[layer])
        out_a, w_up, w_down = _ffn_cast(h_a, g_pre, ffn_w_up.astype(F32), ffn_w_down.astype(F32), g_post,
                                        layer=layer, tf=TF_AUX)
        return _ffn(h_p, g_pre, w_up, w_down, g_post, tm=TM_FFN, tf=TF), out_a

    w_in_t = jnp.swapaxes(mlstm_w_in[:1].astype(F32), 1, 2)
    w_gate_t = jnp.pad(w_in_t[0, PROJ_MAIN:], ((0, GATE_LANES - 2 * HEADS), (0, 0))).astype(BF16)
    b_gate = jnp.pad(jnp.concatenate([mlstm_b_i[0], mlstm_b_f[0]]).astype(F32),
                     (0, GATE_LANES - 2 * HEADS)).reshape(1, GATE_LANES)
    g_mix = gain(norm_mix_pre[0])
    proj_a, gcol_a, w_in_tb = _norm_matmul_gates_cast(h_a, g_mix, w_in_t, w_gate_t, b_gate, n=PROJ_MAIN,
                                                      tn=TN_PROJ)
    proj_p, gcol_p = _norm_matmul_gates(h_p, g_mix, w_in_tb, w_gate_t, b_gate, tm=TM_PROJ, tn=2 * TN_PROJ)

    def gate_rows(gcol, r0, n_blk, seqs, length):
        g8 = gcol[r0:r0 + n_blk * seqs * length, :2 * HEADS]
        return g8.reshape(n_blk, seqs, length, 2 * HEADS).transpose(0, 1, 3, 2)

    head_g = mlstm_head_norm[0].reshape(1, HEADS * DV).astype(F32)
    zc = jnp.zeros((1, HEADS, DQK, DV), F32)
    zn = jnp.zeros((1, HEADS, DQK), F32)
    zm = jnp.zeros((1, HEADS, GATE_LANES), F32)
    mix_m, c_m, n_m, m_m = _mlstm(proj_a, gcol_a, gate_rows(gcol_a, row_m, 1, 1, N_META), head_g, zc, zn, zm,
                                  row0=row_m, n_blocks=1, seqs=1, length=N_META, n_chunks=1)
    mix_p, c_p, n_p_, m_p = _mlstm(proj_p, gcol_p, gate_rows(gcol_p, 0, B * (S // CHUNK), 1, CHUNK), head_g,
                                   jnp.broadcast_to(c_m, (B,) + c_m.shape[1:]),
                                   jnp.broadcast_to(n_m, (B,) + n_m.shape[1:]),
                                   jnp.broadcast_to(m_m, (B,) + m_m.shape[1:]),
                                   row0=0, n_blocks=B, seqs=1, length=CHUNK, n_chunks=S // CHUNK)
    m0_s = jnp.broadcast_to(state_mlstm_m[0].astype(F32)[:, :, None], (DB, HEADS, GATE_LANES))
    mix_s, c_s, n_s_, m_s = _mlstm(proj_a, gcol_a, gate_rows(gcol_a, 0, DB // SAMPLE_SEQS, SAMPLE_SEQS, DS),
                                   head_g, state_mlstm_c[0].astype(F32), state_mlstm_n[0].astype(F32), m0_s,
                                   row0=0, n_blocks=DB // SAMPLE_SEQS, seqs=SAMPLE_SEQS, length=DS, n_chunks=1)
    mix_a = jnp.concatenate([mix_s, mix_m, pad_a], axis=0)
    h_p, h_a = matmul_norm_res(mix_p, mix_a, mlstm_w_out[:1].astype(F32), gain(norm_mix_post[0]))
    h_p, h_a = ffn(0)

    w_pool_in, g_mix = pool_w_in[:1].astype(F32), gain(norm_mix_pre[1])
    u_p = _norm_matmul(h_p, g_mix, w_pool_in, tm=TM, out_dtype=F32)
    u_a = _norm_matmul(h_a, g_mix, w_pool_in, tm=TM_AUX, out_dtype=F32)
    w_group = pool_w_group[:1].astype(F32)
    p_scale = pool_scale[0].reshape(1, D).astype(F32)
    pmix_m = _pool_rows(u_a, jnp.zeros((N_META, D), F32), w_group, p_scale, row0=row_m, n_tiles=1,
                        tr=N_META, tiles_per_seq=1, pos0=0, first_block=0)
    pmix_p = _pool_rows(u_p, u_a, w_group, p_scale, row0=0, n_tiles=n_p // POOL_TR, tr=POOL_TR,
                        tiles_per_seq=S // POOL_TR, pos0=N_META, first_block=row_m // N_META)
    pmix_s, pool_s = _pool_seqs(state_pool[:1].astype(F32), u_a, w_group, p_scale, n_new=DS,
                                seqs_per_block=POOL_SEQS, pos0=PAST_LEN)
    pmix_a = jnp.concatenate([pmix_s, pmix_m, pad_a], axis=0)
    h_p, h_a = matmul_norm_res(pmix_p, pmix_a, pool_w_out[:1].astype(F32), gain(norm_mix_post[1]))
    h_p, h_a = ffn(1)

    y_prompt = h_p.reshape(B, S, D)
    y_sample = h_a[:n_s].reshape(DB, DS, D)
    dt_c, dt_n, dt_m, dt_pool = state_mlstm_c.dtype, state_mlstm_n.dtype, state_mlstm_m.dtype, state_pool.dtype
    u_prompt = u_p.reshape(B, S, D)
    return (y_prompt, y_sample,
            c_p[None].astype(dt_c), n_p_[None].astype(dt_n), m_p[None, :, :, 0].astype(dt_m),
            u_prompt[None, :, S - POOL_BUF:].astype(dt_pool),
            c_s[None].astype(dt_c), n_s_[None].astype(dt_n), m_s[None, :, :, 0].astype(dt_m),
            pool_s[None].astype(dt_pool))
```

```python
import functools

import jax
import jax.numpy as jnp
from jax import lax
from jax.experimental import pallas as pl
from jax.experimental.pallas import tpu as pltpu

D = 2048
N_META = 16
HEADS = 4
DQK = 256
DV = 512
D_FF = 4 * D
POOL_WINDOWS = (2, 4, 8, 16)
POOL_GROUP_DIM = D // len(POOL_WINDOWS)
POOL_BUF = 15
PAST_LEN = 16384
EPS = 1e-6
K_SCALE = DQK ** -0.5
PROJ_MAIN = 2 * HEADS * DQK + 2 * HEADS * DV
GATE_LANES = 128

VMEM_LIMIT = 56 * 1024 * 1024

F32 = jnp.float32
BF16 = jnp.bfloat16

TM_PROJ = 1024
TN_PROJ = 1024
TM = 512
TM_AUX = 544
TM_FFN = 512
TF = 1024
TF_AUX = 512
SUB_ROWS = 256
CAST_SUB_ROWS = TM_AUX // 2
CHUNK = 512
SAMPLE_SEQS = 4
POOL_TR = 512
POOL_SEQS = 16


def _params(semantics):
    return pltpu.CompilerParams(dimension_semantics=semantics, vmem_limit_bytes=VMEM_LIMIT)


def _rmsnorm(x, g):
    return x * lax.rsqrt(jnp.mean(x * x, axis=-1, keepdims=True) + EPS) * g


def _dot(a, b):
    return jnp.dot(a, b, preferred_element_type=F32)


def _norm_matmul_kernel(x_ref, g_ref, w_ref, o_ref, wb_ref):
    @pl.when(pl.program_id(0) == 0)
    def _():
        wb_ref[...] = w_ref[...].astype(BF16)

    for r in range(0, x_ref.shape[0], SUB_ROWS):
        rows = slice(r, min(r + SUB_ROWS, x_ref.shape[0]))
        xn = _rmsnorm(x_ref[rows, :], g_ref[...]).astype(BF16)
        o_ref[rows, :] = _dot(xn, wb_ref[...]).astype(o_ref.dtype)


def _dot_t(a, b):
    return lax.dot_general(a, b, (((1,), (1,)), ((), ())), preferred_element_type=F32)


def _in_proj_first_block(x_ref, g_ref, w_t, wgt_ref, bg_ref, o_ref, gate_ref, xn_ref):
    tm = x_ref.shape[0]
    for r in range(0, tm, SUB_ROWS):
        rows = slice(r, min(r + SUB_ROWS, tm))
        xn = _rmsnorm(x_ref[rows, :], g_ref[...]).astype(BF16)
        xn_ref[rows, :] = xn
        z = _dot_t(xn, wgt_ref[...]) + bg_ref[...]
        lane = lax.broadcasted_iota(jnp.int32, z.shape, 1)
        log_sig = jnp.minimum(z, 0.0) - jnp.log1p(jnp.exp(-jnp.abs(z)))
        gate_ref[rows, :] = jnp.where(lane >= HEADS, log_sig, z)
        o_ref[rows, :] = _dot_t(xn, w_t).astype(o_ref.dtype)


def _norm_matmul_gates_kernel(x_ref, g_ref, wt_ref, wgt_ref, bg_ref, o_ref, gate_ref, xn_ref):
    @pl.when(pl.program_id(1) == 0)
    def _():
        _in_proj_first_block(x_ref, g_ref, wt_ref[...], wgt_ref, bg_ref, o_ref, gate_ref, xn_ref)

    @pl.when(pl.program_id(1) > 0)
    def _():
        o_ref[...] = _dot_t(xn_ref[...], wt_ref[...]).astype(o_ref.dtype)


def _norm_matmul_gates_cast_kernel(x_ref, g_ref, wt_ref, wgt_ref, bg_ref, o_ref, gate_ref, wtb_ref, xn_ref):
    wtb_ref[...] = wt_ref[...].astype(BF16)

    @pl.when(pl.program_id(1) == 0)
    def _():
        _in_proj_first_block(x_ref, g_ref, wtb_ref[...], wgt_ref, bg_ref, o_ref, gate_ref, xn_ref)

    @pl.when(pl.program_id(1) > 0)
    def _():
        o_ref[...] = _dot_t(xn_ref[...], wtb_ref[...]).astype(o_ref.dtype)


def _norm_matmul(x, g, w, *, tm, out_dtype):
    rows = x.shape[0]
    return pl.pallas_call(
        _norm_matmul_kernel,
        out_shape=jax.ShapeDtypeStruct((rows, D), out_dtype),
        grid=(rows // tm,),
        in_specs=[pl.BlockSpec((tm, D), lambda i: (i, 0)),
                  pl.BlockSpec((1, D), lambda i: (0, 0)),
                  pl.BlockSpec((None, D, D), lambda i: (0, 0, 0), pipeline_mode=pl.Buffered(1))],
        out_specs=pl.BlockSpec((tm, D), lambda i: (i, 0)),
        scratch_shapes=[pltpu.VMEM((D, D), BF16)],
        compiler_params=_params(("arbitrary",)),
        name="norm_matmul",
    )(x, g, w)


def _norm_matmul_gates(x, g, w_t, wg_t, bg, *, tm, tn):
    rows, n = x.shape[0], w_t.shape[0]
    return pl.pallas_call(
        _norm_matmul_gates_kernel,
        out_shape=(jax.ShapeDtypeStruct((rows, n), BF16),
                   jax.ShapeDtypeStruct((rows, GATE_LANES), F32)),
        grid=(rows // tm, n // tn),
        in_specs=[pl.BlockSpec((tm, D), lambda i, j: (i, 0)),
                  pl.BlockSpec((1, D), lambda i, j: (0, 0)),
                  pl.BlockSpec((tn, D), lambda i, j: (j, 0)),
                  pl.BlockSpec((GATE_LANES, D), lambda i, j: (0, 0)),
                  pl.BlockSpec((1, GATE_LANES), lambda i, j: (0, 0))],
        out_specs=(pl.BlockSpec((tm, tn), lambda i, j: (i, j)),
                   pl.BlockSpec((tm, GATE_LANES), lambda i, j: (i, 0))),
        scratch_shapes=[pltpu.VMEM((tm, D), BF16)],
        compiler_params=_params(("parallel", "arbitrary")),
        name="norm_matmul_gates",
    )(x, g, w_t, wg_t, bg)


def _norm_matmul_gates_cast(x, g, w_t, wg_t, bg, *, n, tn):
    rows = x.shape[0]
    return pl.pallas_call(
        _norm_matmul_gates_cast_kernel,
        out_shape=(jax.ShapeDtypeStruct((rows, n), BF16),
                   jax.ShapeDtypeStruct((rows, GATE_LANES), F32),
                   jax.ShapeDtypeStruct((n, D), BF16)),
        grid=(1, n // tn),
        in_specs=[pl.BlockSpec((rows, D), lambda i, j: (0, 0), pipeline_mode=pl.Buffered(1)),
                  pl.BlockSpec((1, D), lambda i, j: (0, 0)),
                  pl.BlockSpec((None, tn, D), lambda i, j: (0, j, 0)),
                  pl.BlockSpec((GATE_LANES, D), lambda i, j: (0, 0)),
                  pl.BlockSpec((1, GATE_LANES), lambda i, j: (0, 0))],
        out_specs=(pl.BlockSpec((rows, tn), lambda i, j: (0, j)),
                   pl.BlockSpec((rows, GATE_LANES), lambda i, j: (0, 0)),
                   pl.BlockSpec((tn, D), lambda i, j: (j, 0))),
        scratch_shapes=[pltpu.VMEM((rows, D), BF16)],
        compiler_params=_params(("arbitrary", "arbitrary")),
        name="norm_matmul_gates_cast",
    )(x, g, w_t, wg_t, bg)


def _matmul_norm_res_kernel(a_ref, w_ref, g_ref, h_ref, o_ref, wb_ref):
    @pl.when(pl.program_id(0) == 0)
    def _():
        wb_ref[...] = w_ref[...].astype(BF16)

    y = _dot(a_ref[...], wb_ref[...])
    o_ref[...] = h_ref[...] + _rmsnorm(y, g_ref[...])


def _gated_matmul_norm_res_kernel(hm_ref, og_ref, hg_ref, w_ref, g_ref, h_ref, o_ref, wb_ref):
    @pl.when(pl.program_id(0) == 0)
    def _():
        wb_ref[...] = w_ref[...].astype(BF16)

    tm = hm_ref.shape[0]
    for r in range(0, tm, SUB_ROWS):
        rows = slice(r, min(r + SUB_ROWS, tm))
        parts = []
        for hd in range(HEADS):
            cols = slice(hd * DV, (hd + 1) * DV)
            hn = _rmsnorm(hm_ref[rows, cols], hg_ref[:, cols])
            gate = 0.5 * jnp.tanh(0.5 * og_ref[rows, cols].astype(F32)) + 0.5
            parts.append((hn * gate).astype(BF16))
        y = _dot(jnp.concatenate(parts, axis=-1), wb_ref[...])
        o_ref[rows, :] = h_ref[rows, :] + _rmsnorm(y, g_ref[...])


def _gated_matmul_norm_res(hm, proj, head_g, w, g, h, *, tm):
    rows = hm.shape[0]
    return pl.pallas_call(
        _gated_matmul_norm_res_kernel,
        out_shape=jax.ShapeDtypeStruct((rows, D), F32),
        grid=(rows // tm,),
        in_specs=[pl.BlockSpec((tm, HEADS * DV), lambda i: (i, 0)),
                  pl.BlockSpec((tm, HEADS * DV), lambda i: (i, 2)),
                  pl.BlockSpec((1, HEADS * DV), lambda i: (0, 0)),
                  pl.BlockSpec((None, D, D), lambda i: (0, 0, 0), pipeline_mode=pl.Buffered(1)),
                  pl.BlockSpec((1, D), lambda i: (0, 0)),
                  pl.BlockSpec((tm, D), lambda i: (i, 0))],
        out_specs=pl.BlockSpec((tm, D), lambda i: (i, 0)),
        scratch_shapes=[pltpu.VMEM((D, D), BF16)],
        compiler_params=_params(("arbitrary",)),
        name="gated_matmul_norm_res",
    )(hm, proj, head_g, w, g, h)


def _matmul_norm_res(a, w, g, h, *, tm):
    rows = a.shape[0]
    return pl.pallas_call(
        _matmul_norm_res_kernel,
        out_shape=jax.ShapeDtypeStruct((rows, D), F32),
        grid=(rows // tm,),
        in_specs=[pl.BlockSpec((tm, D), lambda i: (i, 0)),
                  pl.BlockSpec((None, D, D), lambda i: (0, 0, 0), pipeline_mode=pl.Buffered(1)),
                  pl.BlockSpec((1, D), lambda i: (0, 0)),
                  pl.BlockSpec((tm, D), lambda i: (i, 0))],
        out_specs=pl.BlockSpec((tm, D), lambda i: (i, 0)),
        scratch_shapes=[pltpu.VMEM((D, D), BF16)],
        compiler_params=_params(("arbitrary",)),
        name="matmul_norm_res",
    )(a, w, g, h)


def _ffn_chunk(f, last, h_ref, gpre_ref, wup_ref, wdown_ref, gpost_ref, o_ref, xn_ref, *, sub, sub_middle):
    tm = h_ref.shape[0]
    subs = [slice(r, min(r + sub, tm)) for r in range(0, tm, sub)]

    def mlp_chunk(xn):
        a = jnp.maximum(_dot(xn, wup_ref[...]), 0.0)
        return _dot((a * a).astype(BF16), wdown_ref[...])

    @pl.when(f == 0)
    def _():
        for rows in subs:
            xn = _rmsnorm(h_ref[rows, :], gpre_ref[...]).astype(BF16)
            xn_ref[rows, :] = xn
            o_ref[rows, :] = mlp_chunk(xn)

    @pl.when(jnp.logical_and(f > 0, f < last))
    def _():
        for rows in (subs if sub_middle else [slice(None)]):
            o_ref[rows, :] += mlp_chunk(xn_ref[rows, :])

    @pl.when(f == last)
    def _():
        for rows in subs:
            y = o_ref[rows, :] + mlp_chunk(xn_ref[rows, :])
            o_ref[rows, :] = h_ref[rows, :] + _rmsnorm(y, gpost_ref[...])


def _ffn_kernel(h_ref, gpre_ref, wup_ref, wdown_ref, gpost_ref, o_ref, xn_ref):
    _ffn_chunk(pl.program_id(1), pl.num_programs(1) - 1, h_ref, gpre_ref, wup_ref, wdown_ref, gpost_ref,
               o_ref, xn_ref, sub=SUB_ROWS, sub_middle=False)


def _ffn_cast_kernel(h_ref, gpre_ref, wup_ref, wdown_ref, gpost_ref, o_ref, wupb_ref, wdownb_ref, xn_ref):
    wupb_ref[...] = wup_ref[...].astype(BF16)
    wdownb_ref[...] = wdown_ref[...].astype(BF16)
    _ffn_chunk(pl.program_id(0), pl.num_programs(0) - 1, h_ref, gpre_ref, wupb_ref, wdownb_ref, gpost_ref,
               o_ref, xn_ref, sub=CAST_SUB_ROWS, sub_middle=True)


def _ffn(h, g_pre, w_up, w_down, g_post, *, tm, tf):
    n_tiles = h.shape[0] // tm
    return pl.pallas_call(
        _ffn_kernel,
        out_shape=jax.ShapeDtypeStruct((n_tiles * tm, D), F32),
        grid=(n_tiles, D_FF // tf),
        in_specs=[pl.BlockSpec((tm, D), lambda i, f: (i, 0)),
                  pl.BlockSpec((1, D), lambda i, f: (0, 0)),
                  pl.BlockSpec((D, tf), lambda i, f: (0, f)),
                  pl.BlockSpec((tf, D), lambda i, f: (f, 0)),
                  pl.BlockSpec((1, D), lambda i, f: (0, 0))],
        out_specs=pl.BlockSpec((tm, D), lambda i, f: (i, 0)),
        scratch_shapes=[pltpu.VMEM((tm, D), BF16)],
        compiler_params=_params(("parallel", "arbitrary")),
        name="ffn",
    )(h, g_pre, w_up, w_down, g_post)


def _ffn_cast(h, g_pre, w_up, w_down, g_post, *, layer, tf):
    rows = h.shape[0]
    single = pl.Buffered(1)
    return pl.pallas_call(
        _ffn_cast_kernel,
        out_shape=(jax.ShapeDtypeStruct((rows, D), F32),
                   jax.ShapeDtypeStruct((D, D_FF), BF16),
                   jax.ShapeDtypeStruct((D_FF, D), BF16)),
        grid=(D_FF // tf,),
        in_specs=[pl.BlockSpec((rows, D), lambda f: (0, 0), pipeline_mode=single),
                  pl.BlockSpec((1, D), lambda f: (0, 0)),
                  pl.BlockSpec((None, D, tf), lambda f: (layer, 0, f)),
                  pl.BlockSpec((None, tf, D), lambda f: (layer, f, 0)),
                  pl.BlockSpec((1, D), lambda f: (0, 0))],
        out_specs=(pl.BlockSpec((rows, D), lambda f: (0, 0), pipeline_mode=single),
                   pl.BlockSpec((D, tf), lambda f: (0, f)),
                   pl.BlockSpec((tf, D), lambda f: (f, 0))),
        scratch_shapes=[pltpu.VMEM((rows, D), BF16)],
        compiler_params=_params(("arbitrary",)),
        name="ffn_cast",
    )(h, g_pre, w_up, w_down, g_post)


def _mlstm_kernel(qk_ref, v_ref, gcol_ref, grow_ref, c0_ref, n0_ref, m0_ref,
                  out_ref, c_ref, n_ref, m_ref, *, seqs, length, single_chunk):
    if single_chunk:
        c_in, n_in, m_in = c0_ref, n0_ref, m0_ref
    else:
        c_in, n_in, m_in = c_ref, n_ref, m_ref

        @pl.when(pl.program_id(1) == 0)
        def _():
            c_ref[...] = c0_ref[...]
            n_ref[...] = n0_ref[...]
            m_ref[...] = m0_ref[...]

    pairs = [(s, hd) for s in range(seqs) for hd in range(HEADS)]
    L = length
    t_idx = lax.broadcasted_iota(jnp.int32, (L, L), 0)
    s_idx = lax.broadcasted_iota(jnp.int32, (L, L), 1)
    causal = s_idx <= t_idx

    def rows_of(s):
        return slice(s * L, (s + 1) * L)

    def q_of(s, hd):
        return qk_ref[rows_of(s), hd * DQK:(hd + 1) * DQK]

    def k_of(s, hd):
        return qk_ref[rows_of(s), (HEADS + hd) * DQK:(HEADS + hd + 1) * DQK]

    def v_of(s, hd):
        return v_ref[rows_of(s), hd * DV:(hd + 1) * DV]

    qk = [_dot_t(q_of(s, hd), k_of(s, hd)) for s, hd in pairs]
    qc = [_dot(q_of(s, hd), c_in[s, hd].astype(BF16)) for s, hd in pairs]

    gate_vals = []
    for s, hd in pairs:
        gcol = gcol_ref[rows_of(s), :]
        grow = grow_ref[0, s]
        li_row, lf_row = grow[hd:hd + 1, :], grow[HEADS + hd:HEADS + hd + 1, :]
        li_col, lf_col = gcol[:, hd:hd + 1], gcol[:, HEADS + hd:HEADS + hd + 1]
        m = m_in[s, hd:hd + 1, 0:1]
        b_col = jnp.sum(jnp.where(causal, lf_row, 0.0), axis=1, keepdims=True)
        b_row = jnp.sum(jnp.where(t_idx <= s_idx, lf_col, 0.0), axis=0, keepdims=True)
        b_tot = b_col[L - 1:L, :]
        dmat = jnp.where(causal, b_col - b_row + li_row, -jnp.inf)
        m_inter = b_col + m
        m_t = jnp.maximum(m_inter, jnp.max(dmat, axis=1, keepdims=True))
        p = jnp.exp(dmat - m_t)
        w_inter = jnp.exp(m_inter - m_t)
        m_new = m_t[L - 1:L, :]
        decay = jnp.exp(b_tot - b_col + li_col - m_new)
        carry = jnp.exp(b_tot + m - m_new)
        gate_vals.append((p, w_inter, m_t, m_new, decay, carry))

    scores = [qk[i] * (gate_vals[i][0] * K_SCALE) for i in range(len(pairs))]
    kd = [k_of(s, hd).astype(F32) * (gate_vals[i][4] * K_SCALE) for i, (s, hd) in enumerate(pairs)]
    sv = [_dot(scores[i].astype(BF16), v_of(s, hd)) for i, (s, hd) in enumerate(pairs)]
    kv = [lax.dot_general(kd[i].astype(BF16), v_of(s, hd), (((0,), (0,)), ((), ())),
                          preferred_element_type=F32) for i, (s, hd) in enumerate(pairs)]

    for i, (s, hd) in enumerate(pairs):
        p, w_inter, m_t, m_new, decay, carry = gate_vals[i]
        n_row = n_in[s, hd:hd + 1, :]
        c_new = carry * c_in[s, hd] + kv[i]
        n_new = carry * n_row + jnp.sum(kd[i], axis=0, keepdims=True)
        num = sv[i] + qc[i] * w_inter
        qn = jnp.sum(q_of(s, hd).astype(F32) * n_row, axis=1, keepdims=True)
        den = jnp.sum(scores[i], axis=1, keepdims=True) + w_inter * qn
        c_ref[s, hd] = c_new
        n_ref[s, hd:hd + 1, :] = n_new
        m_ref[s, hd:hd + 1, :] = jnp.broadcast_to(m_new, (1, GATE_LANES))
        out_ref[rows_of(s), hd * DV:(hd + 1) * DV] = num / jnp.maximum(jnp.abs(den), jnp.exp(-m_t))


def _mlstm(proj, gcol, grow, c0, n0, m0, *, row0, n_blocks, seqs, length, n_chunks):
    assert seqs == 1 or n_chunks == 1
    blk = seqs * length
    assert row0 % blk == 0
    b0 = row0 // blk
    n_seq = n_blocks * seqs

    def rows_map(col):
        return lambda b, c: (b0 + b * n_chunks + c, col)

    kernel = functools.partial(_mlstm_kernel, seqs=seqs, length=length, single_chunk=n_chunks == 1)
    return pl.pallas_call(
        kernel,
        out_shape=(jax.ShapeDtypeStruct((n_blocks * n_chunks * blk, HEADS * DV), F32),
                   jax.ShapeDtypeStruct((n_seq, HEADS, DQK, DV), F32),
                   jax.ShapeDtypeStruct((n_seq, HEADS, DQK), F32),
                   jax.ShapeDtypeStruct((n_seq, HEADS, GATE_LANES), F32)),
        grid=(n_blocks, n_chunks),
        in_specs=[pl.BlockSpec((blk, 2 * HEADS * DQK), rows_map(0)),
                  pl.BlockSpec((blk, HEADS * DV), rows_map(1)),
                  pl.BlockSpec((blk, GATE_LANES), rows_map(0)),
                  pl.BlockSpec((1, seqs, 2 * HEADS, length), lambda b, c: (b * n_chunks + c, 0, 0, 0)),
                  pl.BlockSpec((seqs, HEADS, DQK, DV), lambda b, c: (b, 0, 0, 0)),
                  pl.BlockSpec((seqs, HEADS, DQK), lambda b, c: (b, 0, 0)),
                  pl.BlockSpec((seqs, HEADS, GATE_LANES), lambda b, c: (b, 0, 0))],
        out_specs=(pl.BlockSpec((blk, HEADS * DV), lambda b, c: (b * n_chunks + c, 0)),
                   pl.BlockSpec((seqs, HEADS, DQK, DV), lambda b, c: (b, 0, 0, 0)),
                   pl.BlockSpec((seqs, HEADS, DQK), lambda b, c: (b, 0, 0)),
                   pl.BlockSpec((seqs, HEADS, GATE_LANES), lambda b, c: (b, 0, 0))),
        compiler_params=_params(("parallel", "arbitrary")),
        name=f"mlstm_l{length}",
    )(proj, proj, gcol, grow, c0, n0, m0)


def _pool_mix(ext, first, n_rows, pos, wg_ref, scale_ref):
    outs = []
    for g, w in enumerate(POOL_WINDOWS):
        cols = slice(g * POOL_GROUP_DIM, (g + 1) * POOL_GROUP_DIM)
        e = ext[:, cols]
        total, span = e, 1
        while span < w:
            total = total + pltpu.roll(total, span, 0)
            span *= 2
        cnt = jnp.minimum(float(w), pos + 1.0)
        pooled = total[first:first + n_rows] / cnt - e[first:first + n_rows]
        outs.append(_dot(pooled.astype(BF16), wg_ref[g].astype(BF16)))
    mixed = jnp.concatenate(outs, axis=-1) * scale_ref[...]
    return mixed.astype(BF16)


def _pool_rows_kernel(u_ref, prev_ref, first_ref, wg_ref, scale_ref, o_ref, *, tiles_per_seq, pos0):
    tr = u_ref.shape[0]
    tile = pl.program_id(0) % tiles_per_seq
    halo = jnp.where(tile == 0, first_ref[...], prev_ref[...])
    ext = jnp.concatenate([halo, u_ref[...]], axis=0)
    pos = (lax.broadcasted_iota(jnp.int32, (tr, 1), 0) + (tile * tr + pos0)).astype(F32)
    o_ref[...] = _pool_mix(ext, N_META, tr, pos, wg_ref, scale_ref)


def _pool_rows(u, first_src, w_group, scale, *, row0, n_tiles, tr, tiles_per_seq, pos0, first_block):
    hb = tr // N_META
    t0 = row0 // tr
    kernel = functools.partial(_pool_rows_kernel, tiles_per_seq=tiles_per_seq, pos0=pos0)
    return pl.pallas_call(
        kernel,
        out_shape=jax.ShapeDtypeStruct((n_tiles * tr, D), BF16),
        grid=(n_tiles,),
        in_specs=[pl.BlockSpec((tr, D), lambda i: (t0 + i, 0)),
                  pl.BlockSpec((N_META, D), lambda i: (jnp.maximum((t0 + i) * hb - 1, 0), 0)),
                  pl.BlockSpec((N_META, D), lambda i: (first_block, 0)),
                  pl.BlockSpec((None, len(POOL_WINDOWS), POOL_GROUP_DIM, POOL_GROUP_DIM),
                               lambda i: (0, 0, 0, 0)),
                  pl.BlockSpec((1, D), lambda i: (0, 0))],
        out_specs=pl.BlockSpec((tr, D), lambda i: (i, 0)),
        compiler_params=_params(("parallel",)),
        name=f"pool_rows_{tr}",
    )(u, u, first_src, w_group, scale)


def _pool_seqs_kernel(prefix_ref, u_ref, wg_ref, scale_ref, o_ref, buf_ref, *, pos0):
    g = prefix_ref.shape[0]
    n_new = u_ref.shape[0] // g
    r = 1 + POOL_BUF + n_new
    prefix = prefix_ref[...]
    u3 = u_ref[...].reshape(g, n_new, D)
    ext = jnp.concatenate([jnp.zeros((g, 1, D), F32), prefix, u3], axis=1).reshape(g * r, D)
    buf_ref[...] = jnp.concatenate([prefix, u3], axis=1)[:, n_new:, :]
    outs = []
    pos = (lax.broadcasted_iota(jnp.int32, (g, n_new, 1), 1) + pos0).astype(F32)
    for gi, w in enumerate(POOL_WINDOWS):
        cols = slice(gi * POOL_GROUP_DIM, (gi + 1) * POOL_GROUP_DIM)
        e = ext[:, cols]
        total, span = e, 1
        while span < w:
            total = total + pltpu.roll(total, span, 0)
            span *= 2
        cnt = jnp.minimum(float(w), pos + 1.0)
        tot3 = total.reshape(g, r, POOL_GROUP_DIM)[:, 1 + POOL_BUF:, :]
        e3 = e.reshape(g, r, POOL_GROUP_DIM)[:, 1 + POOL_BUF:, :]
        pooled = (tot3 / cnt - e3).reshape(g * n_new, POOL_GROUP_DIM)
        outs.append(_dot(pooled.astype(BF16), wg_ref[gi].astype(BF16)))
    mixed = jnp.concatenate(outs, axis=-1) * scale_ref[...]
    o_ref[...] = mixed.astype(BF16)


def _pool_seqs(prefix, u, w_group, scale, *, n_new, seqs_per_block, pos0):
    n_seq = prefix.shape[1]
    kernel = functools.partial(_pool_seqs_kernel, pos0=pos0)
    return pl.pallas_call(
        kernel,
        out_shape=(jax.ShapeDtypeStruct((n_seq * n_new, D), BF16),
                   jax.ShapeDtypeStruct((n_seq, POOL_BUF, D), F32)),
        grid=(n_seq // seqs_per_block,),
        in_specs=[pl.BlockSpec((None, seqs_per_block, POOL_BUF, D), lambda i: (0, i, 0, 0)),
                  pl.BlockSpec((seqs_per_block * n_new, D), lambda i: (i, 0)),
                  pl.BlockSpec((None, len(POOL_WINDOWS), POOL_GROUP_DIM, POOL_GROUP_DIM),
                               lambda i: (0, 0, 0, 0)),
                  pl.BlockSpec((1, D), lambda i: (0, 0))],
        out_specs=(pl.BlockSpec((seqs_per_block * n_new, D), lambda i: (i, 0)),
                   pl.BlockSpec((seqs_per_block, POOL_BUF, D), lambda i: (i, 0, 0))),
        compiler_params=_params(("parallel",)),
        name="pool_seqs",
    )(prefix, u, w_group, scale)


def kernel(x_prompt, x_sample, state_mlstm_c, state_mlstm_n, state_mlstm_m, state_pool, meta_tokens,
           norm_mix_pre, norm_mix_post, norm_ffn_pre, norm_ffn_post, mlstm_w_in, mlstm_b_i, mlstm_b_f,
           mlstm_head_norm, mlstm_w_out, pool_w_in, pool_w_group, pool_scale, pool_w_out,
           ffn_w_up, ffn_w_down):
    B, S, _ = x_prompt.shape
    DB, DS, _ = x_sample.shape
    n_p, n_s = B * S, DB * DS
    rows_a = 2 * TM_AUX
    row_m = n_s
    assert n_s + N_META <= rows_a and n_p % TM_PROJ == 0 and n_p % TM == 0
    assert S % CHUNK == 0 and S % POOL_TR == 0 and row_m % N_META == 0

    h_p = x_prompt.reshape(n_p, D)
    h_a = jnp.concatenate([x_sample.reshape(n_s, D), meta_tokens.astype(F32),
                           jnp.zeros((rows_a - n_s - N_META, D), F32)], axis=0)
    pad_a = jnp.zeros((rows_a - n_s - N_META, D), BF16)

    def gain(a):
        return a.reshape(1, D).astype(F32)


    def matmul_norm_res(a_p, a_a, w, g):
        return _matmul_norm_res(a_p, w, g, h_p, tm=TM), _matmul_norm_res(a_a, w, g, h_a, tm=TM_AUX)

    def ffn(layer):
        g_pre, g_post = gain(norm_ffn_pre[layer]), gain(norm_ffn_post[layer])
        out_a, w_up, w_down = _ffn_cast(h_a, g_pre, ffn_w_up.astype(F32), ffn_w_down.astype(F32), g_post,
                                        layer=layer, tf=TF_AUX)
        return _ffn(h_p, g_pre, w_up, w_down, g_post, tm=TM_FFN, tf=TF), out_a

    w_in_t = jnp.swapaxes(mlstm_w_in[:1].astype(F32), 1, 2)
    w_gate_t = jnp.pad(w_in_t[0, PROJ_MAIN:], ((0, GATE_LANES - 2 * HEADS), (0, 0))).astype(BF16)
    b_gate = jnp.pad(jnp.concatenate([mlstm_b_i[0], mlstm_b_f[0]]).astype(F32),
                     (0, GATE_LANES - 2 * HEADS)).reshape(1, GATE_LANES)
    g_mix = gain(norm_mix_pre[0])
    proj_a, gcol_a, w_in_tb = _norm_matmul_gates_cast(h_a, g_mix, w_in_t, w_gate_t, b_gate, n=PROJ_MAIN,
                                                      tn=TN_PROJ)
    proj_p, gcol_p = _norm_matmul_gates(h_p, g_mix, w_in_tb, w_gate_t, b_gate, tm=TM_PROJ, tn=2 * TN_PROJ)

    def gate_rows(gcol, r0, n_blk, seqs, length):
        g8 = gcol[r0:r0 + n_blk * seqs * length, :2 * HEADS]
        return g8.reshape(n_blk, seqs, length, 2 * HEADS).transpose(0, 1, 3, 2)

    head_g = mlstm_head_norm[0].reshape(1, HEADS * DV).astype(F32)
    zc = jnp.zeros((1, HEADS, DQK, DV), F32)
    zn = jnp.zeros((1, HEADS, DQK), F32)
    zm = jnp.zeros((1, HEADS, GATE_LANES), F32)
    mix_m, c_m, n_m, m_m = _mlstm(proj_a, gcol_a, gate_rows(gcol_a, row_m, 1, 1, N_META), zc, zn, zm,
                                  row0=row_m, n_blocks=1, seqs=1, length=N_META, n_chunks=1)
    mix_p, c_p, n_p_, m_p = _mlstm(proj_p, gcol_p, gate_rows(gcol_p, 0, B * (S // CHUNK), 1, CHUNK),
                                   jnp.broadcast_to(c_m, (B,) + c_m.shape[1:]),
                                   jnp.broadcast_to(n_m, (B,) + n_m.shape[1:]),
                                   jnp.broadcast_to(m_m, (B,) + m_m.shape[1:]),
                                   row0=0, n_blocks=B, seqs=1, length=CHUNK, n_chunks=S // CHUNK)
    m0_s = jnp.broadcast_to(state_mlstm_m[0].astype(F32)[:, :, None], (DB, HEADS, GATE_LANES))
    mix_s, c_s, n_s_, m_s = _mlstm(proj_a, gcol_a, gate_rows(gcol_a, 0, DB // SAMPLE_SEQS, SAMPLE_SEQS, DS),
                                   state_mlstm_c[0].astype(F32), state_mlstm_n[0].astype(F32), m0_s,
                                   row0=0, n_blocks=DB // SAMPLE_SEQS, seqs=SAMPLE_SEQS, length=DS, n_chunks=1)
    mix_a = jnp.concatenate([mix_s, mix_m, pad_a.astype(F32)], axis=0)
    w_out, g_post = mlstm_w_out[:1].astype(F32), gain(norm_mix_post[0])
    h_p, h_a = (_gated_matmul_norm_res(mix_p, proj_p, head_g, w_out, g_post, h_p, tm=TM),
                _gated_matmul_norm_res(mix_a, proj_a, head_g, w_out, g_post, h_a, tm=TM_AUX // 2))
    h_p, h_a = ffn(0)

    w_pool_in, g_mix = pool_w_in[:1].astype(F32), gain(norm_mix_pre[1])
    u_p = _norm_matmul(h_p, g_mix, w_pool_in, tm=TM, out_dtype=F32)
    u_a = _norm_matmul(h_a, g_mix, w_pool_in, tm=TM_AUX, out_dtype=F32)
    w_group = pool_w_group[:1].astype(F32)
    p_scale = pool_scale[0].reshape(1, D).astype(F32)
    pmix_m = _pool_rows(u_a, jnp.zeros((N_META, D), F32), w_group, p_scale, row0=row_m, n_tiles=1,
                        tr=N_META, tiles_per_seq=1, pos0=0, first_block=0)
    pmix_p = _pool_rows(u_p, u_a, w_group, p_scale, row0=0, n_tiles=n_p // POOL_TR, tr=POOL_TR,
                        tiles_per_seq=S // POOL_TR, pos0=N_META, first_block=row_m // N_META)
    pmix_s, pool_s = _pool_seqs(state_pool[:1].astype(F32), u_a, w_group, p_scale, n_new=DS,
                                seqs_per_block=POOL_SEQS, pos0=PAST_LEN)
    pmix_a = jnp.concatenate([pmix_s, pmix_m, pad_a], axis=0)
    h_p, h_a = matmul_norm_res(pmix_p, pmix_a, pool_w_out[:1].astype(F32), gain(norm_mix_post[1]))
    h_p, h_a = ffn(1)

    y_prompt = h_p.reshape(B, S, D)
    y_sample = h_a[:n_s].reshape(DB, DS, D)
    dt_c, dt_n, dt_m, dt_pool = state_mlstm_c.dtype, state_mlstm_n.dtype, state_mlstm_m.dtype, state_pool.dtype
    u_prompt = u_p.reshape(B, S, D)
    return (y_prompt, y_sample,
            c_p[None].astype(dt_c), n_p_[None].astype(dt_n), m_p[None, :, :, 0].astype(dt_m),
            u_prompt[None, :, S - POOL_BUF:].astype(dt_pool),
            c_s[None].astype(dt_c), n_s_[None].astype(dt_n), m_s[None, :, :, 0].astype(dt_m),
            pool_s[None].astype(dt_pool))
```

```python
import functools

import jax
import jax.numpy as jnp
from jax import lax
from jax.experimental import pallas as pl
from jax.experimental.pallas import tpu as pltpu

D = 2048
N_META = 16
HEADS = 4
DQK = 256
DV = 512
D_FF = 4 * D
POOL_WINDOWS = (2, 4, 8, 16)
POOL_GROUP_DIM = D // len(POOL_WINDOWS)
POOL_BUF = 15
PAST_LEN = 16384
EPS = 1e-6
K_SCALE = DQK ** -0.5
PROJ_MAIN = 2 * HEADS * DQK + 2 * HEADS * DV
GATE_LANES = 128

VMEM_LIMIT = 56 * 1024 * 1024

F32 = jnp.float32
BF16 = jnp.bfloat16

TM_PROJ = 1024
TN_PROJ = 1024
TM = 512
TM_AUX = 544
TM_FFN = 512
TF = 1024
TF_AUX = 512
SUB_ROWS = 256
CAST_SUB_ROWS = TM_AUX // 2
CHUNK = 512
SAMPLE_SEQS = 4
POOL_TR = 256
POOL_SEQS = 16


def _params(semantics):
    return pltpu.CompilerParams(dimension_semantics=semantics, vmem_limit_bytes=VMEM_LIMIT)


def _rmsnorm(x, g):
    return x * lax.rsqrt(jnp.mean(x * x, axis=-1, keepdims=True) + EPS) * g


def _dot(a, b):
    return jnp.dot(a, b, preferred_element_type=F32)


def _norm_matmul_kernel(x_ref, g_ref, w_ref, o_ref, wb_ref):
    @pl.when(pl.program_id(0) == 0)
    def _():
        wb_ref[...] = w_ref[...].astype(BF16)

    for r in range(0, x_ref.shape[0], SUB_ROWS):
        rows = slice(r, min(r + SUB_ROWS, x_ref.shape[0]))
        xn = _rmsnorm(x_ref[rows, :], g_ref[...]).astype(BF16)
        o_ref[rows, :] = _dot(xn, wb_ref[...]).astype(o_ref.dtype)


def _dot_t(a, b):
    return lax.dot_general(a, b, (((1,), (1,)), ((), ())), preferred_element_type=F32)


def _in_proj_first_block(x_ref, g_ref, w_t, wgt_ref, bg_ref, o_ref, gate_ref, xn_ref):
    tm = x_ref.shape[0]
    for r in range(0, tm, SUB_ROWS):
        rows = slice(r, min(r + SUB_ROWS, tm))
        xn = _rmsnorm(x_ref[rows, :], g_ref[...]).astype(BF16)
        xn_ref[rows, :] = xn
        z = _dot_t(xn, wgt_ref[...]) + bg_ref[...]
        lane = lax.broadcasted_iota(jnp.int32, z.shape, 1)
        log_sig = jnp.minimum(z, 0.0) - jnp.log1p(jnp.exp(-jnp.abs(z)))
        gate_ref[rows, :] = jnp.where(lane >= HEADS, log_sig, z)
        o_ref[rows, :] = _dot_t(xn, w_t).astype(o_ref.dtype)


def _norm_matmul_gates_kernel(x_ref, g_ref, wt_ref, wgt_ref, bg_ref, o_ref, gate_ref, xn_ref):
    @pl.when(pl.program_id(1) == 0)
    def _():
        _in_proj_first_block(x_ref, g_ref, wt_ref[...], wgt_ref, bg_ref, o_ref, gate_ref, xn_ref)

    @pl.when(pl.program_id(1) > 0)
    def _():
        o_ref[...] = _dot_t(xn_ref[...], wt_ref[...]).astype(o_ref.dtype)


def _norm_matmul_gates_cast_kernel(x_ref, g_ref, wt_ref, wgt_ref, bg_ref, o_ref, gate_ref, wtb_ref, xn_ref):
    wtb_ref[...] = wt_ref[...].astype(BF16)

    @pl.when(pl.program_id(1) == 0)
    def _():
        _in_proj_first_block(x_ref, g_ref, wtb_ref[...], wgt_ref, bg_ref, o_ref, gate_ref, xn_ref)

    @pl.when(pl.program_id(1) > 0)
    def _():
        o_ref[...] = _dot_t(xn_ref[...], wtb_ref[...]).astype(o_ref.dtype)


def _norm_matmul(x, g, w, *, tm, out_dtype):
    rows = x.shape[0]
    return pl.pallas_call(
        _norm_matmul_kernel,
        out_shape=jax.ShapeDtypeStruct((rows, D), out_dtype),
        grid=(rows // tm,),
        in_specs=[pl.BlockSpec((tm, D), lambda i: (i, 0)),
                  pl.BlockSpec((1, D), lambda i: (0, 0)),
                  pl.BlockSpec((None, D, D), lambda i: (0, 0, 0), pipeline_mode=pl.Buffered(1))],
        out_specs=pl.BlockSpec((tm, D), lambda i: (i, 0)),
        scratch_shapes=[pltpu.VMEM((D, D), BF16)],
        compiler_params=_params(("arbitrary",)),
        name="norm_matmul",
    )(x, g, w)


def _norm_matmul_gates(x, g, w_t, wg_t, bg, *, tm, tn):
    rows, n = x.shape[0], w_t.shape[0]
    return pl.pallas_call(
        _norm_matmul_gates_kernel,
        out_shape=(jax.ShapeDtypeStruct((rows, n), BF16),
                   jax.ShapeDtypeStruct((rows, GATE_LANES), F32)),
        grid=(rows // tm, n // tn),
        in_specs=[pl.BlockSpec((tm, D), lambda i, j: (i, 0)),
                  pl.BlockSpec((1, D), lambda i, j: (0, 0)),
                  pl.BlockSpec((tn, D), lambda i, j: (j, 0)),
                  pl.BlockSpec((GATE_LANES, D), lambda i, j: (0, 0)),
                  pl.BlockSpec((1, GATE_LANES), lambda i, j: (0, 0))],
        out_specs=(pl.BlockSpec((tm, tn), lambda i, j: (i, j)),
                   pl.BlockSpec((tm, GATE_LANES), lambda i, j: (i, 0))),
        scratch_shapes=[pltpu.VMEM((tm, D), BF16)],
        compiler_params=_params(("parallel", "arbitrary")),
        name="norm_matmul_gates",
    )(x, g, w_t, wg_t, bg)


def _norm_matmul_gates_cast(x, g, w_t, wg_t, bg, *, n, tn):
    rows = x.shape[0]
    return pl.pallas_call(
        _norm_matmul_gates_cast_kernel,
        out_shape=(jax.ShapeDtypeStruct((rows, n), BF16),
                   jax.ShapeDtypeStruct((rows, GATE_LANES), F32),
                   jax.ShapeDtypeStruct((n, D), BF16)),
        grid=(1, n // tn),
        in_specs=[pl.BlockSpec((rows, D), lambda i, j: (0, 0), pipeline_mode=pl.Buffered(1)),
                  pl.BlockSpec((1, D), lambda i, j: (0, 0)),
                  pl.BlockSpec((None, tn, D), lambda i, j: (0, j, 0)),
                  pl.BlockSpec((GATE_LANES, D), lambda i, j: (0, 0)),
                  pl.BlockSpec((1, GATE_LANES), lambda i, j: (0, 0))],
        out_specs=(pl.BlockSpec((rows, tn), lambda i, j: (0, j)),
                   pl.BlockSpec((rows, GATE_LANES), lambda i, j: (0, 0)),
                   pl.BlockSpec((tn, D), lambda i, j: (j, 0))),
        scratch_shapes=[pltpu.VMEM((rows, D), BF16)],
        compiler_params=_params(("arbitrary", "arbitrary")),
        name="norm_matmul_gates_cast",
    )(x, g, w_t, wg_t, bg)


def _matmul_norm_res_kernel(a_ref, w_ref, g_ref, h_ref, o_ref, wb_ref):
    @pl.when(pl.program_id(0) == 0)
    def _():
        wb_ref[...] = w_ref[...].astype(BF16)

    y = _dot(a_ref[...], wb_ref[...])
    o_ref[...] = h_ref[...] + _rmsnorm(y, g_ref[...])


def _gated_matmul_norm_res_kernel(hm_ref, og_ref, hg_ref, w_ref, g_ref, h_ref, o_ref, wb_ref):
    @pl.when(pl.program_id(0) == 0)
    def _():
        wb_ref[...] = w_ref[...].astype(BF16)

    tm = hm_ref.shape[0]
    for r in range(0, tm, SUB_ROWS):
        rows = slice(r, min(r + SUB_ROWS, tm))
        parts = []
        for hd in range(HEADS):
            cols = slice(hd * DV, (hd + 1) * DV)
            hn = _rmsnorm(hm_ref[rows, cols], hg_ref[:, cols])
            gate = 0.5 * jnp.tanh(0.5 * og_ref[rows, cols].astype(F32)) + 0.5
            parts.append((hn * gate).astype(BF16))
        y = _dot(jnp.concatenate(parts, axis=-1), wb_ref[...])
        o_ref[rows, :] = h_ref[rows, :] + _rmsnorm(y, g_ref[...])


def _gated_matmul_norm_res(hm, proj, head_g, w, g, h, *, tm):
    rows = hm.shape[0]
    return pl.pallas_call(
        _gated_matmul_norm_res_kernel,
        out_shape=jax.ShapeDtypeStruct((rows, D), F32),
        grid=(rows // tm,),
        in_specs=[pl.BlockSpec((tm, HEADS * DV), lambda i: (i, 0)),
                  pl.BlockSpec((tm, HEADS * DV), lambda i: (i, 2)),
                  pl.BlockSpec((1, HEADS * DV), lambda i: (0, 0)),
                  pl.BlockSpec((None, D, D), lambda i: (0, 0, 0), pipeline_mode=pl.Buffered(1)),
                  pl.BlockSpec((1, D), lambda i: (0, 0)),
                  pl.BlockSpec((tm, D), lambda i: (i, 0))],
        out_specs=pl.BlockSpec((tm, D), lambda i: (i, 0)),
        scratch_shapes=[pltpu.VMEM((D, D), BF16)],
        compiler_params=_params(("arbitrary",)),
        name="gated_matmul_norm_res",
    )(hm, proj, head_g, w, g, h)


def _matmul_norm_res(a, w, g, h, *, tm):
    rows = a.shape[0]
    return pl.pallas_call(
        _matmul_norm_res_kernel,
        out_shape=jax.ShapeDtypeStruct((rows, D), F32),
        grid=(rows // tm,),
        in_specs=[pl.BlockSpec((tm, D), lambda i: (i, 0)),
                  pl.BlockSpec((None, D, D), lambda i: (0, 0, 0), pipeline_mode=pl.Buffered(1)),
                  pl.BlockSpec((1, D), lambda i: (0, 0)),
                  pl.BlockSpec((tm, D), lambda i: (i, 0))],
        out_specs=pl.BlockSpec((tm, D), lambda i: (i, 0)),
        scratch_shapes=[pltpu.VMEM((D, D), BF16)],
        compiler_params=_params(("arbitrary",)),
        name="matmul_norm_res",
    )(a, w, g, h)


def _ffn_chunk(f, last, h_ref, gpre_ref, wup_ref, wdown_ref, gpost_ref, o_ref, xn_ref, *, sub, sub_middle):
    tm = h_ref.shape[0]
    subs = [slice(r, min(r + sub, tm)) for r in range(0, tm, sub)]

    def mlp_chunk(xn):
        a = jnp.maximum(_dot(xn, wup_ref[...]), 0.0)
        return _dot((a * a).astype(BF16), wdown_ref[...])

    @pl.when(f == 0)
    def _():
        for rows in subs:
            xn = _rmsnorm(h_ref[rows, :], gpre_ref[...]).astype(BF16)
            xn_ref[rows, :] = xn
            o_ref[rows, :] = mlp_chunk(xn)

    @pl.when(jnp.logical_and(f > 0, f < last))
    def _():
        for rows in (subs if sub_middle else [slice(None)]):
            o_ref[rows, :] += mlp_chunk(xn_ref[rows, :])

    @pl.when(f == last)
    def _():
        for rows in subs:
            y = o_ref[rows, :] + mlp_chunk(xn_ref[rows, :])
            o_ref[rows, :] = h_ref[rows, :] + _rmsnorm(y, gpost_ref[...])


def _ffn_kernel(h_ref, gpre_ref, wup_ref, wdown_ref, gpost_ref, o_ref, xn_ref):
    _ffn_chunk(pl.program_id(1), pl.num_programs(1) - 1, h_ref, gpre_ref, wup_ref, wdown_ref, gpost_ref,
               o_ref, xn_ref, sub=SUB_ROWS, sub_middle=False)


def _ffn_cast_kernel(h_ref, gpre_ref, wup_ref, wdown_ref, gpost_ref, o_ref, wupb_ref, wdownb_ref, xn_ref):
    wupb_ref[...] = wup_ref[...].astype(BF16)
    wdownb_ref[...] = wdown_ref[...].astype(BF16)
    _ffn_chunk(pl.program_id(0), pl.num_programs(0) - 1, h_ref, gpre_ref, wupb_ref, wdownb_ref, gpost_ref,
               o_ref, xn_ref, sub=CAST_SUB_ROWS, sub_middle=True)


def _ffn(h, g_pre, w_up, w_down, g_post, *, tm, tf):
    n_tiles = h.shape[0] // tm
    return pl.pallas_call(
        _ffn_kernel,
        out_shape=jax.ShapeDtypeStruct((n_tiles * tm, D), F32),
        grid=(n_tiles, D_FF // tf),
        in_specs=[pl.BlockSpec((tm, D), lambda i, f: (i, 0)),
                  pl.BlockSpec((1, D), lambda i, f: (0, 0)),
                  pl.BlockSpec((D, tf), lambda i, f: (0, f)),
                  pl.BlockSpec((tf, D), lambda i, f: (f, 0)),
                  pl.BlockSpec((1, D), lambda i, f: (0, 0))],
        out_specs=pl.BlockSpec((tm, D), lambda i, f: (i, 0)),
        scratch_shapes=[pltpu.VMEM((tm, D), BF16)],
        compiler_params=_params(("parallel", "arbitrary")),
        name="ffn",
    )(h, g_pre, w_up, w_down, g_post)


def _ffn_cast(h, g_pre, w_up, w_down, g_post, *, layer, tf):
    rows = h.shape[0]
    single = pl.Buffered(1)
    return pl.pallas_call(
        _ffn_cast_kernel,
        out_shape=(jax.ShapeDtypeStruct((rows, D), F32),
                   jax.ShapeDtypeStruct((D, D_FF), BF16),
                   jax.ShapeDtypeStruct((D_FF, D), BF16)),
        grid=(D_FF // tf,),
        in_specs=[pl.BlockSpec((rows, D), lambda f: (0, 0), pipeline_mode=single),
                  pl.BlockSpec((1, D), lambda f: (0, 0)),
                  pl.BlockSpec((None, D, tf), lambda f: (layer, 0, f)),
                  pl.BlockSpec((None, tf, D), lambda f: (layer, f, 0)),
                  pl.BlockSpec((1, D), lambda f: (0, 0))],
        out_specs=(pl.BlockSpec((rows, D), lambda f: (0, 0), pipeline_mode=single),
                   pl.BlockSpec((D, tf), lambda f: (0, f)),
                   pl.BlockSpec((tf, D), lambda f: (f, 0))),
        scratch_shapes=[pltpu.VMEM((rows, D), BF16)],
        compiler_params=_params(("arbitrary",)),
        name="ffn_cast",
    )(h, g_pre, w_up, w_down, g_post)


def _mlstm_kernel(qk_ref, v_ref, gcol_ref, grow_ref, c0_ref, n0_ref, m0_ref,
                  out_ref, c_ref, n_ref, m_ref, *, seqs, length, single_chunk):
    if single_chunk:
        c_in, n_in, m_in = c0_ref, n0_ref, m0_ref
    else:
        c_in, n_in, m_in = c_ref, n_ref, m_ref

        @pl.when(pl.program_id(1) == 0)
        def _():
            c_ref[...] = c0_ref[...]
            n_ref[...] = n0_ref[...]
            m_ref[...] = m0_ref[...]

    pairs = [(s, hd) for s in range(seqs) for hd in range(HEADS)]
    L = length
    t_idx = lax.broadcasted_iota(jnp.int32, (L, L), 0)
    s_idx = lax.broadcasted_iota(jnp.int32, (L, L), 1)
    causal = s_idx <= t_idx

    def rows_of(s):
        return slice(s * L, (s + 1) * L)

    def q_of(s, hd):
        return qk_ref[rows_of(s), hd * DQK:(hd + 1) * DQK]

    def k_of(s, hd):
        return qk_ref[rows_of(s), (HEADS + hd) * DQK:(HEADS + hd + 1) * DQK]

    def v_of(s, hd):
        return v_ref[rows_of(s), hd * DV:(hd + 1) * DV]

    qk = [_dot_t(q_of(s, hd), k_of(s, hd)) for s, hd in pairs]
    qc = [_dot(q_of(s, hd), c_in[s, hd].astype(BF16)) for s, hd in pairs]

    gate_vals = []
    for s, hd in pairs:
        gcol = gcol_ref[rows_of(s), :]
        grow = grow_ref[0, s]
        li_row, lf_row = grow[hd:hd + 1, :], grow[HEADS + hd:HEADS + hd + 1, :]
        li_col, lf_col = gcol[:, hd:hd + 1], gcol[:, HEADS + hd:HEADS + hd + 1]
        m = m_in[s, hd:hd + 1, 0:1]
        b_col = jnp.sum(jnp.where(causal, lf_row, 0.0), axis=1, keepdims=True)
        b_row = jnp.sum(jnp.where(t_idx <= s_idx, lf_col, 0.0), axis=0, keepdims=True)
        b_tot = b_col[L - 1:L, :]
        dmat = jnp.where(causal, b_col - b_row + li_row, -jnp.inf)
        m_inter = b_col + m
        m_t = jnp.maximum(m_inter, jnp.max(dmat, axis=1, keepdims=True))
        p = jnp.exp(dmat - m_t)
        w_inter = jnp.exp(m_inter - m_t)
        m_new = m_t[L - 1:L, :]
        decay = jnp.exp(b_tot - b_col + li_col - m_new)
        carry = jnp.exp(b_tot + m - m_new)
        gate_vals.append((p, w_inter, m_t, m_new, decay, carry))

    scores = [qk[i] * (gate_vals[i][0] * K_SCALE) for i in range(len(pairs))]
    kd = [k_of(s, hd).astype(F32) * (gate_vals[i][4] * K_SCALE) for i, (s, hd) in enumerate(pairs)]
    sv = [_dot(scores[i].astype(BF16), v_of(s, hd)) for i, (s, hd) in enumerate(pairs)]
    kv = [lax.dot_general(kd[i].astype(BF16), v_of(s, hd), (((0,), (0,)), ((), ())),
                          preferred_element_type=F32) for i, (s, hd) in enumerate(pairs)]

    for i, (s, hd) in enumerate(pairs):
        p, w_inter, m_t, m_new, decay, carry = gate_vals[i]
        n_row = n_in[s, hd:hd + 1, :]
        c_new = carry * c_in[s, hd] + kv[i]
        n_new = carry * n_row + jnp.sum(kd[i], axis=0, keepdims=True)
        num = sv[i] + qc[i] * w_inter
        qn = jnp.sum(q_of(s, hd).astype(F32) * n_row, axis=1, keepdims=True)
        den = jnp.sum(scores[i], axis=1, keepdims=True) + w_inter * qn
        c_ref[s, hd] = c_new
        n_ref[s, hd:hd + 1, :] = n_new
        m_ref[s, hd:hd + 1, :] = jnp.broadcast_to(m_new, (1, GATE_LANES))
        out_ref[rows_of(s), hd * DV:(hd + 1) * DV] = num / jnp.maximum(jnp.abs(den), jnp.exp(-m_t))


def _mlstm(proj, gcol, grow, c0, n0, m0, *, row0, n_blocks, seqs, length, n_chunks):
    assert seqs == 1 or n_chunks == 1
    blk = seqs * length
    assert row0 % blk == 0
    b0 = row0 // blk
    n_seq = n_blocks * seqs

    def rows_map(col):
        return lambda b, c: (b0 + b * n_chunks + c, col)

    kernel = functools.partial(_mlstm_kernel, seqs=seqs, length=length, single_chunk=n_chunks == 1)
    return pl.pallas_call(
        kernel,
        out_shape=(jax.ShapeDtypeStruct((n_blocks * n_chunks * blk, HEADS * DV), F32),
                   jax.ShapeDtypeStruct((n_seq, HEADS, DQK, DV), F32),
                   jax.ShapeDtypeStruct((n_seq, HEADS, DQK), F32),
                   jax.ShapeDtypeStruct((n_seq, HEADS, GATE_LANES), F32)),
        grid=(n_blocks, n_chunks),
        in_specs=[pl.BlockSpec((blk, 2 * HEADS * DQK), rows_map(0)),
                  pl.BlockSpec((blk, HEADS * DV), rows_map(1)),
                  pl.BlockSpec((blk, GATE_LANES), rows_map(0)),
                  pl.BlockSpec((1, seqs, 2 * HEADS, length), lambda b, c: (b * n_chunks + c, 0, 0, 0)),
                  pl.BlockSpec((seqs, HEADS, DQK, DV), lambda b, c: (b, 0, 0, 0)),
                  pl.BlockSpec((seqs, HEADS, DQK), lambda b, c: (b, 0, 0)),
                  pl.BlockSpec((seqs, HEADS, GATE_LANES), lambda b, c: (b, 0, 0))],
        out_specs=(pl.BlockSpec((blk, HEADS * DV), lambda b, c: (b * n_chunks + c, 0)),
                   pl.BlockSpec((seqs, HEADS, DQK, DV), lambda b, c: (b, 0, 0, 0)),
                   pl.BlockSpec((seqs, HEADS, DQK), lambda b, c: (b, 0, 0)),
                   pl.BlockSpec((seqs, HEADS, GATE_LANES), lambda b, c: (b, 0, 0))),
        compiler_params=_params(("parallel", "arbitrary")),
        name=f"mlstm_l{length}",
    )(proj, proj, gcol, grow, c0, n0, m0)


def _pool_mix(ext, first, n_rows, pos, wg_ref, scale_ref):
    outs = []
    for g, w in enumerate(POOL_WINDOWS):
        cols = slice(g * POOL_GROUP_DIM, (g + 1) * POOL_GROUP_DIM)
        e = ext[:, cols]
        total, span = e, 1
        while span < w:
            total = total + pltpu.roll(total, span, 0)
            span *= 2
        cnt = jnp.minimum(float(w), pos + 1.0)
        pooled = total[first:first + n_rows] / cnt - e[first:first + n_rows]
        outs.append(_dot(pooled.astype(BF16), wg_ref[g].astype(BF16)))
    mixed = jnp.concatenate(outs, axis=-1) * scale_ref[...]
    return mixed.astype(BF16)


def _pool_rows_kernel(u_ref, prev_ref, first_ref, wg_ref, scale_ref, o_ref, *, tiles_per_seq, pos0):
    tr = u_ref.shape[0]
    tile = pl.program_id(0) % tiles_per_seq
    halo = jnp.where(tile == 0, first_ref[...], prev_ref[...])
    ext = jnp.concatenate([halo, u_ref[...]], axis=0)
    pos = (lax.broadcasted_iota(jnp.int32, (tr, 1), 0) + (tile * tr + pos0)).astype(F32)
    o_ref[...] = _pool_mix(ext, N_META, tr, pos, wg_ref, scale_ref)


def _pool_rows(u, first_src, w_group, scale, *, row0, n_tiles, tr, tiles_per_seq, pos0, first_block):
    hb = tr // N_META
    t0 = row0 // tr
    kernel = functools.partial(_pool_rows_kernel, tiles_per_seq=tiles_per_seq, pos0=pos0)
    return pl.pallas_call(
        kernel,
        out_shape=jax.ShapeDtypeStruct((n_tiles * tr, D), BF16),
        grid=(n_tiles,),
        in_specs=[pl.BlockSpec((tr, D), lambda i: (t0 + i, 0)),
                  pl.BlockSpec((N_META, D), lambda i: (jnp.maximum((t0 + i) * hb - 1, 0), 0)),
                  pl.BlockSpec((N_META, D), lambda i: (first_block, 0)),
                  pl.BlockSpec((None, len(POOL_WINDOWS), POOL_GROUP_DIM, POOL_GROUP_DIM),
                               lambda i: (0, 0, 0, 0)),
                  pl.BlockSpec((1, D), lambda i: (0, 0))],
        out_specs=pl.BlockSpec((tr, D), lambda i: (i, 0)),
        compiler_params=_params(("parallel",)),
        name=f"pool_rows_{tr}",
    )(u, u, first_src, w_group, scale)


def _pool_out_proj_kernel(u_ref, prev_ref, first_ref, wg_ref, scale_ref, w_ref, g_ref, h_ref, o_ref, wb_ref,
                          *, tiles_per_seq, pos0):
    @pl.when(pl.program_id(0) == 0)
    def _():
        wb_ref[...] = w_ref[...].astype(BF16)

    tr = u_ref.shape[0]
    tile = pl.program_id(0) % tiles_per_seq
    halo = jnp.where(tile == 0, first_ref[...], prev_ref[...])
    ext = jnp.concatenate([halo, u_ref[...]], axis=0)
    pos = (lax.broadcasted_iota(jnp.int32, (tr, 1), 0) + (tile * tr + pos0)).astype(F32)
    y = _dot(_pool_mix(ext, N_META, tr, pos, wg_ref, scale_ref), wb_ref[...])
    o_ref[...] = h_ref[...] + _rmsnorm(y, g_ref[...])


def _pool_out_proj(u, first_src, w_group, scale, w, g, h, *, tr, tiles_per_seq, pos0, first_block):
    hb = tr // N_META
    single = pl.Buffered(1)
    kernel = functools.partial(_pool_out_proj_kernel, tiles_per_seq=tiles_per_seq, pos0=pos0)
    return pl.pallas_call(
        kernel,
        out_shape=jax.ShapeDtypeStruct(h.shape, F32),
        grid=(u.shape[0] // tr,),
        in_specs=[pl.BlockSpec((tr, D), lambda i: (i, 0)),
                  pl.BlockSpec((N_META, D), lambda i: (jnp.maximum(i * hb - 1, 0), 0)),
                  pl.BlockSpec((N_META, D), lambda i: (first_block, 0)),
                  pl.BlockSpec((None, len(POOL_WINDOWS), POOL_GROUP_DIM, POOL_GROUP_DIM),
                               lambda i: (0, 0, 0, 0), pipeline_mode=single),
                  pl.BlockSpec((1, D), lambda i: (0, 0)),
                  pl.BlockSpec((None, D, D), lambda i: (0, 0, 0), pipeline_mode=single),
                  pl.BlockSpec((1, D), lambda i: (0, 0)),
                  pl.BlockSpec((tr, D), lambda i: (i, 0))],
        out_specs=pl.BlockSpec((tr, D), lambda i: (i, 0)),
        scratch_shapes=[pltpu.VMEM((D, D), BF16)],
        compiler_params=_params(("arbitrary",)),
        name="pool_out_proj",
    )(u, u, first_src, w_group, scale, w, g, h)


def _pool_seqs_kernel(prefix_ref, u_ref, wg_ref, scale_ref, o_ref, buf_ref, *, pos0):
    g = prefix_ref.shape[0]
    n_new = u_ref.shape[0] // g
    r = 1 + POOL_BUF + n_new
    prefix = prefix_ref[...]
    u3 = u_ref[...].reshape(g, n_new, D)
    ext = jnp.concatenate([jnp.zeros((g, 1, D), F32), prefix, u3], axis=1).reshape(g * r, D)
    buf_ref[...] = jnp.concatenate([prefix, u3], axis=1)[:, n_new:, :]
    outs = []
    pos = (lax.broadcasted_iota(jnp.int32, (g, n_new, 1), 1) + pos0).astype(F32)
    for gi, w in enumerate(POOL_WINDOWS):
        cols = slice(gi * POOL_GROUP_DIM, (gi + 1) * POOL_GROUP_DIM)
        e = ext[:, cols]
        total, span = e, 1
        while span < w:
            total = total + pltpu.roll(total, span, 0)
            span *= 2
        cnt = jnp.minimum(float(w), pos + 1.0)
        tot3 = total.reshape(g, r, POOL_GROUP_DIM)[:, 1 + POOL_BUF:, :]
        e3 = e.reshape(g, r, POOL_GROUP_DIM)[:, 1 + POOL_BUF:, :]
        pooled = (tot3 / cnt - e3).reshape(g * n_new, POOL_GROUP_DIM)
        outs.append(_dot(pooled.astype(BF16), wg_ref[gi].astype(BF16)))
    mixed = jnp.concatenate(outs, axis=-1) * scale_ref[...]
    o_ref[...] = mixed.astype(BF16)


def _pool_seqs(prefix, u, w_group, scale, *, n_new, seqs_per_block, pos0):
    n_seq = prefix.shape[1]
    kernel = functools.partial(_pool_seqs_kernel, pos0=pos0)
    return pl.pallas_call(
        kernel,
        out_shape=(jax.ShapeDtypeStruct((n_seq * n_new, D), BF16),
                   jax.ShapeDtypeStruct((n_seq, POOL_BUF, D), F32)),
        grid=(n_seq // seqs_per_block,),
        in_specs=[pl.BlockSpec((None, seqs_per_block, POOL_BUF, D), lambda i: (0, i, 0, 0)),
                  pl.BlockSpec((seqs_per_block * n_new, D), lambda i: (i, 0)),
                  pl.BlockSpec((None, len(POOL_WINDOWS), POOL_GROUP_DIM, POOL_GROUP_DIM),
                               lambda i: (0, 0, 0, 0)),
                  pl.BlockSpec((1, D), lambda i: (0, 0))],
        out_specs=(pl.BlockSpec((seqs_per_block * n_new, D), lambda i: (i, 0)),
                   pl.BlockSpec((seqs_per_block, POOL_BUF, D), lambda i: (i, 0, 0))),
        compiler_params=_params(("parallel",)),
        name="pool_seqs",
    )(prefix, u, w_group, scale)


def kernel(x_prompt, x_sample, state_mlstm_c, state_mlstm_n, state_mlstm_m, state_pool, meta_tokens,
           norm_mix_pre, norm_mix_post, norm_ffn_pre, norm_ffn_post, mlstm_w_in, mlstm_b_i, mlstm_b_f,
           mlstm_head_norm, mlstm_w_out, pool_w_in, pool_w_group, pool_scale, pool_w_out,
           ffn_w_up, ffn_w_down):
    B, S, _ = x_prompt.shape
    DB, DS, _ = x_sample.shape
    n_p, n_s = B * S, DB * DS
    rows_a = 2 * TM_AUX
    row_m = n_s
    assert n_s + N_META <= rows_a and n_p % TM_PROJ == 0 and n_p % TM == 0
    assert S % CHUNK == 0 and S % POOL_TR == 0 and row_m % N_META == 0

    h_p = x_prompt.reshape(n_p, D)
    h_a = jnp.concatenate([x_sample.reshape(n_s, D), meta_tokens.astype(F32),
                           jnp.zeros((rows_a - n_s - N_META, D), F32)], axis=0)
    pad_a = jnp.zeros((rows_a - n_s - N_META, D), BF16)

    def gain(a):
        return a.reshape(1, D).astype(F32)


    def ffn(layer):
        g_pre, g_post = gain(norm_ffn_pre[layer]), gain(norm_ffn_post[layer])
        out_a, w_up, w_down = _ffn_cast(h_a, g_pre, ffn_w_up.astype(F32), ffn_w_down.astype(F32), g_post,
                                        layer=layer, tf=TF_AUX)
        return _ffn(h_p, g_pre, w_up, w_down, g_post, tm=TM_FFN, tf=TF), out_a

    w_in_t = jnp.swapaxes(mlstm_w_in[:1].astype(F32), 1, 2)
    w_gate_t = jnp.pad(w_in_t[0, PROJ_MAIN:], ((0, GATE_LANES - 2 * HEADS), (0, 0))).astype(BF16)
    b_gate = jnp.pad(jnp.concatenate([mlstm_b_i[0], mlstm_b_f[0]]).astype(F32),
                     (0, GATE_LANES - 2 * HEADS)).reshape(1, GATE_LANES)
    g_mix = gain(norm_mix_pre[0])
    proj_a, gcol_a, w_in_tb = _norm_matmul_gates_cast(h_a, g_mix, w_in_t, w_gate_t, b_gate, n=PROJ_MAIN,
                                                      tn=TN_PROJ)
    proj_p, gcol_p = _norm_matmul_gates(h_p, g_mix, w_in_tb, w_gate_t, b_gate, tm=TM_PROJ, tn=2 * TN_PROJ)

    def gate_rows(gcol, r0, n_blk, seqs, length):
        g8 = gcol[r0:r0 + n_blk * seqs * length, :2 * HEADS]
        return g8.reshape(n_blk, seqs, length, 2 * HEADS).transpose(0, 1, 3, 2)

    head_g = mlstm_head_norm[0].reshape(1, HEADS * DV).astype(F32)
    zc = jnp.zeros((1, HEADS, DQK, DV), F32)
    zn = jnp.zeros((1, HEADS, DQK), F32)
    zm = jnp.zeros((1, HEADS, GATE_LANES), F32)
    mix_m, c_m, n_m, m_m = _mlstm(proj_a, gcol_a, gate_rows(gcol_a, row_m, 1, 1, N_META), zc, zn, zm,
                                  row0=row_m, n_blocks=1, seqs=1, length=N_META, n_chunks=1)
    mix_p, c_p, n_p_, m_p = _mlstm(proj_p, gcol_p, gate_rows(gcol_p, 0, B * (S // CHUNK), 1, CHUNK),
                                   jnp.broadcast_to(c_m, (B,) + c_m.shape[1:]),
                                   jnp.broadcast_to(n_m, (B,) + n_m.shape[1:]),
                                   jnp.broadcast_to(m_m, (B,) + m_m.shape[1:]),
                                   row0=0, n_blocks=B, seqs=1, length=CHUNK, n_chunks=S // CHUNK)
    m0_s = jnp.broadcast_to(state_mlstm_m[0].astype(F32)[:, :, None], (DB, HEADS, GATE_LANES))
    mix_s, c_s, n_s_, m_s = _mlstm(proj_a, gcol_a, gate_rows(gcol_a, 0, DB // SAMPLE_SEQS, SAMPLE_SEQS, DS),
                                   state_mlstm_c[0].astype(F32), state_mlstm_n[0].astype(F32), m0_s,
                                   row0=0, n_blocks=DB // SAMPLE_SEQS, seqs=SAMPLE_SEQS, length=DS, n_chunks=1)
    mix_a = jnp.concatenate([mix_s, mix_m, pad_a.astype(F32)], axis=0)
    w_out, g_post = mlstm_w_out[:1].astype(F32), gain(norm_mix_post[0])
    h_p, h_a = (_gated_matmul_norm_res(mix_p, proj_p, head_g, w_out, g_post, h_p, tm=TM),
                _gated_matmul_norm_res(mix_a, proj_a, head_g, w_out, g_post, h_a, tm=TM_AUX // 2))
    h_p, h_a = ffn(0)

    w_pool_in, g_mix = pool_w_in[:1].astype(F32), gain(norm_mix_pre[1])
    u_p = _norm_matmul(h_p, g_mix, w_pool_in, tm=TM, out_dtype=F32)
    u_a = _norm_matmul(h_a, g_mix, w_pool_in, tm=TM_AUX, out_dtype=F32)
    w_group = pool_w_group[:1].astype(F32)
    p_scale = pool_scale[0].reshape(1, D).astype(F32)
    pmix_m = _pool_rows(u_a, jnp.zeros((N_META, D), F32), w_group, p_scale, row0=row_m, n_tiles=1,
                        tr=N_META, tiles_per_seq=1, pos0=0, first_block=0)
    pmix_s, pool_s = _pool_seqs(state_pool[:1].astype(F32), u_a, w_group, p_scale, n_new=DS,
                                seqs_per_block=POOL_SEQS, pos0=PAST_LEN)
    pmix_a = jnp.concatenate([pmix_s, pmix_m, pad_a], axis=0)
    w_out, g_post = pool_w_out[:1].astype(F32), gain(norm_mix_post[1])
    h_p, h_a = (_pool_out_proj(u_p, u_a, w_group, p_scale, w_out, g_post, h_p, tr=POOL_TR,
                               tiles_per_seq=S // POOL_TR, pos0=N_META, first_block=row_m // N_META),
                _matmul_norm_res(pmix_a, w_out, g_post, h_a, tm=TM_AUX))
    h_p, h_a = ffn(1)

    y_prompt = h_p.reshape(B, S, D)
    y_sample = h_a[:n_s].reshape(DB, DS, D)
    dt_c, dt_n, dt_m, dt_pool = state_mlstm_c.dtype, state_mlstm_n.dtype, state_mlstm_m.dtype, state_pool.dtype
    u_prompt = u_p.reshape(B, S, D)
    return (y_prompt, y_sample,
            c_p[None].astype(dt_c), n_p_[None].astype(dt_n), m_p[None, :, :, 0].astype(dt_m),
            u_prompt[None, :, S - POOL_BUF:].astype(dt_pool),
            c_s[None].astype(dt_c), n_s_[None].astype(dt_n), m_s[None, :, :, 0].astype(dt_m),
            pool_s[None].astype(dt_pool))
```

```python
import functools

import jax
import jax.numpy as jnp
from jax import lax
from jax.experimental import pallas as pl
from jax.experimental.pallas import tpu as pltpu

D = 2048
N_META = 16
HEADS = 4
DQK = 256
DV = 512
D_FF = 4 * D
POOL_WINDOWS = (2, 4, 8, 16)
POOL_GROUP_DIM = D // len(POOL_WINDOWS)
POOL_BUF = 15
PAST_LEN = 16384
EPS = 1e-6
K_SCALE = DQK ** -0.5
PROJ_MAIN = 2 * HEADS * DQK + 2 * HEADS * DV
GATE_LANES = 128

VMEM_LIMIT = 56 * 1024 * 1024

F32 = jnp.float32
BF16 = jnp.bfloat16

TM_PROJ = 1024
TN_PROJ = 1024
TM = 512
TM_AUX = 544
TM_FFN = 512
TF = 1024
TF_AUX = 512
SUB_ROWS = 256
CAST_SUB_ROWS = TM_AUX // 2
CHUNK = 512
SAMPLE_SEQS = 4
POOL_TR = 256
POOL_SEQS = 16


def _params(semantics):
    return pltpu.CompilerParams(dimension_semantics=semantics, vmem_limit_bytes=VMEM_LIMIT)


def _rmsnorm(x, g):
    return x * lax.rsqrt(jnp.mean(x * x, axis=-1, keepdims=True) + EPS) * g


def _dot(a, b):
    return jnp.dot(a, b, preferred_element_type=F32)


def _norm_matmul_kernel(x_ref, g_ref, w_ref, o_ref, wb_ref):
    @pl.when(pl.program_id(0) == 0)
    def _():
        wb_ref[...] = w_ref[...].astype(BF16)

    for r in range(0, x_ref.shape[0], SUB_ROWS):
        rows = slice(r, min(r + SUB_ROWS, x_ref.shape[0]))
        xn = _rmsnorm(x_ref[rows, :], g_ref[...]).astype(BF16)
        o_ref[rows, :] = _dot(xn, wb_ref[...]).astype(o_ref.dtype)


def _dot_t(a, b):
    return lax.dot_general(a, b, (((1,), (1,)), ((), ())), preferred_element_type=F32)


def _in_proj_first_block(x_ref, g_ref, w_t, wgt_ref, bg_ref, o_ref, gate_ref, xn_ref):
    tm = x_ref.shape[0]
    for r in range(0, tm, SUB_ROWS):
        rows = slice(r, min(r + SUB_ROWS, tm))
        xn = _rmsnorm(x_ref[rows, :], g_ref[...]).astype(BF16)
        xn_ref[rows, :] = xn
        z = _dot_t(xn, wgt_ref[...]) + bg_ref[...]
        lane = lax.broadcasted_iota(jnp.int32, z.shape, 1)
        log_sig = jnp.minimum(z, 0.0) - jnp.log1p(jnp.exp(-jnp.abs(z)))
        gate_ref[rows, :] = jnp.where(lane >= HEADS, log_sig, z)
        o_ref[rows, :] = _dot_t(xn, w_t).astype(o_ref.dtype)


def _norm_matmul_gates_kernel(x_ref, g_ref, wt_ref, wgt_ref, bg_ref, o_ref, gate_ref, xn_ref):
    @pl.when(pl.program_id(1) == 0)
    def _():
        _in_proj_first_block(x_ref, g_ref, wt_ref[...], wgt_ref, bg_ref, o_ref, gate_ref, xn_ref)

    @pl.when(pl.program_id(1) > 0)
    def _():
        o_ref[...] = _dot_t(xn_ref[...], wt_ref[...]).astype(o_ref.dtype)


def _norm_matmul_gates_cast_kernel(x_ref, g_ref, wt_ref, wgt_ref, bg_ref, o_ref, gate_ref, wtb_ref, xn_ref):
    wtb_ref[...] = wt_ref[...].astype(BF16)

    @pl.when(pl.program_id(1) == 0)
    def _():
        _in_proj_first_block(x_ref, g_ref, wtb_ref[...], wgt_ref, bg_ref, o_ref, gate_ref, xn_ref)

    @pl.when(pl.program_id(1) > 0)
    def _():
        o_ref[...] = _dot_t(xn_ref[...], wtb_ref[...]).astype(o_ref.dtype)


def _norm_matmul(x, g, w, *, tm, out_dtype):
    rows = x.shape[0]
    return pl.pallas_call(
        _norm_matmul_kernel,
        out_shape=jax.ShapeDtypeStruct((rows, D), out_dtype),
        grid=(rows // tm,),
        in_specs=[pl.BlockSpec((tm, D), lambda i: (i, 0)),
                  pl.BlockSpec((1, D), lambda i: (0, 0)),
                  pl.BlockSpec((None, D, D), lambda i: (0, 0, 0), pipeline_mode=pl.Buffered(1))],
        out_specs=pl.BlockSpec((tm, D), lambda i: (i, 0)),
        scratch_shapes=[pltpu.VMEM((D, D), BF16)],
        compiler_params=_params(("arbitrary",)),
        name="norm_matmul",
    )(x, g, w)


def _norm_matmul_gates(x, g, w_t, wg_t, bg, *, tm, tn):
    rows, n = x.shape[0], w_t.shape[0]
    return pl.pallas_call(
        _norm_matmul_gates_kernel,
        out_shape=(jax.ShapeDtypeStruct((rows, n), BF16),
                   jax.ShapeDtypeStruct((rows, GATE_LANES), F32)),
        grid=(rows // tm, n // tn),
        in_specs=[pl.BlockSpec((tm, D), lambda i, j: (i, 0)),
                  pl.BlockSpec((1, D), lambda i, j: (0, 0)),
                  pl.BlockSpec((tn, D), lambda i, j: (j, 0)),
                  pl.BlockSpec((GATE_LANES, D), lambda i, j: (0, 0)),
                  pl.BlockSpec((1, GATE_LANES), lambda i, j: (0, 0))],
        out_specs=(pl.BlockSpec((tm, tn), lambda i, j: (i, j)),
                   pl.BlockSpec((tm, GATE_LANES), lambda i, j: (i, 0))),
        scratch_shapes=[pltpu.VMEM((tm, D), BF16)],
        compiler_params=_params(("parallel", "arbitrary")),
        name="norm_matmul_gates",
    )(x, g, w_t, wg_t, bg)


def _norm_matmul_gates_cast(x, g, w_t, wg_t, bg, *, n, tn):
    rows = x.shape[0]
    return pl.pallas_call(
        _norm_matmul_gates_cast_kernel,
        out_shape=(jax.ShapeDtypeStruct((rows, n), BF16),
                   jax.ShapeDtypeStruct((rows, GATE_LANES), F32),
                   jax.ShapeDtypeStruct((n, D), BF16)),
        grid=(1, n // tn),
        in_specs=[pl.BlockSpec((rows, D), lambda i, j: (0, 0), pipeline_mode=pl.Buffered(1)),
                  pl.BlockSpec((1, D), lambda i, j: (0, 0)),
                  pl.BlockSpec((None, tn, D), lambda i, j: (0, j, 0)),
                  pl.BlockSpec((GATE_LANES, D), lambda i, j: (0, 0)),
                  pl.BlockSpec((1, GATE_LANES), lambda i, j: (0, 0))],
        out_specs=(pl.BlockSpec((rows, tn), lambda i, j: (0, j)),
                   pl.BlockSpec((rows, GATE_LANES), lambda i, j: (0, 0)),
                   pl.BlockSpec((tn, D), lambda i, j: (j, 0))),
        scratch_shapes=[pltpu.VMEM((rows, D), BF16)],
        compiler_params=_params(("arbitrary", "arbitrary")),
        name="norm_matmul_gates_cast",
    )(x, g, w_t, wg_t, bg)


def _matmul_norm_res_kernel(a_ref, w_ref, g_ref, h_ref, o_ref, wb_ref):
    @pl.when(pl.program_id(0) == 0)
    def _():
        wb_ref[...] = w_ref[...].astype(BF16)

    y = _dot(a_ref[...], wb_ref[...])
    o_ref[...] = h_ref[...] + _rmsnorm(y, g_ref[...])


def _gated_matmul_norm_res_kernel(hm_ref, og_ref, hg_ref, w_ref, g_ref, h_ref, o_ref, wb_ref):
    @pl.when(pl.program_id(0) == 0)
    def _():
        wb_ref[...] = w_ref[...].astype(BF16)

    tm = hm_ref.shape[0]
    for r in range(0, tm, SUB_ROWS):
        rows = slice(r, min(r + SUB_ROWS, tm))
        parts = []
        for hd in range(HEADS):
            cols = slice(hd * DV, (hd + 1) * DV)
            hn = _rmsnorm(hm_ref[rows, cols], hg_ref[:, cols])
            gate = 0.5 * jnp.tanh(0.5 * og_ref[rows, cols].astype(F32)) + 0.5
            parts.append((hn * gate).astype(BF16))
        y = _dot(jnp.concatenate(parts, axis=-1), wb_ref[...])
        o_ref[rows, :] = h_ref[rows, :] + _rmsnorm(y, g_ref[...])


def _gated_matmul_norm_res(hm, proj, head_g, w, g, h, *, tm):
    rows = hm.shape[0]
    return pl.pallas_call(
        _gated_matmul_norm_res_kernel,
        out_shape=jax.ShapeDtypeStruct((rows, D), F32),
        grid=(rows // tm,),
        in_specs=[pl.BlockSpec((tm, HEADS * DV), lambda i: (i, 0)),
                  pl.BlockSpec((tm, HEADS * DV), lambda i: (i, 2)),
                  pl.BlockSpec((1, HEADS * DV), lambda i: (0, 0)),
                  pl.BlockSpec((None, D, D), lambda i: (0, 0, 0), pipeline_mode=pl.Buffered(1)),
                  pl.BlockSpec((1, D), lambda i: (0, 0)),
                  pl.BlockSpec((tm, D), lambda i: (i, 0))],
        out_specs=pl.BlockSpec((tm, D), lambda i: (i, 0)),
        scratch_shapes=[pltpu.VMEM((D, D), BF16)],
        compiler_params=_params(("arbitrary",)),
        name="gated_matmul_norm_res",
    )(hm, proj, head_g, w, g, h)


def _matmul_norm_res(a, w, g, h, *, tm):
    rows = a.shape[0]
    return pl.pallas_call(
        _matmul_norm_res_kernel,
        out_shape=jax.ShapeDtypeStruct((rows, D), F32),
        grid=(rows // tm,),
        in_specs=[pl.BlockSpec((tm, D), lambda i: (i, 0)),
                  pl.BlockSpec((None, D, D), lambda i: (0, 0, 0), pipeline_mode=pl.Buffered(1)),
                  pl.BlockSpec((1, D), lambda i: (0, 0)),
                  pl.BlockSpec((tm, D), lambda i: (i, 0))],
        out_specs=pl.BlockSpec((tm, D), lambda i: (i, 0)),
        scratch_shapes=[pltpu.VMEM((D, D), BF16)],
        compiler_params=_params(("arbitrary",)),
        name="matmul_norm_res",
    )(a, w, g, h)


def _ffn_chunk(f, last, h_ref, gpre_ref, wup_ref, wdown_ref, gpost_ref, o_ref, xn_ref, *, sub, sub_middle):
    tm = h_ref.shape[0]
    subs = [slice(r, min(r + sub, tm)) for r in range(0, tm, sub)]

    def mlp_chunk(xn):
        a = jnp.maximum(_dot(xn, wup_ref[...]), 0.0)
        return _dot((a * a).astype(BF16), wdown_ref[...])

    @pl.when(f == 0)
    def _():
        for rows in subs:
            xn = _rmsnorm(h_ref[rows, :], gpre_ref[...]).astype(BF16)
            xn_ref[rows, :] = xn
            o_ref[rows, :] = mlp_chunk(xn)

    @pl.when(jnp.logical_and(f > 0, f < last))
    def _():
        for rows in (subs if sub_middle else [slice(None)]):
            o_ref[rows, :] += mlp_chunk(xn_ref[rows, :])

    @pl.when(f == last)
    def _():
        for rows in subs:
            y = o_ref[rows, :] + mlp_chunk(xn_ref[rows, :])
            o_ref[rows, :] = h_ref[rows, :] + _rmsnorm(y, gpost_ref[...])


def _ffn_kernel(h_ref, gpre_ref, wup_ref, wdown_ref, gpost_ref, o_ref, xn_ref):
    _ffn_chunk(pl.program_id(1), pl.num_programs(1) - 1, h_ref, gpre_ref, wup_ref, wdown_ref, gpost_ref,
               o_ref, xn_ref, sub=SUB_ROWS, sub_middle=False)


def _ffn_cast_kernel(h_ref, gpre_ref, wup_ref, wdown_ref, gpost_ref, o_ref, wupb_ref, wdownb_ref, xn_ref):
    wupb_ref[...] = wup_ref[...].astype(BF16)
    wdownb_ref[...] = wdown_ref[...].astype(BF16)
    _ffn_chunk(pl.program_id(0), pl.num_programs(0) - 1, h_ref, gpre_ref, wupb_ref, wdownb_ref, gpost_ref,
               o_ref, xn_ref, sub=CAST_SUB_ROWS, sub_middle=True)


def _ffn(h, g_pre, w_up, w_down, g_post, *, tm, tf):
    n_tiles = h.shape[0] // tm
    return pl.pallas_call(
        _ffn_kernel,
        out_shape=jax.ShapeDtypeStruct((n_tiles * tm, D), F32),
        grid=(n_tiles, D_FF // tf),
        in_specs=[pl.BlockSpec((tm, D), lambda i, f: (i, 0)),
                  pl.BlockSpec((1, D), lambda i, f: (0, 0)),
                  pl.BlockSpec((D, tf), lambda i, f: (0, f)),
                  pl.BlockSpec((tf, D), lambda i, f: (f, 0)),
                  pl.BlockSpec((1, D), lambda i, f: (0, 0))],
        out_specs=pl.BlockSpec((tm, D), lambda i, f: (i, 0)),
        scratch_shapes=[pltpu.VMEM((tm, D), BF16)],
        compiler_params=_params(("parallel", "arbitrary")),
        name="ffn",
    )(h, g_pre, w_up, w_down, g_post)


def _ffn_cast(h, g_pre, w_up, w_down, g_post, *, layer, tf):
    rows = h.shape[0]
    single = pl.Buffered(1)
    return pl.pallas_call(
        _ffn_cast_kernel,
        out_shape=(jax.ShapeDtypeStruct((rows, D), F32),
                   jax.ShapeDtypeStruct((D, D_FF), BF16),
                   jax.ShapeDtypeStruct((D_FF, D), BF16)),
        grid=(D_FF // tf,),
        in_specs=[pl.BlockSpec((rows, D), lambda f: (0, 0), pipeline_mode=single),
                  pl.BlockSpec((1, D), lambda f: (0, 0)),
                  pl.BlockSpec((None, D, tf), lambda f: (layer, 0, f)),
                  pl.BlockSpec((None, tf, D), lambda f: (layer, f, 0)),
                  pl.BlockSpec((1, D), lambda f: (0, 0))],
        out_specs=(pl.BlockSpec((rows, D), lambda f: (0, 0), pipeline_mode=single),
                   pl.BlockSpec((D, tf), lambda f: (0, f)),
                   pl.BlockSpec((tf, D), lambda f: (f, 0))),
        scratch_shapes=[pltpu.VMEM((rows, D), BF16)],
        compiler_params=_params(("arbitrary",)),
        name="ffn_cast",
    )(h, g_pre, w_up, w_down, g_post)


def _mlstm_kernel(qk_ref, v_ref, gcol_ref, grow_ref, c0_ref, n0_ref, m0_ref,
                  out_ref, c_ref, n_ref, m_ref, *, seqs, length, single_chunk):
    if single_chunk:
        c_in, n_in, m_in = c0_ref, n0_ref, m0_ref
    else:
        c_in, n_in, m_in = c_ref, n_ref, m_ref

        @pl.when(pl.program_id(1) == 0)
        def _():
            c_ref[...] = c0_ref[...]
            n_ref[...] = n0_ref[...]
            m_ref[...] = m0_ref[...]

    all_pairs = [(s, hd) for s in range(seqs) for hd in range(HEADS)]
    per_pass = 1 if length >= 256 else len(all_pairs)
    for start in range(0, len(all_pairs), per_pass):
        _mlstm_pairs(all_pairs[start:start + per_pass], length, qk_ref, v_ref, gcol_ref, grow_ref,
                     c_in, n_in, m_in, out_ref, c_ref, n_ref, m_ref)


def _mlstm_pairs(pairs, length, qk_ref, v_ref, gcol_ref, grow_ref, c_in, n_in, m_in, out_ref, c_ref, n_ref, m_ref):
    L = length
    t_idx = lax.broadcasted_iota(jnp.int32, (L, L), 0)
    s_idx = lax.broadcasted_iota(jnp.int32, (L, L), 1)
    causal = s_idx <= t_idx

    def rows_of(s):
        return slice(s * L, (s + 1) * L)

    def q_of(s, hd):
        return qk_ref[rows_of(s), hd * DQK:(hd + 1) * DQK]

    def k_of(s, hd):
        return qk_ref[rows_of(s), (HEADS + hd) * DQK:(HEADS + hd + 1) * DQK]

    def v_of(s, hd):
        return v_ref[rows_of(s), hd * DV:(hd + 1) * DV]

    qk = [_dot_t(q_of(s, hd), k_of(s, hd)) for s, hd in pairs]
    qc = [_dot(q_of(s, hd), c_in[s, hd].astype(BF16)) for s, hd in pairs]

    gate_vals = []
    for s, hd in pairs:
        gcol = gcol_ref[rows_of(s), :]
        grow = grow_ref[0, s]
        li_row, lf_row = grow[hd:hd + 1, :], grow[HEADS + hd:HEADS + hd + 1, :]
        li_col, lf_col = gcol[:, hd:hd + 1], gcol[:, HEADS + hd:HEADS + hd + 1]
        m = m_in[s, hd:hd + 1, 0:1]
        b_col = jnp.sum(jnp.where(causal, lf_row, 0.0), axis=1, keepdims=True)
        b_row = jnp.sum(jnp.where(t_idx <= s_idx, lf_col, 0.0), axis=0, keepdims=True)
        b_tot = b_col[L - 1:L, :]
        dmat = jnp.where(causal, b_col - b_row + li_row, -jnp.inf)
        m_inter = b_col + m
        m_t = jnp.maximum(m_inter, jnp.max(dmat, axis=1, keepdims=True))
        p = jnp.exp(dmat - m_t)
        w_inter = jnp.exp(m_inter - m_t)
        m_new = m_t[L - 1:L, :]
        decay = jnp.exp(b_tot - b_col + li_col - m_new)
        carry = jnp.exp(b_tot + m - m_new)
        gate_vals.append((p, w_inter, m_t, m_new, decay, carry))

    scores = [qk[i] * (gate_vals[i][0] * K_SCALE) for i in range(len(pairs))]
    kd = [k_of(s, hd).astype(F32) * (gate_vals[i][4] * K_SCALE) for i, (s, hd) in enumerate(pairs)]
    sv = [_dot(scores[i].astype(BF16), v_of(s, hd)) for i, (s, hd) in enumerate(pairs)]
    kv = [lax.dot_general(kd[i].astype(BF16), v_of(s, hd), (((0,), (0,)), ((), ())),
                          preferred_element_type=F32) for i, (s, hd) in enumerate(pairs)]

    for i, (s, hd) in enumerate(pairs):
        p, w_inter, m_t, m_new, decay, carry = gate_vals[i]
        n_row = n_in[s, hd:hd + 1, :]
        c_new = carry * c_in[s, hd] + kv[i]
        n_new = carry * n_row + jnp.sum(kd[i], axis=0, keepdims=True)
        num = sv[i] + qc[i] * w_inter
        qn = jnp.sum(q_of(s, hd).astype(F32) * n_row, axis=1, keepdims=True)
        den = jnp.sum(scores[i], axis=1, keepdims=True) + w_inter * qn
        c_ref[s, hd] = c_new
        n_ref[s, hd:hd + 1, :] = n_new
        m_ref[s, hd:hd + 1, :] = jnp.broadcast_to(m_new, (1, GATE_LANES))
        out_ref[rows_of(s), hd * DV:(hd + 1) * DV] = num / jnp.maximum(jnp.abs(den), jnp.exp(-m_t))


def _mlstm(proj, gcol, grow, c0, n0, m0, *, row0, n_blocks, seqs, length, n_chunks):
    assert seqs == 1 or n_chunks == 1
    blk = seqs * length
    assert row0 % blk == 0
    b0 = row0 // blk
    n_seq = n_blocks * seqs

    def rows_map(col):
        return lambda b, c: (b0 + b * n_chunks + c, col)

    kernel = functools.partial(_mlstm_kernel, seqs=seqs, length=length, single_chunk=n_chunks == 1)
    return pl.pallas_call(
        kernel,
        out_shape=(jax.ShapeDtypeStruct((n_blocks * n_chunks * blk, HEADS * DV), F32),
                   jax.ShapeDtypeStruct((n_seq, HEADS, DQK, DV), F32),
                   jax.ShapeDtypeStruct((n_seq, HEADS, DQK), F32),
                   jax.ShapeDtypeStruct((n_seq, HEADS, GATE_LANES), F32)),
        grid=(n_blocks, n_chunks),
        in_specs=[pl.BlockSpec((blk, 2 * HEADS * DQK), rows_map(0)),
                  pl.BlockSpec((blk, HEADS * DV), rows_map(1)),
                  pl.BlockSpec((blk, GATE_LANES), rows_map(0)),
                  pl.BlockSpec((1, seqs, 2 * HEADS, length), lambda b, c: (b * n_chunks + c, 0, 0, 0)),
                  pl.BlockSpec((seqs, HEADS, DQK, DV), lambda b, c: (b, 0, 0, 0)),
                  pl.BlockSpec((seqs, HEADS, DQK), lambda b, c: (b, 0, 0)),
                  pl.BlockSpec((seqs, HEADS, GATE_LANES), lambda b, c: (b, 0, 0))],
        out_specs=(pl.BlockSpec((blk, HEADS * DV), lambda b, c: (b * n_chunks + c, 0)),
                   pl.BlockSpec((seqs, HEADS, DQK, DV), lambda b, c: (b, 0, 0, 0)),
                   pl.BlockSpec((seqs, HEADS, DQK), lambda b, c: (b, 0, 0)),
                   pl.BlockSpec((seqs, HEADS, GATE_LANES), lambda b, c: (b, 0, 0))),
        compiler_params=_params(("parallel", "arbitrary")),
        name=f"mlstm_l{length}",
    )(proj, proj, gcol, grow, c0, n0, m0)


def _pool_mix(ext, first, n_rows, pos, wg_ref, scale_ref):
    outs = []
    for g, w in enumerate(POOL_WINDOWS):
        cols = slice(g * POOL_GROUP_DIM, (g + 1) * POOL_GROUP_DIM)
        e = ext[:, cols]
        total, span = e, 1
        while span < w:
            total = total + pltpu.roll(total, span, 0)
            span *= 2
        cnt = jnp.minimum(float(w), pos + 1.0)
        pooled = total[first:first + n_rows] / cnt - e[first:first + n_rows]
        outs.append(_dot(pooled.astype(BF16), wg_ref[g].astype(BF16)))
    mixed = jnp.concatenate(outs, axis=-1) * scale_ref[...]
    return mixed.astype(BF16)


def _pool_rows_kernel(u_ref, prev_ref, first_ref, wg_ref, scale_ref, o_ref, *, tiles_per_seq, pos0):
    tr = u_ref.shape[0]
    tile = pl.program_id(0) % tiles_per_seq
    halo = jnp.where(tile == 0, first_ref[...], prev_ref[...])
    ext = jnp.concatenate([halo, u_ref[...]], axis=0)
    pos = (lax.broadcasted_iota(jnp.int32, (tr, 1), 0) + (tile * tr + pos0)).astype(F32)
    o_ref[...] = _pool_mix(ext, N_META, tr, pos, wg_ref, scale_ref)


def _pool_rows(u, first_src, w_group, scale, *, row0, n_tiles, tr, tiles_per_seq, pos0, first_block):
    hb = tr // N_META
    t0 = row0 // tr
    kernel = functools.partial(_pool_rows_kernel, tiles_per_seq=tiles_per_seq, pos0=pos0)
    return pl.pallas_call(
        kernel,
        out_shape=jax.ShapeDtypeStruct((n_tiles * tr, D), BF16),
        grid=(n_tiles,),
        in_specs=[pl.BlockSpec((tr, D), lambda i: (t0 + i, 0)),
                  pl.BlockSpec((N_META, D), lambda i: (jnp.maximum((t0 + i) * hb - 1, 0), 0)),
                  pl.BlockSpec((N_META, D), lambda i: (first_block, 0)),
                  pl.BlockSpec((None, len(POOL_WINDOWS), POOL_GROUP_DIM, POOL_GROUP_DIM),
                               lambda i: (0, 0, 0, 0)),
                  pl.BlockSpec((1, D), lambda i: (0, 0))],
        out_specs=pl.BlockSpec((tr, D), lambda i: (i, 0)),
        compiler_params=_params(("parallel",)),
        name=f"pool_rows_{tr}",
    )(u, u, first_src, w_group, scale)


def _pool_out_proj_kernel(u_ref, prev_ref, first_ref, wg_ref, scale_ref, w_ref, g_ref, h_ref, o_ref, wb_ref,
                          *, tiles_per_seq, pos0):
    @pl.when(pl.program_id(0) == 0)
    def _():
        wb_ref[...] = w_ref[...].astype(BF16)

    tr = u_ref.shape[0]
    tile = pl.program_id(0) % tiles_per_seq
    halo = jnp.where(tile == 0, first_ref[...], prev_ref[...])
    ext = jnp.concatenate([halo, u_ref[...]], axis=0)
    pos = (lax.broadcasted_iota(jnp.int32, (tr, 1), 0) + (tile * tr + pos0)).astype(F32)
    y = _dot(_pool_mix(ext, N_META, tr, pos, wg_ref, scale_ref), wb_ref[...])
    o_ref[...] = h_ref[...] + _rmsnorm(y, g_ref[...])


def _pool_out_proj(u, first_src, w_group, scale, w, g, h, *, tr, tiles_per_seq, pos0, first_block):
    hb = tr // N_META
    single = pl.Buffered(1)
    kernel = functools.partial(_pool_out_proj_kernel, tiles_per_seq=tiles_per_seq, pos0=pos0)
    return pl.pallas_call(
        kernel,
        out_shape=jax.ShapeDtypeStruct(h.shape, F32),
        grid=(u.shape[0] // tr,),
        in_specs=[pl.BlockSpec((tr, D), lambda i: (i, 0)),
                  pl.BlockSpec((N_META, D), lambda i: (jnp.maximum(i * hb - 1, 0), 0)),
                  pl.BlockSpec((N_META, D), lambda i: (first_block, 0)),
                  pl.BlockSpec((None, len(POOL_WINDOWS), POOL_GROUP_DIM, POOL_GROUP_DIM),
                               lambda i: (0, 0, 0, 0), pipeline_mode=single),
                  pl.BlockSpec((1, D), lambda i: (0, 0)),
                  pl.BlockSpec((None, D, D), lambda i: (0, 0, 0), pipeline_mode=single),
                  pl.BlockSpec((1, D), lambda i: (0, 0)),
                  pl.BlockSpec((tr, D), lambda i: (i, 0))],
        out_specs=pl.BlockSpec((tr, D), lambda i: (i, 0)),
        scratch_shapes=[pltpu.VMEM((D, D), BF16)],
        compiler_params=_params(("arbitrary",)),
        name="pool_out_proj",
    )(u, u, first_src, w_group, scale, w, g, h)


def _pool_seqs_kernel(prefix_ref, u_ref, wg_ref, scale_ref, o_ref, buf_ref, *, pos0):
    g = prefix_ref.shape[0]
    n_new = u_ref.shape[0] // g
    r = 1 + POOL_BUF + n_new
    prefix = prefix_ref[...]
    u3 = u_ref[...].reshape(g, n_new, D)
    ext = jnp.concatenate([jnp.zeros((g, 1, D), F32), prefix, u3], axis=1).reshape(g * r, D)
    buf_ref[...] = jnp.concatenate([prefix, u3], axis=1)[:, n_new:, :]
    outs = []
    pos = (lax.broadcasted_iota(jnp.int32, (g, n_new, 1), 1) + pos0).astype(F32)
    for gi, w in enumerate(POOL_WINDOWS):
        cols = slice(gi * POOL_GROUP_DIM, (gi + 1) * POOL_GROUP_DIM)
        e = ext[:, cols]
        total, span = e, 1
        while span < w:
            total = total + pltpu.roll(total, span, 0)
            span *= 2
        cnt = jnp.minimum(float(w), pos + 1.0)
        tot3 = total.reshape(g, r, POOL_GROUP_DIM)[:, 1 + POOL_BUF:, :]
        e3 = e.reshape(g, r, POOL_GROUP_DIM)[:, 1 + POOL_BUF:, :]
        pooled = (tot3 / cnt - e3).reshape(g * n_new, POOL_GROUP_DIM)
        outs.append(_dot(pooled.astype(BF16), wg_ref[gi].astype(BF16)))
    mixed = jnp.concatenate(outs, axis=-1) * scale_ref[...]
    o_ref[...] = mixed.astype(BF16)


def _pool_seqs(prefix, u, w_group, scale, *, n_new, seqs_per_block, pos0):
    n_seq = prefix.shape[1]
    kernel = functools.partial(_pool_seqs_kernel, pos0=pos0)
    return pl.pallas_call(
        kernel,
        out_shape=(jax.ShapeDtypeStruct((n_seq * n_new, D), BF16),
                   jax.ShapeDtypeStruct((n_seq, POOL_BUF, D), F32)),
        grid=(n_seq // seqs_per_block,),
        in_specs=[pl.BlockSpec((None, seqs_per_block, POOL_BUF, D), lambda i: (0, i, 0, 0)),
                  pl.BlockSpec((seqs_per_block * n_new, D), lambda i: (i, 0)),
                  pl.BlockSpec((None, len(POOL_WINDOWS), POOL_GROUP_DIM, POOL_GROUP_DIM),
                               lambda i: (0, 0, 0, 0)),
                  pl.BlockSpec((1, D), lambda i: (0, 0))],
        out_specs=(pl.BlockSpec((seqs_per_block * n_new, D), lambda i: (i, 0)),
                   pl.BlockSpec((seqs_per_block, POOL_BUF, D), lambda i: (i, 0, 0))),
        compiler_params=_params(("parallel",)),
        name="pool_seqs",
    )(prefix, u, w_group, scale)


def kernel(x_prompt, x_sample, state_mlstm_c, state_mlstm_n, state_mlstm_m, state_pool, meta_tokens,
           norm_mix_pre, norm_mix_post, norm_ffn_pre, norm_ffn_post, mlstm_w_in, mlstm_b_i, mlstm_b_f,
           mlstm_head_norm, mlstm_w_out, pool_w_in, pool_w_group, pool_scale, pool_w_out,
           ffn_w_up, ffn_w_down):
    B, S, _ = x_prompt.shape
    DB, DS, _ = x_sample.shape
    n_p, n_s = B * S, DB * DS
    rows_a = 2 * TM_AUX
    row_m = n_s
    assert n_s + N_META <= rows_a and n_p % TM_PROJ == 0 and n_p % TM == 0
    assert S % CHUNK == 0 and S % POOL_TR == 0 and row_m % N_META == 0

    h_p = x_prompt.reshape(n_p, D)
    h_a = jnp.concatenate([x_sample.reshape(n_s, D), meta_tokens.astype(F32),
                           jnp.zeros((rows_a - n_s - N_META, D), F32)], axis=0)
    pad_a = jnp.zeros((rows_a - n_s - N_META, D), BF16)

    def gain(a):
        return a.reshape(1, D).astype(F32)


    def ffn(layer):
        g_pre, g_post = gain(norm_ffn_pre[layer]), gain(norm_ffn_post[layer])
        out_a, w_up, w_down = _ffn_cast(h_a, g_pre, ffn_w_up.astype(F32), ffn_w_down.astype(F32), g_post,
                                        layer=layer, tf=TF_AUX)
        return _ffn(h_p, g_pre, w_up, w_down, g_post, tm=TM_FFN, tf=TF), out_a

    w_in_t = jnp.swapaxes(mlstm_w_in[:1].astype(F32), 1, 2)
    w_gate_t = jnp.pad(w_in_t[0, PROJ_MAIN:], ((0, GATE_LANES - 2 * HEADS), (0, 0))).astype(BF16)
    b_gate = jnp.pad(jnp.concatenate([mlstm_b_i[0], mlstm_b_f[0]]).astype(F32),
                     (0, GATE_LANES - 2 * HEADS)).reshape(1, GATE_LANES)
    g_mix = gain(norm_mix_pre[0])
    proj_a, gcol_a, w_in_tb = _norm_matmul_gates_cast(h_a, g_mix, w_in_t, w_gate_t, b_gate, n=PROJ_MAIN,
                                                      tn=TN_PROJ)
    proj_p, gcol_p = _norm_matmul_gates(h_p, g_mix, w_in_tb, w_gate_t, b_gate, tm=TM_PROJ, tn=2 * TN_PROJ)

    def gate_rows(gcol, r0, n_blk, seqs, length):
        g8 = gcol[r0:r0 + n_blk * seqs * length, :2 * HEADS]
        return g8.reshape(n_blk, seqs, length, 2 * HEADS).transpose(0, 1, 3, 2)

    head_g = mlstm_head_norm[0].reshape(1, HEADS * DV).astype(F32)
    zc = jnp.zeros((1, HEADS, DQK, DV), F32)
    zn = jnp.zeros((1, HEADS, DQK), F32)
    zm = jnp.zeros((1, HEADS, GATE_LANES), F32)
    mix_m, c_m, n_m, m_m = _mlstm(proj_a, gcol_a, gate_rows(gcol_a, row_m, 1, 1, N_META), zc, zn, zm,
                                  row0=row_m, n_blocks=1, seqs=1, length=N_META, n_chunks=1)
    mix_p, c_p, n_p_, m_p = _mlstm(proj_p, gcol_p, gate_rows(gcol_p, 0, B * (S // CHUNK), 1, CHUNK),
                                   jnp.broadcast_to(c_m, (B,) + c_m.shape[1:]),
                                   jnp.broadcast_to(n_m, (B,) + n_m.shape[1:]),
                                   jnp.broadcast_to(m_m, (B,) + m_m.shape[1:]),
                                   row0=0, n_blocks=B, seqs=1, length=CHUNK, n_chunks=S // CHUNK)
    m0_s = jnp.broadcast_to(state_mlstm_m[0].astype(F32)[:, :, None], (DB, HEADS, GATE_LANES))
    mix_s, c_s, n_s_, m_s = _mlstm(proj_a, gcol_a, gate_rows(gcol_a, 0, DB // SAMPLE_SEQS, SAMPLE_SEQS, DS),
                                   state_mlstm_c[0].astype(F32), state_mlstm_n[0].astype(F32), m0_s,
                                   row0=0, n_blocks=DB // SAMPLE_SEQS, seqs=SAMPLE_SEQS, length=DS, n_chunks=1)
    mix_a = jnp.concatenate([mix_s, mix_m, pad_a.astype(F32)], axis=0)
    w_out, g_post = mlstm_w_out[:1].astype(F32), gain(norm_mix_post[0])
    h_p, h_a = (_gated_matmul_norm_res(mix_p, proj_p, head_g, w_out, g_post, h_p, tm=TM),
                _gated_matmul_norm_res(mix_a, proj_a, head_g, w_out, g_post, h_a, tm=TM_AUX // 2))
    h_p, h_a = ffn(0)

    w_pool_in, g_mix = pool_w_in[:1].astype(F32), gain(norm_mix_pre[1])
    u_p = _norm_matmul(h_p, g_mix, w_pool_in, tm=TM, out_dtype=F32)
    u_a = _norm_matmul(h_a, g_mix, w_pool_in, tm=TM_AUX, out_dtype=F32)
    w_group = pool_w_group[:1].astype(F32)
    p_scale = pool_scale[0].reshape(1, D).astype(F32)
    pmix_m = _pool_rows(u_a, jnp.zeros((N_META, D), F32), w_group, p_scale, row0=row_m, n_tiles=1,
                        tr=N_META, tiles_per_seq=1, pos0=0, first_block=0)
    pmix_s, pool_s = _pool_seqs(state_pool[:1].astype(F32), u_a, w_group, p_scale, n_new=DS,
                                seqs_per_block=POOL_SEQS, pos0=PAST_LEN)
    pmix_a = jnp.concatenate([pmix_s, pmix_m, pad_a], axis=0)
    w_out, g_post = pool_w_out[:1].astype(F32), gain(norm_mix_post[1])
    h_p, h_a = (_pool_out_proj(u_p, u_a, w_group, p_scale, w_out, g_post, h_p, tr=POOL_TR,
                               tiles_per_seq=S // POOL_TR, pos0=N_META, first_block=row_m // N_META),
                _matmul_norm_res(pmix_a, w_out, g_post, h_a, tm=TM_AUX))
    h_p, h_a = ffn(1)

    y_prompt = h_p.reshape(B, S, D)
    y_sample = h_a[:n_s].reshape(DB, DS, D)
    dt_c, dt_n, dt_m, dt_pool = state_mlstm_c.dtype, state_mlstm_n.dtype, state_mlstm_m.dtype, state_pool.dtype
    u_prompt = u_p.reshape(B, S, D)
    return (y_prompt, y_sample,
            c_p[None].astype(dt_c), n_p_[None].astype(dt_n), m_p[None, :, :, 0].astype(dt_m),
            u_prompt[None, :, S - POOL_BUF:].astype(dt_pool),
            c_s[None].astype(dt_c), n_s_[None].astype(dt_n), m_s[None, :, :, 0].astype(dt_m),
            pool_s[None].astype(dt_pool))
```
